```python
import math
import jax, jax.numpy as jnp
from jax import lax
import numpy as np

D_MODEL = 1024
BATCH = 8
SEQ = 8192
DEPTH = 4

GRID_W = 64
CTX_LEN = 256
EPS = 1e-6
N_MOD = 6

GLA_HEADS = 4
GLA_DK = 64
GLA_DV = 128
GLA_KW = GLA_HEADS * GLA_DK
GLA_VW = GLA_HEADS * GLA_DV
GATE_RANK = 16
GATE_TAU = 16.0
CHUNK = 64
ROPE_BASE = 10000.0

NA_HEADS = 8
NA_DH = 64
NA_W = NA_HEADS * NA_DH
NA_KH = 8
NA_KW = 16

N_BRANCH = 2
IN_SPLITS = (GLA_KW, GLA_KW, GLA_VW, 2 * GATE_RANK, GLA_VW, NA_W, NA_W, NA_W, N_BRANCH * D_MODEL)
IN_COLS = 2 * GLA_KW + 2 * GLA_VW + 2 * GATE_RANK + 3 * NA_W + N_BRANCH * D_MODEL

PEER_HEADS = 8
N_KEYS = 128
N_EXPERTS = N_KEYS * N_KEYS
PEER_DK = 256
PEER_TOPK = 16
TOKEN_BLOCK = 128

kernel_name = "hybrid_gla_natten_peer_dit"


def rmsnorm(x, g):
    xf = x.astype(jnp.float32)
    y = xf * lax.rsqrt(jnp.mean(xf * xf, axis=-1, keepdims=True) + EPS)
    return (y * g.astype(jnp.float32)).astype(x.dtype)


def split_cols(t, sizes):
    outs, start = [], 0
    for s in sizes:
        outs.append(t[..., start:start + s])
        start += s
    return outs


def heads(t, n_heads):
    return t.reshape(t.shape[:-1] + (n_heads, t.shape[-1] // n_heads))


def axial_rope(x):
    n = x.shape[1]
    t = jnp.arange(n, dtype=jnp.int32)
    pos = jnp.stack([t // GRID_W, t % GRID_W], axis=0).astype(jnp.float32)
    half = x.shape[-1] // 2
    quarter = half // 2
    freqs = ROPE_BASE ** (-jnp.arange(quarter, dtype=jnp.float32) / quarter)
    ang = pos[:, :, None] * freqs
    cos = jnp.cos(ang).astype(x.dtype)
    sin = jnp.sin(ang).astype(x.dtype)

    def rot(xs, cs, sn):
        x1, x2 = xs[..., :quarter], xs[..., quarter:]
        cs = cs[None, :, None, :]
        sn = sn[None, :, None, :]
        return jnp.concatenate([x1 * cs - x2 * sn, x1 * sn + x2 * cs], axis=-1)

    return jnp.concatenate([rot(x[..., :half], cos[0], sin[0]),
                            rot(x[..., half:], cos[1], sin[1])], axis=-1)


def gla_scan(q, k, v, log_a, s0):
    B, L, H, dk = q.shape
    dv = v.shape[-1]
    n = L // CHUNK

    def chunks(t):
        return t.astype(jnp.float32).reshape(B, n, CHUNK, H, t.shape[-1]).transpose(1, 0, 3, 2, 4)

    mask = jnp.tril(jnp.ones((CHUNK, CHUNK), dtype=bool))[:, :, None]

    def step(s, inp):
        qi, ki, vi, ai = inp
        b = jnp.cumsum(ai, axis=2)
        rel = b[:, :, :, None, :] - b[:, :, None, :, :]
        decay = jnp.where(mask, jnp.exp(jnp.minimum(rel, 0.0)), 0.0)
        attn = jnp.einsum('bhik,bhjk,bhijk->bhij', qi, ki, decay)
        o = (jnp.einsum('bhij,bhjv->bhiv', attn, vi)
             + jnp.einsum('bhik,bhkv->bhiv', qi * jnp.exp(b), s))
        b_end = b[:, :, -1, :]
        s = (jnp.exp(b_end)[..., None] * s
             + jnp.einsum('bhjk,bhjv->bhkv', ki * jnp.exp(b_end[:, :, None, :] - b), vi))
        return s, o

    s_fin, o = lax.scan(step, s0, (chunks(q), chunks(k), chunks(v), chunks(log_a)))
    o = o.transpose(1, 0, 3, 2, 4).reshape(B, L, H, dv)
    return o.astype(v.dtype), s_fin


def flip(t):
    return jnp.flip(t, axis=1)


def na_latent(q, k, v, k_ctx, v_ctx, rpb):
    B, L, H, dh = q.shape
    rows = L // GRID_W
    kh = min(NA_KH, rows)
    scale = NA_DH ** -0.5
    qg = q.reshape(B, rows, GRID_W, H, dh)
    kg = k.reshape(B, rows, GRID_W, H, dh)
    vg = v.reshape(B, rows, GRID_W, H, dh)
    col = jnp.arange(GRID_W, dtype=jnp.int32)
    c0 = jnp.clip(col - NA_KW // 2, 0, GRID_W - NA_KW)
    col_idx = c0[:, None] + jnp.arange(NA_KW, dtype=jnp.int32)[None, :]
    col_off = col_idx - col[:, None] + (NA_KW - 1)

    def one_row(r):
        r0 = jnp.clip(r - kh // 2, 0, rows - kh)
        k_rows = lax.dynamic_slice_in_dim(kg, r0, kh, axis=1)
        v_rows = lax.dynamic_slice_in_dim(vg, r0, kh, axis=1)
        k_win = k_rows[:, :, col_idx]
        v_win = v_rows[:, :, col_idx]
        q_r = lax.dynamic_index_in_dim(qg, r, axis=1, keepdims=False)
        row_off = r0 + jnp.arange(kh, dtype=jnp.int32) - r + (NA_KH - 1)
        bias = rpb[:, row_off[None, :, None], col_off[:, None, :]]
        s_win = (jnp.einsum('bwhd,bawihd->bhwai', q_r, k_win).astype(jnp.float32) * scale
                 + bias.astype(jnp.float32)[None])
        s_ctx = jnp.einsum('bwhd,bchd->bhwc', q_r, k_ctx).astype(jnp.float32) * scale
        s_all = jnp.concatenate([s_win.reshape(B, H, GRID_W, kh * NA_KW), s_ctx], axis=-1)
        p = jax.nn.softmax(s_all, axis=-1).astype(v.dtype)
        p_win = p[..., :kh * NA_KW].reshape(B, H, GRID_W, kh, NA_KW)
        p_ctx = p[..., kh * NA_KW:]
        return (jnp.einsum('bhwai,bawihd->bwhd', p_win, v_win)
                + jnp.einsum('bhwc,bchd->bwhd', p_ctx, v_ctx))

    out = lax.map(one_row, jnp.arange(rows, dtype=jnp.int32))
    return out.transpose(1, 0, 2, 3, 4).reshape(B, L, H * dh)


def na_context(q, k, v):
    B, Lc, H, dh = q.shape
    s = jnp.einsum('bqhd,bkhd->bhqk', q, k).astype(jnp.float32) * (NA_DH ** -0.5)
    p = jax.nn.softmax(s, axis=-1).astype(v.dtype)
    return jnp.einsum('bhqk,bkhd->bqhd', p, v).reshape(B, Lc, H * dh)


def gla_inputs(parts, rope, w_af, b_af, w_ab, b_ab):
    q, k, v, z, r = parts[:5]
    q = heads(q, GLA_HEADS)
    k = heads(k, GLA_HEADS)
    if rope:
        q, k = axial_rope(q), axial_rope(k)
    q = q * (GLA_DK ** -0.5)
    v = heads(v, GLA_HEADS)
    zf, zb = z[..., :GATE_RANK], z[..., GATE_RANK:]
    la_f = heads(jax.nn.log_sigmoid((zf @ w_af + b_af).astype(jnp.float32)) / GATE_TAU, GLA_HEADS)
    la_b = heads(jax.nn.log_sigmoid((zb @ w_ab + b_ab).astype(jnp.float32)) / GATE_TAU, GLA_HEADS)
    return q, k, v, la_f, la_b, r


def gla_out(o, r, g_gla, w_pg):
    o = rmsnorm(o, g_gla)
    o = o.reshape(o.shape[:2] + (GLA_VW,)) * jax.nn.silu(r)
    return o @ w_pg


def token_mixer(h, hc, w_in, w_af, b_af, w_ab, b_ab, g_gla, rpb, w_pg, w_pn, w_o, with_ctx_out):
    p = split_cols(h @ w_in, IN_SPLITS)
    pc = split_cols(hc @ w_in, IN_SPLITS)
    B = h.shape[0]

    qc, kc, vc, lafc, labc, rc = gla_inputs(pc, False, w_af, b_af, w_ab, b_ab)
    ql, kl, vl, lafl, labl, rl = gla_inputs(p, True, w_af, b_af, w_ab, b_ab)
    s0 = jnp.zeros((B, GLA_HEADS, GLA_DK, GLA_DV), jnp.float32)
    o_cf, s_cf = gla_scan(qc, kc, vc, lafc, s0)
    o_cb, s_cb = gla_scan(flip(qc), flip(kc), flip(vc), flip(labc), s0)
    o_lf, _ = gla_scan(ql, kl, vl, lafl, s_cf)
    o_lb, _ = gla_scan(flip(ql), flip(kl), flip(vl), flip(labl), s_cb)
    ya = gla_out(o_lf + flip(o_lb), rl, g_gla, w_pg)

    nq, nk, nv = (heads(t, NA_HEADS) for t in p[5:8])
    nkc, nvc = heads(pc[6], NA_HEADS), heads(pc[7], NA_HEADS)
    yb = na_latent(nq, nk, nv, nkc, nvc, rpb) @ w_pn

    gates = p[8]
    y = (jax.nn.sigmoid(gates[..., :D_MODEL]) * ya + jax.nn.sigmoid(gates[..., D_MODEL:]) * yb) @ w_o

    if not with_ctx_out:
        return y, None
    ya_c = gla_out(o_cf + flip(o_cb), rc, g_gla, w_pg)
    yb_c = na_context(heads(pc[5], NA_HEADS), nkc, nvc) @ w_pn
    gates_c = pc[8]
    yc = (jax.nn.sigmoid(gates_c[..., :D_MODEL]) * ya_c + jax.nn.sigmoid(gates_c[..., D_MODEL:]) * yb_c) @ w_o
    return y, yc


def peer(h, w_query, sub_keys, expert_u, expert_v):
    shape = h.shape
    tok = h.reshape(-1, TOKEN_BLOCK, shape[-1])

    def block(xb):
        tb = xb.shape[0]
        qb = (xb @ w_query).reshape(tb, PEER_HEADS, 2, PEER_DK // 2)
        s = jnp.einsum('thpd,hpnd->thpn', qb, sub_keys).astype(jnp.float32)
        s_top, i_top = lax.top_k(s, PEER_TOPK)
        cand = s_top[:, :, 0, :, None] + s_top[:, :, 1, None, :]
        cand_idx = i_top[:, :, 0, :, None] * N_KEYS + i_top[:, :, 1, None, :]
        best, pos = lax.top_k(cand.reshape(tb, PEER_HEADS, PEER_TOPK * PEER_TOPK), PEER_TOPK)
        experts = jnp.take_along_axis(cand_idx.reshape(tb, PEER_HEADS, PEER_TOPK * PEER_TOPK), pos, axis=-1)
        g = jax.nn.softmax(best, axis=-1).reshape(tb, PEER_HEADS * PEER_TOPK).astype(xb.dtype)
        experts = experts.reshape(tb, PEER_HEADS * PEER_TOPK)
        u = jnp.take(expert_u, experts, axis=0)
        v = jnp.take(expert_v, experts, axis=0)
        act = jax.nn.gelu(jnp.einsum('td,ted->te', xb, u))
        return jnp.einsum('te,ted->td', act * g, v)

    return lax.map(block, tok).reshape(shape)


def setup_inputs(seed: int = 0) -> dict:
    key = jax.random.key(seed)
    ks = jax.random.split(key, 32)
    D = D_MODEL
    f32 = jnp.float32

    def nrm(k, shape, s):
        return jax.random.normal(k, shape, f32) * s

    return {
        "x": nrm(ks[0], (BATCH, SEQ, D), 1.0),
        "c": nrm(ks[1], (BATCH, D), 1.0),
        "ctx": nrm(ks[2], (BATCH, CTX_LEN, D), 1.0),
        "c_ctx": nrm(ks[3], (D,), 1.0),
        "w_mod": nrm(ks[4], (DEPTH, D, N_MOD * D), 0.5 * D ** -0.5),
        "b_mod": nrm(ks[5], (DEPTH, N_MOD * D), 0.02),
        "g_norm1": 1.0 + nrm(ks[6], (DEPTH, D), 0.02),
        "w_in": nrm(ks[7], (DEPTH, D, IN_COLS), D ** -0.5),
        "w_alpha_f": nrm(ks[8], (DEPTH, GATE_RANK, GLA_KW), GATE_RANK ** -0.5),
        "b_alpha_f": nrm(ks[9], (DEPTH, GLA_KW), 0.1),
        "w_alpha_b": nrm(ks[10], (DEPTH, GATE_RANK, GLA_KW), GATE_RANK ** -0.5),
        "b_alpha_b": nrm(ks[11], (DEPTH, GLA_KW), 0.1),
        "g_gla": 1.0 + nrm(ks[12], (DEPTH, GLA_DV), 0.02),
        "rpb": nrm(ks[13], (DEPTH, NA_HEADS, 2 * NA_KH - 1, 2 * NA_KW - 1), 0.1),
        "w_proj_gla": nrm(ks[14], (DEPTH, GLA_VW, D), GLA_VW ** -0.5),
        "w_proj_na": nrm(ks[15], (DEPTH, NA_W, D), NA_W ** -0.5),
        "w_out": nrm(ks[16], (DEPTH, D, D), D ** -0.5),
        "g_norm2": 1.0 + nrm(ks[17], (DEPTH, D), 0.02),
        "w_query": nrm(ks[18], (DEPTH, D, PEER_HEADS * PEER_DK), D ** -0.5),
        "sub_keys": nrm(ks[19], (DEPTH, PEER_HEADS, 2, N_KEYS, PEER_DK // 2), (PEER_DK // 2) ** -0.5),
        "expert_u": nrm(ks[20], (DEPTH, N_EXPERTS, D), D ** -0.5),
        "expert_v": nrm(ks[21], (DEPTH, N_EXPERTS, D), PEER_HEADS ** -0.5),
        "g_final": 1.0 + nrm(ks[22], (D,), 0.02),
    }


def reference(x, c, ctx, c_ctx, w_mod, b_mod, g_norm1, w_in, w_alpha_f, b_alpha_f, w_alpha_b, b_alpha_b,
              g_gla, rpb, w_proj_gla, w_proj_na, w_out, g_norm2, w_query, sub_keys, expert_u, expert_v,
              g_final):
    c_act = jax.nn.silu(c)
    cc_act = jax.nn.silu(c_ctx)
    xc = ctx
    for i in range(DEPTH):
        last = i == DEPTH - 1
        mod = c_act @ w_mod[i] + b_mod[i]
        mod_c = cc_act @ w_mod[i] + b_mod[i]
        sh1, sc1, ga1, sh2, sc2, ga2 = jnp.split(mod[:, None, :], N_MOD, axis=-1)
        csh1, csc1, cga1, csh2, csc2, cga2 = jnp.split(mod_c, N_MOD, axis=-1)

        h = rmsnorm(x, g_norm1[i]) * (1 + sc1) + sh1
        hc = rmsnorm(xc, g_norm1[i]) * (1 + csc1) + csh1
        y, yc = token_mixer(h, hc, w_in[i], w_alpha_f[i], b_alpha_f[i], w_alpha_b[i], b_alpha_b[i],
                            g_gla[i], rpb[i], w_proj_gla[i], w_proj_na[i], w_out[i], not last)
        x = x + ga1 * y
        h2 = rmsnorm(x, g_norm2[i]) * (1 + sc2) + sh2
        x = x + ga2 * peer(h2, w_query[i], sub_keys[i], expert_u[i], expert_v[i])

        if not last:
            xc = xc + cga1 * yc
            hc2 = rmsnorm(xc, g_norm2[i]) * (1 + csc2) + csh2
            xc = xc + cga2 * peer(hc2, w_query[i], sub_keys[i], expert_u[i], expert_v[i])
    return rmsnorm(x, g_final)
```

```python
import functools

import jax
import jax.numpy as jnp
from jax import lax
from jax.experimental import pallas as pl
from jax.experimental.pallas import tpu as pltpu

F32 = jnp.float32
BF16 = jnp.bfloat16
I32 = jnp.int32
U32 = jnp.uint32
HIGHEST = lax.Precision.HIGHEST

GRID_W = 64
EPS = 1e-6
N_MOD = 6
GLA_HEADS = 4
GLA_DK = 64
GLA_DV = 128
GLA_KW = GLA_HEADS * GLA_DK
GLA_VW = GLA_HEADS * GLA_DV
GATE_RANK = 16
GATE_TAU = 16.0
CHUNK = 64
ROPE_BASE = 10000.0
NA_HEADS = 8
NA_DH = 64
NA_W = NA_HEADS * NA_DH
NA_KH = 8
NA_KW = 16
PEER_HEADS = 8
N_KEYS = 128
PEER_DK = 256
PEER_TOPK = 16

LANE = 128
SUBLANE = 8
VREG_ELEMS = LANE * SUBLANE

ROW_TILE = 256
GLA_BLOCK = 256
PEER_TOKENS = 128
MASK_NEG = -1e30

_NT = (((1,), (1,)), ((), ()))


def _cp(sem, vmem_mb):
    return pltpu.CompilerParams(dimension_semantics=sem, vmem_limit_bytes=vmem_mb * 1024 * 1024)


def _norm_mod(x, g, shift, scale):
    ms = jnp.mean(x * x, axis=-1, keepdims=True)
    return (x * lax.rsqrt(ms + EPS) * g) * (1.0 + scale) + shift


def _mod_kernel(c_ref, w_ref, b_ref, o_ref):
    a = c_ref[...]
    a = a * jax.nn.sigmoid(a)
    o_ref[0] = jnp.dot(a, w_ref[0], preferred_element_type=F32, precision=HIGHEST) + b_ref[0]


def _modulation(cin, w_mod, b_mod):
    depth, d, n = w_mod.shape
    rows = cin.shape[0]
    tn = n // 4
    return pl.pallas_call(
        _mod_kernel,
        grid=(depth, n // tn),
        in_specs=[pl.BlockSpec((rows, d), lambda l, j: (0, 0)),
                  pl.BlockSpec((1, d, tn), lambda l, j: (l, 0, j)),
                  pl.BlockSpec((1, 1, tn), lambda l, j: (l, 0, j))],
        out_specs=pl.BlockSpec((1, rows, tn), lambda l, j: (l, 0, j)),
        out_shape=jax.ShapeDtypeStruct((depth, rows, n), F32),
        compiler_params=_cp(("arbitrary", "arbitrary"), 40),
        name="modulation",
    )(cin, w_mod, b_mod.reshape(depth, 1, n))


_PROJ_OUTS = (("qk", 2 * GLA_KW, F32), ("v", GLA_VW, F32), ("r", GLA_VW, F32),
              ("nq", NA_W, BF16), ("nk", NA_W, BF16), ("nv", NA_W, BF16),
              ("gates", None, F32), ("z", LANE, F32))


def _proj_kernel(x_ref, mod_ref, g_ref, w_ref, *out_refs):
    m = mod_ref[0, 0]
    h = _norm_mod(x_ref[0], g_ref[...], m[0:1], m[1:2]).astype(BF16)
    col = 0
    for ref in out_refs:
        n = ref.shape[-1]
        ref[0] = jnp.dot(h, w_ref[:, col:col + n], preferred_element_type=F32).astype(ref.dtype)
        col += n


def _project(xa, modseg, g, w_perm, n_ctx_tiles):
    b, t, d = xa.shape
    outs = [(name, (2 * d if n is None else n), dt) for name, n, dt in _PROJ_OUTS]
    assert sum(n for _, n, _ in outs) == w_perm.shape[1]
    return pl.pallas_call(
        _proj_kernel,
        grid=(b, t // ROW_TILE),
        in_specs=[pl.BlockSpec((1, ROW_TILE, d), lambda i, j: (i, j, 0)),
                  pl.BlockSpec((1, 1, SUBLANE, d), lambda i, j: (i, (j >= n_ctx_tiles).astype(I32), 0, 0)),
                  pl.BlockSpec((1, d), lambda i, j: (0, 0)),
                  pl.BlockSpec(w_perm.shape, lambda i, j: (0, 0))],
        out_specs=[pl.BlockSpec((1, ROW_TILE, n), lambda i, j: (i, j, 0)) for _, n, _ in outs],
        out_shape=[jax.ShapeDtypeStruct((b, t, n), dt) for _, n, dt in outs],
        compiler_params=_cp(("parallel", "arbitrary"), 48),
        name="in_proj",
    )(xa, modseg, g, w_perm)


def _gla_kernel(qk_ref, v_ref, z_ref, cs_ref, sn_ref, wa_ref, ba_ref, o_ref, st_ref, *, reverse):
    @pl.when(pl.program_id(1) == 0)
    def _init():
        st_ref[...] = jnp.zeros_like(st_ref)

    kw2 = 2 * GLA_KW
    blk = qk_ref.shape[1]
    qk = qk_ref[0]
    lane = lax.broadcasted_iota(I32, (1, kw2), 1)
    first = (lane % 32) < 16
    partner = jnp.where(first, pltpu.roll(qk, kw2 - 16, 1), pltpu.roll(qk, 16, 1))
    qk = qk * cs_ref[...] + partner * sn_ref[...]
    zz = jnp.dot(z_ref[0], wa_ref[...], preferred_element_type=F32, precision=HIGHEST) + ba_ref[...]
    la = (jnp.minimum(zz, 0.0) - jnp.log1p(jnp.exp(-jnp.abs(zz)))) * (1.0 / GATE_TAU)

    ri = lax.broadcasted_iota(I32, (CHUNK, CHUNK), 0)
    ci = lax.broadcasted_iota(I32, (CHUNK, CHUNK), 1)
    keep = (ci >= ri) if reverse else (ci <= ri)
    tri = keep.astype(F32)
    khead = lax.broadcasted_iota(I32, (1, GLA_KW), 1) // GLA_DK
    vrow = lax.broadcasted_iota(I32, (GLA_VW, GLA_KW), 0) // GLA_DV
    kcol = lax.broadcasted_iota(I32, (GLA_VW, GLA_KW), 1) // GLA_DK
    same_head = vrow == kcol

    order = range(blk // CHUNK)
    if reverse:
        order = reversed(order)
    for c in order:
        lo = c * CHUNK
        q = qk[lo:lo + CHUNK, :GLA_KW]
        k = qk[lo:lo + CHUNK, GLA_KW:]
        v = v_ref[0, lo:lo + CHUNK, :].astype(BF16)
        bcum = jnp.dot(tri, la[lo:lo + CHUNK], preferred_element_type=F32, precision=HIGHEST)
        bend = bcum[0:1] if reverse else bcum[CHUNK - 1:CHUNK]
        qd = q * jnp.exp(bcum)
        kd = (k * jnp.exp(-bcum)).astype(BF16)
        kd2 = (k * jnp.exp(bend - bcum)).astype(BF16)
        st = st_ref[...]
        o_inter = lax.dot_general(qd.astype(BF16), st.astype(BF16), _NT, preferred_element_type=F32)
        for h in range(GLA_HEADS):
            qm = jnp.where(khead == h, qd, 0.0).astype(BF16)
            s = lax.dot_general(qm, kd, _NT, preferred_element_type=F32)
            s = jnp.where(keep, s, 0.0).astype(BF16)
            vs = slice(h * GLA_DV, (h + 1) * GLA_DV)
            o_ref[0, lo:lo + CHUNK, vs] = (jnp.dot(s, v[:, vs], preferred_element_type=F32)
                                           + o_inter[:, vs])
        upd = jnp.dot(v.T, kd2, preferred_element_type=F32)
        st_ref[...] = st * jnp.exp(bend) + jnp.where(same_head, upd, 0.0)


def _gla(qk, v, z, cs, sn, wa, ba, n_ctx_blocks, reverse):
    b, t, _ = qk.shape
    nblk = t // GLA_BLOCK

    if reverse:
        def blk(j):
            return jnp.where(j < n_ctx_blocks, n_ctx_blocks - 1 - j, nblk - 1 - (j - n_ctx_blocks))
    else:
        def blk(j):
            return j

    tok = lambda n: pl.BlockSpec((1, GLA_BLOCK, n), lambda i, j: (i, blk(j), 0))
    tab = lambda n: pl.BlockSpec((GLA_BLOCK, n), lambda i, j: (blk(j), 0))
    return pl.pallas_call(
        functools.partial(_gla_kernel, reverse=reverse),
        grid=(b, nblk),
        in_specs=[tok(2 * GLA_KW), tok(GLA_VW), tok(LANE), tab(2 * GLA_KW), tab(2 * GLA_KW),
                  pl.BlockSpec((LANE, GLA_KW), lambda i, j: (0, 0)),
                  pl.BlockSpec((1, GLA_KW), lambda i, j: (0, 0))],
        out_specs=tok(GLA_VW),
        out_shape=jax.ShapeDtypeStruct((b, t, GLA_VW), F32),
        scratch_shapes=[pltpu.VMEM((GLA_VW, GLA_KW), F32)],
        compiler_params=_cp(("parallel", "arbitrary"), 40),
        name="gla_bwd" if reverse else "gla_fwd",
    )(qk, v, z, cs, sn, wa, ba)


def _na_kernel(*refs):
    q_ref = refs[0]
    k_refs = refs[1:1 + NA_KH]
    v_refs = refs[1 + NA_KH:1 + 2 * NA_KH]
    kc_ref, vc_ref, tb_ref, o_ref = refs[1 + 2 * NA_KH:]
    scale = NA_DH ** -0.5
    half = lax.broadcasted_iota(I32, (1, LANE), 1) // NA_DH
    for p in range(NA_HEADS // 2):
        ls = slice(p * LANE, (p + 1) * LANE)
        qp = q_ref[0, :, ls]
        kw = jnp.concatenate([r[0, :, ls] for r in k_refs], axis=0)
        vw = jnp.concatenate([r[0, :, ls] for r in v_refs], axis=0)
        kc = kc_ref[0, :, ls]
        vc = vc_ref[0, :, ls]
        outs = []
        for s in range(2):
            qm = jnp.where(half == s, qp, jnp.zeros_like(qp))
            sw = lax.dot_general(qm, kw, _NT, preferred_element_type=F32) * scale + tb_ref[0, 2 * p + s]
            sc = lax.dot_general(qm, kc, _NT, preferred_element_type=F32) * scale
            m = jnp.maximum(jnp.max(sw, axis=-1, keepdims=True), jnp.max(sc, axis=-1, keepdims=True))
            pw = jnp.exp(sw - m)
            pc = jnp.exp(sc - m)
            den = jnp.sum(pw, axis=-1, keepdims=True) + jnp.sum(pc, axis=-1, keepdims=True)
            o = (jnp.dot(pw.astype(BF16), vw, preferred_element_type=F32)
                 + jnp.dot(pc.astype(BF16), vc, preferred_element_type=F32))
            outs.append(o / den)
        o_ref[0, :, ls] = jnp.where(half == 0, outs[0], outs[1]).astype(o_ref.dtype)


def _na_bias_tables(rpb):
    w = jnp.arange(GRID_W, dtype=I32)
    c0 = jnp.clip(w - NA_KW // 2, 0, GRID_W - NA_KW)
    cp = w[None, :]
    valid = (cp >= c0[:, None]) & (cp < c0[:, None] + NA_KW)
    off = jnp.clip(cp - w[:, None] + (NA_KW - 1), 0, 2 * NA_KW - 2)
    t = jnp.where(valid[None, None], rpb[:, :, off], MASK_NEG)
    tabs = [jnp.concatenate([t[:, s + a] for a in range(NA_KH)], axis=-1) for s in range(NA_KH)]
    tabs.append(jnp.full_like(tabs[0], MASK_NEG))
    return jnp.stack(tabs, axis=0).astype(F32)


def _na(nq, nk, nv, tables, lc):
    b, t, _ = nq.shape
    n_ctx = lc // GRID_W
    rows = (t - lc) // GRID_W
    assert rows >= NA_KH

    def row0(j):
        r = jnp.maximum(j - n_ctx, 0)
        return r, jnp.clip(r - NA_KH // 2, 0, rows - NA_KH)

    def win(a):
        return pl.BlockSpec((1, GRID_W, NA_W), lambda i, j: (i, n_ctx + row0(j)[1] + a, 0))

    def tab_idx(i, j):
        r, r0 = row0(j)
        return (jnp.where(j < n_ctx, NA_KH, r0 - r + NA_KH - 1), 0, 0, 0)

    return pl.pallas_call(
        _na_kernel,
        grid=(b, t // GRID_W),
        in_specs=([pl.BlockSpec((1, GRID_W, NA_W), lambda i, j: (i, j, 0))]
                  + [win(a) for a in range(NA_KH)] + [win(a) for a in range(NA_KH)]
                  + [pl.BlockSpec((1, lc, NA_W), lambda i, j: (i, 0, 0))] * 2
                  + [pl.BlockSpec((1, NA_HEADS, GRID_W, NA_KH * GRID_W), tab_idx)]),
        out_specs=pl.BlockSpec((1, GRID_W, NA_W), lambda i, j: (i, j, 0)),
        out_shape=jax.ShapeDtypeStruct((b, t, NA_W), BF16),
        compiler_params=_cp(("parallel", "arbitrary"), 40),
        name="natten",
    )(nq, *([nk] * NA_KH), *([nv] * NA_KH), nk, nv, tables)


def _merge_kernel(x_ref, of_ref, ob_ref, r_ref, na_ref, gt_ref, mod_ref, gg_ref, wpg_ref, wpn_ref, wo_ref,
                  o_ref):
    d = x_ref.shape[-1]
    o = of_ref[0] + ob_ref[0]
    r = r_ref[0]
    parts = []
    for h in range(GLA_HEADS):
        oh = o[:, h * GLA_DV:(h + 1) * GLA_DV]
        ms = jnp.mean(oh * oh, axis=-1, keepdims=True)
        parts.append(oh * lax.rsqrt(ms + EPS) * gg_ref[...])
    on = jnp.concatenate(parts, axis=-1) * (r * jax.nn.sigmoid(r))
    ya = jnp.dot(on.astype(BF16), wpg_ref[...], preferred_element_type=F32)
    yb = jnp.dot(na_ref[0], wpn_ref[...], preferred_element_type=F32)
    gt = gt_ref[0]
    y = jax.nn.sigmoid(gt[:, :d]) * ya + jax.nn.sigmoid(gt[:, d:]) * yb
    y = jnp.dot(y.astype(BF16), wo_ref[...], preferred_element_type=F32)
    o_ref[0] = x_ref[0] + mod_ref[0, 0][2:3] * y


def _merge(xa, o_f, o_b, r, na, gates, modseg, g_gla, wpg, wpn, wo, n_ctx_tiles):
    b, t, d = xa.shape
    tok = lambda n: pl.BlockSpec((1, ROW_TILE, n), lambda i, j: (i, j, 0))
    full = lambda a: pl.BlockSpec(a.shape, lambda i, j: (0,) * a.ndim)
    return pl.pallas_call(
        _merge_kernel,
        grid=(b, t // ROW_TILE),
        in_specs=[tok(d), tok(GLA_VW), tok(GLA_VW), tok(GLA_VW), tok(NA_W), tok(2 * d),
                  pl.BlockSpec((1, 1, SUBLANE, d), lambda i, j: (i, (j >= n_ctx_tiles).astype(I32), 0, 0)),
                  full(g_gla), full(wpg), full(wpn), full(wo)],
        out_specs=tok(d),
        out_shape=jax.ShapeDtypeStruct((b, t, d), F32),
        compiler_params=_cp(("parallel", "arbitrary"), 40),
        name="merge",
    )(xa, o_f, o_b, r, na, gates, modseg, g_gla, wpg, wpn, wo)


def _topk_rows(s, ids, k):
    rows = s.shape[0]
    rid = lax.broadcasted_iota(I32, s.shape, 0).astype(F32)
    vals, out_ids = [], []
    for _ in range(k):
        m = jnp.max(s, axis=0, keepdims=True)
        pos = jnp.min(jnp.where(s == m, rid, float(rows)), axis=0, keepdims=True)
        hit = rid == pos
        vals.append(m)
        out_ids.append(pos if ids is None else jnp.max(jnp.where(hit, ids, -1.0), axis=0, keepdims=True))
        s = jnp.where(hit, -jnp.inf, s)
    return jnp.concatenate(vals, axis=0), jnp.concatenate(out_ids, axis=0)


def _peer1_kernel(x_ref, mod_ref, g_ref, wq_ref, sk_ref, idx_ref, gate_ref):
    m = mod_ref[0, 0]
    h = _norm_mod(x_ref[0], g_ref[...], m[3:4], m[4:5]).astype(BF16)
    half = PEER_DK // 2
    for hd in range(PEER_HEADS):
        tops = []
        for p in range(2):
            col = (2 * hd + p) * half
            qp = jnp.dot(h, wq_ref[:, col:col + half], preferred_element_type=F32).astype(BF16)
            st = lax.dot_general(sk_ref[hd, p], qp, _NT, preferred_element_type=F32)
            tops.append(_topk_rows(st, None, PEER_TOPK))
        (v0, i0), (v1, i1) = tops
        cand = jnp.concatenate([v0[a:a + 1] + v1 for a in range(PEER_TOPK)], axis=0)
        cidx = jnp.concatenate([i0[a:a + 1] * float(N_KEYS) + i1 for a in range(PEER_TOPK)], axis=0)
        best, experts = _topk_rows(cand, cidx, PEER_TOPK)
        e = jnp.exp(best - best[0:1])
        rs = slice(hd * PEER_TOPK, (hd + 1) * PEER_TOPK)
        gate_ref[0, rs, :] = e / jnp.sum(e, axis=0, keepdims=True)
        idx_ref[0, rs, :] = experts.astype(I32)


def _peer_select(xa, modseg, g, wq, sk, n_ctx_tiles):
    b, t, d = xa.shape
    ne = PEER_HEADS * PEER_TOPK
    return pl.pallas_call(
        _peer1_kernel,
        grid=(b, t // ROW_TILE),
        in_specs=[pl.BlockSpec((1, ROW_TILE, d), lambda i, j: (i, j, 0)),
                  pl.BlockSpec((1, 1, SUBLANE, d), lambda i, j: (i, (j >= n_ctx_tiles).astype(I32), 0, 0)),
                  pl.BlockSpec((1, d), lambda i, j: (0, 0)),
                  pl.BlockSpec(wq.shape, lambda i, j: (0, 0)),
                  pl.BlockSpec(sk.shape, lambda i, j: (0, 0, 0, 0))],
        out_specs=[pl.BlockSpec((1, ne, ROW_TILE), lambda i, j: (i, 0, j))] * 2,
        out_shape=[jax.ShapeDtypeStruct((b, ne, t), I32), jax.ShapeDtypeStruct((b, ne, t), F32)],
        compiler_params=_cp(("parallel", "arbitrary"), 40),
        name="peer_select",
    )(xa, modseg, g, wq, sk)


_BITREV8 = (0, 4, 2, 6, 1, 5, 3, 7)


def _expert_row(tbl_ref, r):
    word = tbl_ref[lax.shift_right_logical(r, 1)]
    sh = ((1 - (r & 1)) * 16).astype(U32)
    word = lax.shift_left(word, jnp.broadcast_to(sh, word.shape)) & jnp.uint32(0xFFFF0000)
    return lax.bitcast_convert_type(word, F32)


def _sublane_sums(prods):
    sub = lax.broadcasted_iota(I32, (SUBLANE, LANE), 0)
    lvl = prods
    for stride in (4, 2, 1):
        low = (sub % (2 * stride)) < stride
        nxt = []
        for a, bb in zip(lvl[0::2], lvl[1::2]):
            nxt.append(jnp.where(low, a + pltpu.roll(a, SUBLANE - stride, 0), bb + pltpu.roll(bb, stride, 0)))
        lvl = nxt
    return lvl[0]


def _peer2_kernel(idx_ref, x_ref, mod_ref, g_ref, gate_ref, tbl_ref, coef_ref, act_ref):
    ne, tb = act_ref.shape
    d = x_ref.shape[1] * x_ref.shape[2]
    a = g_ref[0] * (1.0 + mod_ref[4])
    shift = mod_ref[3]
    ones = jnp.ones((LANE, LANE), F32)
    lane_t = lax.broadcasted_iota(I32, (ne, tb), 1)
    act_ref[...] = jnp.zeros_like(act_ref)

    def token(t, carry):
        x = x_ref[t]
        ms = jnp.sum(x * x, keepdims=True) * (1.0 / d)
        h = x * lax.rsqrt(ms + EPS) * a + shift
        groups = []
        for gi in range(ne // SUBLANE):
            prods = [_expert_row(tbl_ref, idx_ref[gi * SUBLANE + _BITREV8[i], t]) * h for i in range(SUBLANE)]
            groups.append(_sublane_sums(prods))
        part = jnp.concatenate(groups, axis=0)
        tot = jnp.dot(part, ones, preferred_element_type=F32, precision=HIGHEST)
        act_ref[...] = jnp.where(lane_t == t, tot, act_ref[...])
        return carry

    lax.fori_loop(0, tb, token, 0)
    coef_ref[0] = jax.nn.gelu(act_ref[...], approximate=True) * gate_ref[0]


def _peer3_kernel(idx_ref, cf_ref, x_ref, mod_ref, tbl_ref, o_ref):
    ne, tb = idx_ref.shape
    ga = mod_ref[5]

    def token(t, carry):
        accs = [jnp.zeros((SUBLANE, LANE), F32) for _ in range(4)]
        for e in range(ne):
            accs[e % 4] = accs[e % 4] + cf_ref[e, t] * _expert_row(tbl_ref, idx_ref[e, t])
        o_ref[t] = x_ref[t] + ga * ((accs[0] + accs[1]) + (accs[2] + accs[3]))
        return carry

    lax.fori_loop(0, tb, token, 0)


def _peer_blocks(b, t, lc):
    nj = t // PEER_TOKENS
    n_ctx = lc // PEER_TOKENS
    tokb = pl.BlockSpec((PEER_TOKENS, SUBLANE, LANE), lambda n: (n, 0, 0))
    modb = pl.BlockSpec((SUBLANE, SUBLANE, LANE), lambda n: ((n // nj) * 2 + ((n % nj) >= n_ctx).astype(I32), 0, 0))
    return nj, tokb, modb


def _peer_act(x3, idx2, gate, modv, g3, tbl, b, t, lc):
    ne = PEER_HEADS * PEER_TOPK
    nj, tokb, modb = _peer_blocks(b, t, lc)
    return pl.pallas_call(
        _peer2_kernel,
        grid=(b * nj,),
        in_specs=[pl.BlockSpec((ne, PEER_TOKENS), lambda n: (n // nj, n % nj), memory_space=pltpu.SMEM),
                  tokb, modb,
                  pl.BlockSpec((1, SUBLANE, LANE), lambda n: (0, 0, 0)),
                  pl.BlockSpec((1, ne, PEER_TOKENS), lambda n: (n // nj, 0, n % nj)),
                  pl.BlockSpec(tbl.shape, lambda n: (0, 0, 0), pipeline_mode=pl.Buffered(1))],
        out_specs=pl.BlockSpec((1, ne, PEER_TOKENS), lambda n: (n // nj, 0, n % nj)),
        out_shape=jax.ShapeDtypeStruct((b, ne, t), F32),
        scratch_shapes=[pltpu.VMEM((ne, PEER_TOKENS), F32)],
        compiler_params=_cp(("arbitrary",), 56),
        name="peer_act",
    )(idx2, x3, modv, g3, gate, tbl)


def _peer_out(x3, idx2, coef2, modv, tbl, b, t, lc):
    ne = PEER_HEADS * PEER_TOPK
    nj, tokb, modb = _peer_blocks(b, t, lc)
    smem = pl.BlockSpec((ne, PEER_TOKENS), lambda n: (n // nj, n % nj), memory_space=pltpu.SMEM)
    return pl.pallas_call(
        _peer3_kernel,
        grid=(b * nj,),
        in_specs=[smem, smem, tokb, modb,
                  pl.BlockSpec(tbl.shape, lambda n: (0, 0, 0), pipeline_mode=pl.Buffered(1))],
        out_specs=tokb,
        out_shape=jax.ShapeDtypeStruct(x3.shape, F32),
        compiler_params=_cp(("arbitrary",), 56),
        name="peer_out",
    )(idx2, coef2, x3, modv, tbl)


def _pack_experts(w):
    e, d = w.shape
    assert d == VREG_ELEMS
    bits = lax.bitcast_convert_type(w.astype(BF16), jnp.uint16).astype(U32).reshape(e // 2, 2, d)
    return (bits[:, 0] | (bits[:, 1] << 16)).reshape(e // 2, SUBLANE, LANE)


def _final_kernel(x_ref, g_ref, o_ref):
    x = x_ref[0]
    ms = jnp.mean(x * x, axis=-1, keepdims=True)
    o_ref[0] = x * lax.rsqrt(ms + EPS) * g_ref[...]


def _final_norm(xa, g, lc):
    b, t, d = xa.shape
    off = lc // ROW_TILE
    return pl.pallas_call(
        _final_kernel,
        grid=(b, (t - lc) // ROW_TILE),
        in_specs=[pl.BlockSpec((1, ROW_TILE, d), lambda i, j: (i, j + off, 0)),
                  pl.BlockSpec((1, d), lambda i, j: (0, 0))],
        out_specs=pl.BlockSpec((1, ROW_TILE, d), lambda i, j: (i, j, 0)),
        out_shape=jax.ShapeDtypeStruct((b, t - lc, d), F32),
        compiler_params=_cp(("parallel", "arbitrary"), 32),
        name="final_norm",
    )(xa, g)


def _rope_tables(l, lc):
    t = jnp.arange(l, dtype=I32)
    quarter = GLA_DK // 4
    freqs = ROPE_BASE ** (-jnp.arange(quarter, dtype=F32) / quarter)
    ang_r = (t // GRID_W).astype(F32)[:, None] * freqs
    ang_c = (t % GRID_W).astype(F32)[:, None] * freqs
    cos = jnp.concatenate([jnp.cos(ang_r)] * 2 + [jnp.cos(ang_c)] * 2, axis=-1)
    sin = jnp.concatenate([-jnp.sin(ang_r), jnp.sin(ang_r), -jnp.sin(ang_c), jnp.sin(ang_c)], axis=-1)
    cos = jnp.tile(cos, (1, GLA_HEADS))
    sin = jnp.tile(sin, (1, GLA_HEADS))
    qs = GLA_DK ** -0.5
    cs = jnp.concatenate([cos * qs, cos], axis=-1)
    sn = jnp.concatenate([sin * qs, sin], axis=-1)
    cs_ctx = jnp.concatenate([jnp.full((lc, GLA_KW), qs, F32), jnp.ones((lc, GLA_KW), F32)], axis=-1)
    return (jnp.concatenate([cs_ctx, cs], axis=0),
            jnp.concatenate([jnp.zeros((lc, 2 * GLA_KW), F32), sn], axis=0))


def _permute_w_in(w, d):
    sizes = (GLA_KW, GLA_KW, GLA_VW, 2 * GATE_RANK, GLA_VW, NA_W, NA_W, NA_W, 2 * d)
    offs = [0]
    for s in sizes:
        offs.append(offs[-1] + s)
    q, k, v, z, r, nq, nk, nv, gates = [w[:, offs[i]:offs[i + 1]] for i in range(len(sizes))]
    zpad = jnp.pad(z, ((0, 0), (0, LANE - 2 * GATE_RANK)))
    return jnp.concatenate([q, k, v, r, nq, nk, nv, gates, zpad], axis=1).astype(BF16)


def _gate_weights(w_a, b_a, first_row):
    wa = jnp.zeros((LANE, GLA_KW), F32).at[first_row:first_row + GATE_RANK].set(w_a)
    return wa, b_a.reshape(1, GLA_KW)


def kernel(x, c, ctx, c_ctx, w_mod, b_mod, g_norm1, w_in, w_alpha_f, b_alpha_f, w_alpha_b, b_alpha_b, g_gla, rpb,
           w_proj_gla, w_proj_na, w_out, g_norm2, w_query, sub_keys, expert_u, expert_v, g_final):
    b, l, d = x.shape
    lc = ctx.shape[1]
    depth = w_mod.shape[0]
    t = lc + l
    assert d == VREG_ELEMS and lc % ROW_TILE == 0 and l % ROW_TILE == 0 and l % GRID_W == 0
    n_ctx_tiles = lc // ROW_TILE

    rows = -(-(b + 1) // SUBLANE) * SUBLANE
    cin = jnp.zeros((rows, d), F32).at[:b].set(c).at[b].set(c_ctx)
    mod_all = _modulation(cin, w_mod, b_mod)

    cs, sn = _rope_tables(l, lc)
    xa = jnp.concatenate([ctx, x], axis=1)

    for i in range(depth):
        mod = mod_all[i].reshape(rows, N_MOD, d)
        modseg = jnp.stack([jnp.broadcast_to(mod[b], (b, N_MOD, d)), mod[:b]], axis=1)
        modseg = jnp.pad(modseg, ((0, 0), (0, 0), (0, SUBLANE - N_MOD), (0, 0)))
        modv = modseg.reshape(b * 2 * SUBLANE, SUBLANE, LANE)

        qk, v, r, nq, nk, nv, gates, z = _project(xa, modseg, g_norm1[i].reshape(1, d),
                                                  _permute_w_in(w_in[i], d), n_ctx_tiles)
        waf, baf = _gate_weights(w_alpha_f[i], b_alpha_f[i], 0)
        wab, bab = _gate_weights(w_alpha_b[i], b_alpha_b[i], GATE_RANK)
        o_f = _gla(qk, v, z, cs, sn, waf, baf, lc // GLA_BLOCK, reverse=False)
        o_b = _gla(qk, v, z, cs, sn, wab, bab, lc // GLA_BLOCK, reverse=True)
        na = _na(nq, nk, nv, _na_bias_tables(rpb[i]), lc)
        xa = _merge(xa, o_f, o_b, r, na, gates, modseg, g_gla[i].reshape(1, GLA_DV),
                    w_proj_gla[i].astype(BF16), w_proj_na[i].astype(BF16), w_out[i].astype(BF16), n_ctx_tiles)

        idx, gate = _peer_select(xa, modseg, g_norm2[i].reshape(1, d), w_query[i].astype(BF16),
                                 sub_keys[i].astype(BF16), n_ctx_tiles)
        ne = idx.shape[1]
        idx2 = idx.reshape(b * ne, t)
        x3 = xa.reshape(b * t, SUBLANE, LANE)
        coef = _peer_act(x3, idx2, gate, modv, g_norm2[i].reshape(1, SUBLANE, LANE),
                         _pack_experts(expert_u[i]), b, t, lc)
        x3 = _peer_out(x3, idx2, coef.reshape(b * ne, t), modv, _pack_experts(expert_v[i]), b, t, lc)
        xa = x3.reshape(b, t, d)

    return _final_norm(xa, g_final.reshape(1, d), lc)
```

```python
import functools

import jax
import jax.numpy as jnp
from jax import lax
from jax.experimental import pallas as pl
from jax.experimental.pallas import tpu as pltpu

F32 = jnp.float32
BF16 = jnp.bfloat16
I32 = jnp.int32
U32 = jnp.uint32
HIGHEST = lax.Precision.HIGHEST

GRID_W = 64
EPS = 1e-6
N_MOD = 6
GLA_HEADS = 4
GLA_DK = 64
GLA_DV = 128
GLA_KW = GLA_HEADS * GLA_DK
GLA_VW = GLA_HEADS * GLA_DV
GATE_RANK = 16
GATE_TAU = 16.0
CHUNK = 64
ROPE_BASE = 10000.0
NA_HEADS = 8
NA_DH = 64
NA_W = NA_HEADS * NA_DH
NA_KH = 8
NA_KW = 16
PEER_HEADS = 8
N_KEYS = 128
PEER_DK = 256
PEER_TOPK = 16

LANE = 128
SUBLANE = 8
VREG_ELEMS = LANE * SUBLANE

ROW_TILE = 256
GLA_BLOCK = 256
PEER_TOKENS = 128
MASK_NEG = -1e30

_NT = (((1,), (1,)), ((), ()))


def _cp(sem, vmem_mb):
    return pltpu.CompilerParams(dimension_semantics=sem, vmem_limit_bytes=vmem_mb * 1024 * 1024)


def _norm_mod(x, g, shift, scale):
    ms = jnp.mean(x * x, axis=-1, keepdims=True)
    return (x * lax.rsqrt(ms + EPS) * g) * (1.0 + scale) + shift


def _mod_kernel(c_ref, w_ref, b_ref, o_ref):
    a = c_ref[...]
    a = a * jax.nn.sigmoid(a)
    o_ref[0] = jnp.dot(a, w_ref[0], preferred_element_type=F32, precision=HIGHEST) + b_ref[0]


def _modulation(cin, w_mod, b_mod):
    depth, d, n = w_mod.shape
    rows = cin.shape[0]
    tn = n // 4
    return pl.pallas_call(
        _mod_kernel,
        grid=(depth, n // tn),
        in_specs=[pl.BlockSpec((rows, d), lambda l, j: (0, 0)),
                  pl.BlockSpec((1, d, tn), lambda l, j: (l, 0, j)),
                  pl.BlockSpec((1, 1, tn), lambda l, j: (l, 0, j))],
        out_specs=pl.BlockSpec((1, rows, tn), lambda l, j: (l, 0, j)),
        out_shape=jax.ShapeDtypeStruct((depth, rows, n), F32),
        compiler_params=_cp(("arbitrary", "arbitrary"), 40),
        name="modulation",
    )(cin, w_mod, b_mod.reshape(depth, 1, n))


_PROJ_OUTS = (("qk", 2 * GLA_KW, F32), ("v", GLA_VW, F32), ("r", GLA_VW, F32),
              ("nq", NA_W, BF16), ("nk", NA_W, BF16), ("nv", NA_W, BF16),
              ("gates", None, F32), ("z", LANE, F32))


def _proj_kernel(x_ref, mod_ref, g_ref, w_ref, *out_refs):
    m = mod_ref[0, 0]
    h = _norm_mod(x_ref[0], g_ref[...], m[0:1], m[1:2]).astype(BF16)
    col = 0
    for ref in out_refs:
        n = ref.shape[-1]
        ref[0] = jnp.dot(h, w_ref[:, col:col + n], preferred_element_type=F32).astype(ref.dtype)
        col += n


def _project(xa, modseg, g, w_perm, n_ctx_tiles):
    b, t, d = xa.shape
    outs = [(name, (2 * d if n is None else n), dt) for name, n, dt in _PROJ_OUTS]
    assert sum(n for _, n, _ in outs) == w_perm.shape[1]
    return pl.pallas_call(
        _proj_kernel,
        grid=(b, t // ROW_TILE),
        in_specs=[pl.BlockSpec((1, ROW_TILE, d), lambda i, j: (i, j, 0)),
                  pl.BlockSpec((1, 1, SUBLANE, d), lambda i, j: (i, (j >= n_ctx_tiles).astype(I32), 0, 0)),
                  pl.BlockSpec((1, d), lambda i, j: (0, 0)),
                  pl.BlockSpec(w_perm.shape, lambda i, j: (0, 0))],
        out_specs=[pl.BlockSpec((1, ROW_TILE, n), lambda i, j: (i, j, 0)) for _, n, _ in outs],
        out_shape=[jax.ShapeDtypeStruct((b, t, n), dt) for _, n, dt in outs],
        compiler_params=_cp(("parallel", "arbitrary"), 48),
        name="in_proj",
    )(xa, modseg, g, w_perm)


def _gla_kernel(qk_ref, v_ref, z_ref, cs_ref, sn_ref, wa_ref, ba_ref, o_ref, st_ref, *, reverse):
    @pl.when(pl.program_id(1) == 0)
    def _init():
        st_ref[...] = jnp.zeros_like(st_ref)

    kw2 = 2 * GLA_KW
    blk = qk_ref.shape[1]
    qk = qk_ref[0]
    lane = lax.broadcasted_iota(I32, (1, kw2), 1)
    first = (lane % 32) < 16
    partner = jnp.where(first, pltpu.roll(qk, kw2 - 16, 1), pltpu.roll(qk, 16, 1))
    qk = qk * cs_ref[...] + partner * sn_ref[...]
    zz = jnp.dot(z_ref[0], wa_ref[...], preferred_element_type=F32, precision=HIGHEST) + ba_ref[...]
    la = (jnp.minimum(zz, 0.0) - jnp.log1p(jnp.exp(-jnp.abs(zz)))) * (1.0 / GATE_TAU)

    ri = lax.broadcasted_iota(I32, (CHUNK, CHUNK), 0)
    ci = lax.broadcasted_iota(I32, (CHUNK, CHUNK), 1)
    keep = (ci >= ri) if reverse else (ci <= ri)
    tri = keep.astype(F32)
    khead = lax.broadcasted_iota(I32, (1, GLA_KW), 1) // GLA_DK
    vrow = lax.broadcasted_iota(I32, (GLA_VW, GLA_KW), 0) // GLA_DV
    kcol = lax.broadcasted_iota(I32, (GLA_VW, GLA_KW), 1) // GLA_DK
    same_head = vrow == kcol

    order = range(blk // CHUNK)
    if reverse:
        order = reversed(order)
    for c in order:
        lo = c * CHUNK
        q = qk[lo:lo + CHUNK, :GLA_KW]
        k = qk[lo:lo + CHUNK, GLA_KW:]
        v = v_ref[0, lo:lo + CHUNK, :].astype(BF16)
        bcum = jnp.dot(tri, la[lo:lo + CHUNK], preferred_element_type=F32, precision=HIGHEST)
        bend = bcum[0:1] if reverse else bcum[CHUNK - 1:CHUNK]
        qd = q * jnp.exp(bcum)
        kd = (k * jnp.exp(-bcum)).astype(BF16)
        kd2 = (k * jnp.exp(bend - bcum)).astype(BF16)
        st = st_ref[...]
        o_inter = lax.dot_general(qd.astype(BF16), st.astype(BF16), _NT, preferred_element_type=F32)
        for h in range(GLA_HEADS):
            qm = jnp.where(khead == h, qd, 0.0).astype(BF16)
            s = lax.dot_general(qm, kd, _NT, preferred_element_type=F32)
            s = jnp.where(keep, s, 0.0).astype(BF16)
            vs = slice(h * GLA_DV, (h + 1) * GLA_DV)
            o_ref[0, lo:lo + CHUNK, vs] = (jnp.dot(s, v[:, vs], preferred_element_type=F32)
                                           + o_inter[:, vs])
        upd = jnp.dot(v.T, kd2, preferred_element_type=F32)
        st_ref[...] = st * jnp.exp(bend) + jnp.where(same_head, upd, 0.0)


def _gla(qk, v, z, cs, sn, wa, ba, n_ctx_blocks, reverse):
    b, t, _ = qk.shape
    nblk = t // GLA_BLOCK

    if reverse:
        def blk(j):
            return jnp.where(j < n_ctx_blocks, n_ctx_blocks - 1 - j, nblk - 1 - (j - n_ctx_blocks))
    else:
        def blk(j):
            return j

    tok = lambda n: pl.BlockSpec((1, GLA_BLOCK, n), lambda i, j: (i, blk(j), 0))
    tab = lambda n: pl.BlockSpec((GLA_BLOCK, n), lambda i, j: (blk(j), 0))
    return pl.pallas_call(
        functools.partial(_gla_kernel, reverse=reverse),
        grid=(b, nblk),
        in_specs=[tok(2 * GLA_KW), tok(GLA_VW), tok(LANE), tab(2 * GLA_KW), tab(2 * GLA_KW),
                  pl.BlockSpec((LANE, GLA_KW), lambda i, j: (0, 0)),
                  pl.BlockSpec((1, GLA_KW), lambda i, j: (0, 0))],
        out_specs=tok(GLA_VW),
        out_shape=jax.ShapeDtypeStruct((b, t, GLA_VW), F32),
        scratch_shapes=[pltpu.VMEM((GLA_VW, GLA_KW), F32)],
        compiler_params=_cp(("parallel", "arbitrary"), 40),
        name="gla_bwd" if reverse else "gla_fwd",
    )(qk, v, z, cs, sn, wa, ba)


def _na_kernel(*refs):
    q_ref = refs[0]
    k_refs = refs[1:1 + NA_KH]
    v_refs = refs[1 + NA_KH:1 + 2 * NA_KH]
    kc_ref, vc_ref, tb_ref, o_ref = refs[1 + 2 * NA_KH:]
    scale = NA_DH ** -0.5
    half = lax.broadcasted_iota(I32, (1, LANE), 1) // NA_DH
    for p in range(NA_HEADS // 2):
        ls = slice(p * LANE, (p + 1) * LANE)
        qp = q_ref[0, :, ls]
        kw = jnp.concatenate([r[0, :, ls] for r in k_refs], axis=0)
        vw = jnp.concatenate([r[0, :, ls] for r in v_refs], axis=0)
        kc = kc_ref[0, :, ls]
        vc = vc_ref[0, :, ls]
        outs = []
        for s in range(2):
            qm = jnp.where(half == s, qp, jnp.zeros_like(qp))
            sw = lax.dot_general(qm, kw, _NT, preferred_element_type=F32) * scale + tb_ref[0, 2 * p + s]
            sc = lax.dot_general(qm, kc, _NT, preferred_element_type=F32) * scale
            m = jnp.maximum(jnp.max(sw, axis=-1, keepdims=True), jnp.max(sc, axis=-1, keepdims=True))
            pw = jnp.exp(sw - m)
            pc = jnp.exp(sc - m)
            den = jnp.sum(pw, axis=-1, keepdims=True) + jnp.sum(pc, axis=-1, keepdims=True)
            o = (jnp.dot(pw.astype(BF16), vw, preferred_element_type=F32)
                 + jnp.dot(pc.astype(BF16), vc, preferred_element_type=F32))
            outs.append(o / den)
        o_ref[0, :, ls] = jnp.where(half == 0, outs[0], outs[1]).astype(o_ref.dtype)


def _na_bias_tables(rpb):
    w = jnp.arange(GRID_W, dtype=I32)
    c0 = jnp.clip(w - NA_KW // 2, 0, GRID_W - NA_KW)
    cp = w[None, :]
    valid = (cp >= c0[:, None]) & (cp < c0[:, None] + NA_KW)
    off = jnp.clip(cp - w[:, None] + (NA_KW - 1), 0, 2 * NA_KW - 2)
    t = jnp.where(valid[None, None], rpb[:, :, off], MASK_NEG)
    tabs = [jnp.concatenate([t[:, s + a] for a in range(NA_KH)], axis=-1) for s in range(NA_KH)]
    tabs.append(jnp.full_like(tabs[0], MASK_NEG))
    return jnp.stack(tabs, axis=0).astype(F32)


def _na(nq, nk, nv, tables, lc):
    b, t, _ = nq.shape
    n_ctx = lc // GRID_W
    rows = (t - lc) // GRID_W
    assert rows >= NA_KH

    def row0(j):
        r = jnp.maximum(j - n_ctx, 0)
        return r, jnp.clip(r - NA_KH // 2, 0, rows - NA_KH)

    def win(a):
        return pl.BlockSpec((1, GRID_W, NA_W), lambda i, j: (i, n_ctx + row0(j)[1] + a, 0))

    def tab_idx(i, j):
        r, r0 = row0(j)
        return (jnp.where(j < n_ctx, NA_KH, r0 - r + NA_KH - 1), 0, 0, 0)

    return pl.pallas_call(
        _na_kernel,
        grid=(b, t // GRID_W),
        in_specs=([pl.BlockSpec((1, GRID_W, NA_W), lambda i, j: (i, j, 0))]
                  + [win(a) for a in range(NA_KH)] + [win(a) for a in range(NA_KH)]
                  + [pl.BlockSpec((1, lc, NA_W), lambda i, j: (i, 0, 0))] * 2
                  + [pl.BlockSpec((1, NA_HEADS, GRID_W, NA_KH * GRID_W), tab_idx)]),
        out_specs=pl.BlockSpec((1, GRID_W, NA_W), lambda i, j: (i, j, 0)),
        out_shape=jax.ShapeDtypeStruct((b, t, NA_W), BF16),
        compiler_params=_cp(("parallel", "arbitrary"), 40),
        name="natten",
    )(nq, *([nk] * NA_KH), *([nv] * NA_KH), nk, nv, tables)


def _merge_kernel(x_ref, of_ref, ob_ref, r_ref, na_ref, gt_ref, mod_ref, gg_ref, wpg_ref, wpn_ref, wo_ref,
                  o_ref):
    d = x_ref.shape[-1]
    o = of_ref[0] + ob_ref[0]
    r = r_ref[0]
    parts = []
    for h in range(GLA_HEADS):
        oh = o[:, h * GLA_DV:(h + 1) * GLA_DV]
        ms = jnp.mean(oh * oh, axis=-1, keepdims=True)
        parts.append(oh * lax.rsqrt(ms + EPS) * gg_ref[...])
    on = jnp.concatenate(parts, axis=-1) * (r * jax.nn.sigmoid(r))
    ya = jnp.dot(on.astype(BF16), wpg_ref[...], preferred_element_type=F32)
    yb = jnp.dot(na_ref[0], wpn_ref[...], preferred_element_type=F32)
    gt = gt_ref[0]
    y = jax.nn.sigmoid(gt[:, :d]) * ya + jax.nn.sigmoid(gt[:, d:]) * yb
    y = jnp.dot(y.astype(BF16), wo_ref[...], preferred_element_type=F32)
    o_ref[0] = x_ref[0] + mod_ref[0, 0][2:3] * y


def _merge(xa, o_f, o_b, r, na, gates, modseg, g_gla, wpg, wpn, wo, n_ctx_tiles):
    b, t, d = xa.shape
    tok = lambda n: pl.BlockSpec((1, ROW_TILE, n), lambda i, j: (i, j, 0))
    full = lambda a: pl.BlockSpec(a.shape, lambda i, j: (0,) * a.ndim)
    return pl.pallas_call(
        _merge_kernel,
        grid=(b, t // ROW_TILE),
        in_specs=[tok(d), tok(GLA_VW), tok(GLA_VW), tok(GLA_VW), tok(NA_W), tok(2 * d),
                  pl.BlockSpec((1, 1, SUBLANE, d), lambda i, j: (i, (j >= n_ctx_tiles).astype(I32), 0, 0)),
                  full(g_gla), full(wpg), full(wpn), full(wo)],
        out_specs=tok(d),
        out_shape=jax.ShapeDtypeStruct((b, t, d), F32),
        compiler_params=_cp(("parallel", "arbitrary"), 40),
        name="merge",
    )(xa, o_f, o_b, r, na, gates, modseg, g_gla, wpg, wpn, wo)


def _topk_rows(s, ids, k):
    rows = s.shape[0]
    rid = lax.broadcasted_iota(I32, s.shape, 0).astype(F32)
    vals, out_ids = [], []
    for _ in range(k):
        m = jnp.max(s, axis=0, keepdims=True)
        pos = jnp.min(jnp.where(s == m, rid, float(rows)), axis=0, keepdims=True)
        hit = rid == pos
        vals.append(m)
        out_ids.append(pos if ids is None else jnp.max(jnp.where(hit, ids, -1.0), axis=0, keepdims=True))
        s = jnp.where(hit, -jnp.inf, s)
    return jnp.concatenate(vals, axis=0), jnp.concatenate(out_ids, axis=0)


def _staircase_candidates(v0, i0, v1, i1):
    k = PEER_TOPK
    sub = lax.broadcasted_iota(I32, (SUBLANE, v0.shape[1]), 0)
    vals = [v0[0:1] + v1]
    ids = [i0[0:1] * float(N_KEYS) + i1]
    for a in range(1, k // 2):
        nb = k // (a + 1)
        live = sub < nb
        vals.append(jnp.where(live, v0[a:a + 1] + v1[:SUBLANE], -jnp.inf))
        ids.append(i0[a:a + 1] * float(N_KEYS) + i1[:SUBLANE])
    vals.append(v0[k // 2:] + v1[0:1])
    ids.append(i0[k // 2:] * float(N_KEYS) + i1[0:1])
    return jnp.concatenate(vals, axis=0), jnp.concatenate(ids, axis=0)


def _peer1_kernel(x_ref, mod_ref, g_ref, wq_ref, sk_ref, idx_ref, pair_ref, gate_ref):
    m = mod_ref[0, 0]
    h = _norm_mod(x_ref[0], g_ref[...], m[3:4], m[4:5]).astype(BF16)
    half = PEER_DK // 2
    chosen = []
    for hd in range(PEER_HEADS):
        tops = []
        for p in range(2):
            col = (2 * hd + p) * half
            qp = jnp.dot(h, wq_ref[:, col:col + half], preferred_element_type=F32).astype(BF16)
            st = lax.dot_general(sk_ref[hd, p], qp, _NT, preferred_element_type=F32)
            tops.append(_topk_rows(st, None, PEER_TOPK))
        (v0, i0), (v1, i1) = tops
        cand, cidx = _staircase_candidates(v0, i0, v1, i1)
        best, experts = _topk_rows(cand, cidx, PEER_TOPK)
        e = jnp.exp(best - best[0:1])
        rs = slice(hd * PEER_TOPK, (hd + 1) * PEER_TOPK)
        gate_ref[0, rs, :] = e / jnp.sum(e, axis=0, keepdims=True)
        idx_ref[0, rs, :] = experts.astype(I32)
        chosen.append(experts)
    pair_ref[0] = lax.shift_right_logical(jnp.concatenate(chosen, axis=0).T.astype(I32), 1)


def _peer_select(xa, modseg, g, wq, sk, n_ctx_tiles):
    b, t, d = xa.shape
    ne = PEER_HEADS * PEER_TOPK
    return pl.pallas_call(
        _peer1_kernel,
        grid=(b, t // ROW_TILE),
        in_specs=[pl.BlockSpec((1, ROW_TILE, d), lambda i, j: (i, j, 0)),
                  pl.BlockSpec((1, 1, SUBLANE, d), lambda i, j: (i, (j >= n_ctx_tiles).astype(I32), 0, 0)),
                  pl.BlockSpec((1, d), lambda i, j: (0, 0)),
                  pl.BlockSpec(wq.shape, lambda i, j: (0, 0)),
                  pl.BlockSpec(sk.shape, lambda i, j: (0, 0, 0, 0))],
        out_specs=[pl.BlockSpec((1, ne, ROW_TILE), lambda i, j: (i, 0, j)),
                   pl.BlockSpec((1, ROW_TILE, ne), lambda i, j: (i, j, 0)),
                   pl.BlockSpec((1, ne, ROW_TILE), lambda i, j: (i, 0, j))],
        out_shape=[jax.ShapeDtypeStruct((b, ne, t), I32), jax.ShapeDtypeStruct((b, t, ne), I32),
                   jax.ShapeDtypeStruct((b, ne, t), F32)],
        compiler_params=_cp(("parallel", "arbitrary"), 40),
        name="peer_select",
    )(xa, modseg, g, wq, sk)


PAIR_ROWS = 2 * SUBLANE


def _gather_tiles(tbl_ref, pair_ref, t, ne):
    base = pl.multiple_of(t * ne, ne)
    return jnp.concatenate([tbl_ref[pair_ref[base + j]] for j in range(ne)], axis=0)


def _for_token_pairs(tb, body):
    def trip(i, carry):
        body(2 * i)
        body(2 * i + 1)
        return carry

    lax.fori_loop(0, tb // 2, trip, 0)


def _peer2_kernel(pair_ref, idx_ref, x_ref, mod_ref, g_ref, gate_ref, tbl_ref, cc_ref, acc_ref):
    tb, kk = acc_ref.shape
    ne = kk // PAIR_ROWS
    d = x_ref.shape[1] * x_ref.shape[2]
    a = g_ref[0] * (1.0 + mod_ref[4])
    shift = mod_ref[3]
    lane = lax.broadcasted_iota(I32, (SUBLANE, kk), 1)
    sub = lax.broadcasted_iota(I32, (SUBLANE, kk), 0)
    diag = ((lane % PAIR_ROWS) // 2) == sub

    def token(t):
        x = x_ref[t]
        ms = jnp.sum(x * x, keepdims=True) * (1.0 / d)
        h = x * lax.rsqrt(ms + EPS) * a + shift
        hb = jnp.concatenate([h, jnp.zeros_like(h)], axis=0).astype(BF16)
        tiles = _gather_tiles(tbl_ref, pair_ref, t, ne)
        out = lax.dot_general(hb, tiles, _NT, preferred_element_type=F32)
        acc_ref[pl.ds(t, 1), :] = jnp.sum(jnp.where(diag, out[:SUBLANE], 0.0), axis=0, keepdims=True)

    _for_token_pairs(tb, token)

    kq = lax.broadcasted_iota(I32, (2 * ne, kk), 1)
    rq = lax.broadcasted_iota(I32, (2 * ne, kk), 0)
    sel = (((kq // PAIR_ROWS) == (rq % ne)) & ((kq % 2) == (rq // ne))).astype(F32)
    at = lax.dot_general(sel, acc_ref[...], _NT, preferred_element_type=F32, precision=HIGHEST)
    par = idx_ref[0] & 1
    act = jnp.where(par == 0, at[:ne], at[ne:])
    coef = jax.nn.gelu(act, approximate=True) * gate_ref[0]
    ke = lax.broadcasted_iota(I32, (ne, kk), 1)
    re = lax.broadcasted_iota(I32, (ne, kk), 0)
    expand = ((ke // PAIR_ROWS) == re).astype(BF16)
    cexp = jnp.dot(coef.T.astype(BF16), expand, preferred_element_type=F32)
    pexp = jnp.dot(par.astype(F32).T.astype(BF16), expand, preferred_element_type=F32)
    lanepar = (lax.broadcasted_iota(I32, (tb, kk), 1) % 2).astype(F32)
    cc_ref[...] = jnp.where(pexp == lanepar, cexp, 0.0)


def _peer3_kernel(pair_ref, cc_ref, x_ref, mod_ref, tbl_ref, o_ref):
    tb, kk = cc_ref.shape
    ne = kk // PAIR_ROWS
    ga = mod_ref[5]
    lane = lax.broadcasted_iota(I32, (PAIR_ROWS, kk), 1)
    sub = lax.broadcasted_iota(I32, (PAIR_ROWS, kk), 0)
    diag = ((lane % PAIR_ROWS) // 2) == sub

    def token(t):
        row = cc_ref[pl.ds(t, 1), :]
        cm = jnp.where(diag, jnp.broadcast_to(row, (PAIR_ROWS, kk)), 0.0).astype(BF16)
        tiles = _gather_tiles(tbl_ref, pair_ref, t, ne)
        o = jnp.dot(cm, tiles, preferred_element_type=F32)
        o_ref[t] = x_ref[t] + ga * o[:SUBLANE]

    _for_token_pairs(tb, token)


def _peer_blocks(b, t, lc):
    nj = t // PEER_TOKENS
    n_ctx = lc // PEER_TOKENS
    tokb = pl.BlockSpec((PEER_TOKENS, SUBLANE, LANE), lambda n: (n, 0, 0))
    modb = pl.BlockSpec((SUBLANE, SUBLANE, LANE), lambda n: ((n // nj) * 2 + ((n % nj) >= n_ctx).astype(I32), 0, 0))
    return nj, tokb, modb


def _peer_act(x3, pair2, idx, gate, modv, g3, tbl, b, t, lc):
    ne = PEER_HEADS * PEER_TOPK
    kk = ne * PAIR_ROWS
    nj, tokb, modb = _peer_blocks(b, t, lc)
    expb = pl.BlockSpec((1, ne, PEER_TOKENS), lambda n: (n // nj, 0, n % nj))
    return pl.pallas_call(
        _peer2_kernel,
        grid=(b * nj,),
        in_specs=[pl.BlockSpec((PEER_TOKENS * ne,), lambda n: (n,), memory_space=pltpu.SMEM),
                  expb, tokb, modb,
                  pl.BlockSpec((1, SUBLANE, LANE), lambda n: (0, 0, 0)),
                  expb,
                  pl.BlockSpec(tbl.shape, lambda n: (0, 0, 0), pipeline_mode=pl.Buffered(1))],
        out_specs=pl.BlockSpec((PEER_TOKENS, kk), lambda n: (n, 0)),
        out_shape=jax.ShapeDtypeStruct((b * t, kk), F32),
        scratch_shapes=[pltpu.VMEM((PEER_TOKENS, kk), F32)],
        compiler_params=_cp(("arbitrary",), 56),
        name="peer_act",
    )(pair2, idx, x3, modv, g3, gate, tbl)


def _peer_out(x3, pair2, cc, modv, tbl, b, t, lc):
    ne = PEER_HEADS * PEER_TOPK
    kk = ne * PAIR_ROWS
    nj, tokb, modb = _peer_blocks(b, t, lc)
    return pl.pallas_call(
        _peer3_kernel,
        grid=(b * nj,),
        in_specs=[pl.BlockSpec((PEER_TOKENS * ne,), lambda n: (n,), memory_space=pltpu.SMEM),
                  pl.BlockSpec((PEER_TOKENS, kk), lambda n: (n, 0)),
                  tokb, modb,
                  pl.BlockSpec(tbl.shape, lambda n: (0, 0, 0), pipeline_mode=pl.Buffered(1))],
        out_specs=tokb,
        out_shape=jax.ShapeDtypeStruct(x3.shape, F32),
        compiler_params=_cp(("arbitrary",), 56),
        name="peer_out",
    )(pair2, cc, x3, modv, tbl)


def _pack_experts(w):
    e, d = w.shape
    assert d == VREG_ELEMS
    w = w.astype(BF16).reshape(e // 2, 2, SUBLANE, LANE)
    return jnp.transpose(w, (0, 2, 1, 3)).reshape(e // 2, PAIR_ROWS, LANE)


def _peer(xa, modseg, modv, g2, w_query, sub_keys, expert_u, expert_v, lc):
    b, t, d = xa.shape
    idx, pair, gate = _peer_select(xa, modseg, g2.reshape(1, d), w_query.astype(BF16), sub_keys.astype(BF16),
                                   lc // ROW_TILE)
    pair2 = pair.reshape(b * t * idx.shape[1])
    x3 = xa.reshape(b * t, SUBLANE, LANE)
    cc = _peer_act(x3, pair2, idx, gate, modv, g2.reshape(1, SUBLANE, LANE), _pack_experts(expert_u), b, t, lc)
    x3 = _peer_out(x3, pair2, cc, modv, _pack_experts(expert_v), b, t, lc)
    return x3.reshape(b, t, d)


def _final_kernel(x_ref, g_ref, o_ref):
    x = x_ref[0]
    ms = jnp.mean(x * x, axis=-1, keepdims=True)
    o_ref[0] = x * lax.rsqrt(ms + EPS) * g_ref[...]


def _final_norm(xa, g, lc):
    b, t, d = xa.shape
    off = lc // ROW_TILE
    return pl.pallas_call(
        _final_kernel,
        grid=(b, (t - lc) // ROW_TILE),
        in_specs=[pl.BlockSpec((1, ROW_TILE, d), lambda i, j: (i, j + off, 0)),
                  pl.BlockSpec((1, d), lambda i, j: (0, 0))],
        out_specs=pl.BlockSpec((1, ROW_TILE, d), lambda i, j: (i, j, 0)),
        out_shape=jax.ShapeDtypeStruct((b, t - lc, d), F32),
        compiler_params=_cp(("parallel", "arbitrary"), 32),
        name="final_norm",
    )(xa, g)


def _rope_tables(l, lc):
    t = jnp.arange(l, dtype=I32)
    quarter = GLA_DK // 4
    freqs = ROPE_BASE ** (-jnp.arange(quarter, dtype=F32) / quarter)
    ang_r = (t // GRID_W).astype(F32)[:, None] * freqs
    ang_c = (t % GRID_W).astype(F32)[:, None] * freqs
    cos = jnp.concatenate([jnp.cos(ang_r)] * 2 + [jnp.cos(ang_c)] * 2, axis=-1)
    sin = jnp.concatenate([-jnp.sin(ang_r), jnp.sin(ang_r), -jnp.sin(ang_c), jnp.sin(ang_c)], axis=-1)
    cos = jnp.tile(cos, (1, GLA_HEADS))
    sin = jnp.tile(sin, (1, GLA_HEADS))
    qs = GLA_DK ** -0.5
    cs = jnp.concatenate([cos * qs, cos], axis=-1)
    sn = jnp.concatenate([sin * qs, sin], axis=-1)
    cs_ctx = jnp.concatenate([jnp.full((lc, GLA_KW), qs, F32), jnp.ones((lc, GLA_KW), F32)], axis=-1)
    return (jnp.concatenate([cs_ctx, cs], axis=0),
            jnp.concatenate([jnp.zeros((lc, 2 * GLA_KW), F32), sn], axis=0))


def _permute_w_in(w, d):
    sizes = (GLA_KW, GLA_KW, GLA_VW, 2 * GATE_RANK, GLA_VW, NA_W, NA_W, NA_W, 2 * d)
    offs = [0]
    for s in sizes:
        offs.append(offs[-1] + s)
    q, k, v, z, r, nq, nk, nv, gates = [w[:, offs[i]:offs[i + 1]] for i in range(len(sizes))]
    zpad = jnp.pad(z, ((0, 0), (0, LANE - 2 * GATE_RANK)))
    return jnp.concatenate([q, k, v, r, nq, nk, nv, gates, zpad], axis=1).astype(BF16)


def _gate_weights(w_a, b_a, first_row):
    wa = jnp.zeros((LANE, GLA_KW), F32).at[first_row:first_row + GATE_RANK].set(w_a)
    return wa, b_a.reshape(1, GLA_KW)


def kernel(x, c, ctx, c_ctx, w_mod, b_mod, g_norm1, w_in, w_alpha_f, b_alpha_f, w_alpha_b, b_alpha_b, g_gla, rpb,
           w_proj_gla, w_proj_na, w_out, g_norm2, w_query, sub_keys, expert_u, expert_v, g_final):
    b, l, d = x.shape
    lc = ctx.shape[1]
    depth = w_mod.shape[0]
    t = lc + l
    assert d == VREG_ELEMS and lc % ROW_TILE == 0 and l % ROW_TILE == 0 and l % GRID_W == 0
    n_ctx_tiles = lc // ROW_TILE

    rows = -(-(b + 1) // SUBLANE) * SUBLANE
    cin = jnp.zeros((rows, d), F32).at[:b].set(c).at[b].set(c_ctx)
    mod_all = _modulation(cin, w_mod, b_mod)

    cs, sn = _rope_tables(l, lc)
    xa = jnp.concatenate([ctx, x], axis=1)

    for i in range(depth):
        mod = mod_all[i].reshape(rows, N_MOD, d)
        modseg = jnp.stack([jnp.broadcast_to(mod[b], (b, N_MOD, d)), mod[:b]], axis=1)
        modseg = jnp.pad(modseg, ((0, 0), (0, 0), (0, SUBLANE - N_MOD), (0, 0)))
        modv = modseg.reshape(b * 2 * SUBLANE, SUBLANE, LANE)

        qk, v, r, nq, nk, nv, gates, z = _project(xa, modseg, g_norm1[i].reshape(1, d),
                                                  _permute_w_in(w_in[i], d), n_ctx_tiles)
        waf, baf = _gate_weights(w_alpha_f[i], b_alpha_f[i], 0)
        wab, bab = _gate_weights(w_alpha_b[i], b_alpha_b[i], GATE_RANK)
        o_f = _gla(qk, v, z, cs, sn, waf, baf, lc // GLA_BLOCK, reverse=False)
        o_b = _gla(qk, v, z, cs, sn, wab, bab, lc // GLA_BLOCK, reverse=True)
        na = _na(nq, nk, nv, _na_bias_tables(rpb[i]), lc)
        xa = _merge(xa, o_f, o_b, r, na, gates, modseg, g_gla[i].reshape(1, GLA_DV),
                    w_proj_gla[i].astype(BF16), w_proj_na[i].astype(BF16), w_out[i].astype(BF16), n_ctx_tiles)

        xa = _peer(xa, modseg, modv, g_norm2[i], w_query[i], sub_keys[i], expert_u[i], expert_v[i], lc)

    return _final_norm(xa, g_final.reshape(1, d), lc)
```

```python
import functools

import jax
import jax.numpy as jnp
from jax import lax
from jax.experimental import pallas as pl
from jax.experimental.pallas import tpu as pltpu

F32 = jnp.float32
BF16 = jnp.bfloat16
I32 = jnp.int32
U32 = jnp.uint32
HIGHEST = lax.Precision.HIGHEST

GRID_W = 64
EPS = 1e-6
N_MOD = 6
GLA_HEADS = 4
GLA_DK = 64
GLA_DV = 128
GLA_KW = GLA_HEADS * GLA_DK
GLA_VW = GLA_HEADS * GLA_DV
GATE_RANK = 16
GATE_TAU = 16.0
CHUNK = 64
ROPE_BASE = 10000.0
NA_HEADS = 8
NA_DH = 64
NA_W = NA_HEADS * NA_DH
NA_KH = 8
NA_KW = 16
PEER_HEADS = 8
N_KEYS = 128
PEER_DK = 256
PEER_TOPK = 16

LANE = 128
SUBLANE = 8
VREG_ELEMS = LANE * SUBLANE

ROW_TILE = 256
GLA_BLOCK = 256
SELECT_TILE = 256
PEER_TOKENS = 128
MASK_NEG = -1e30

_NT = (((1,), (1,)), ((), ()))


def _cp(sem, vmem_mb):
    return pltpu.CompilerParams(dimension_semantics=sem, vmem_limit_bytes=vmem_mb * 1024 * 1024)


def _norm_mod(x, g, shift, scale):
    ms = jnp.mean(x * x, axis=-1, keepdims=True)
    return (x * lax.rsqrt(ms + EPS) * g) * (1.0 + scale) + shift


def _mod_kernel(c_ref, w_ref, b_ref, o_ref):
    a = c_ref[...]
    a = a * jax.nn.sigmoid(a)
    o_ref[0] = jnp.dot(a, w_ref[0], preferred_element_type=F32, precision=HIGHEST) + b_ref[0]


def _modulation(cin, w_mod, b_mod):
    depth, d, n = w_mod.shape
    rows = cin.shape[0]
    tn = n // 4
    return pl.pallas_call(
        _mod_kernel,
        grid=(depth, n // tn),
        in_specs=[pl.BlockSpec((rows, d), lambda l, j: (0, 0)),
                  pl.BlockSpec((1, d, tn), lambda l, j: (l, 0, j)),
                  pl.BlockSpec((1, 1, tn), lambda l, j: (l, 0, j))],
        out_specs=pl.BlockSpec((1, rows, tn), lambda l, j: (l, 0, j)),
        out_shape=jax.ShapeDtypeStruct((depth, rows, n), F32),
        compiler_params=_cp(("arbitrary", "arbitrary"), 40),
        name="modulation",
    )(cin, w_mod, b_mod.reshape(depth, 1, n))


_PROJ_OUTS = (("qk", 2 * GLA_KW, F32), ("v", GLA_VW, F32), ("r", GLA_VW, F32),
              ("nq", NA_W, BF16), ("nk", NA_W, BF16), ("nv", NA_W, BF16),
              ("gates", None, F32), ("z", LANE, F32))


def _proj_kernel(x_ref, mod_ref, g_ref, w_ref, *out_refs):
    m = mod_ref[0, 0]
    h = _norm_mod(x_ref[0], g_ref[...], m[0:1], m[1:2]).astype(BF16)
    col = 0
    for ref in out_refs:
        n = ref.shape[-1]
        ref[0] = jnp.dot(h, w_ref[:, col:col + n], preferred_element_type=F32).astype(ref.dtype)
        col += n


def _project(xa, modseg, g, w_perm, n_ctx_tiles):
    b, t, d = xa.shape
    outs = [(name, (2 * d if n is None else n), dt) for name, n, dt in _PROJ_OUTS]
    assert sum(n for _, n, _ in outs) == w_perm.shape[1]
    return pl.pallas_call(
        _proj_kernel,
        grid=(b, t // ROW_TILE),
        in_specs=[pl.BlockSpec((1, ROW_TILE, d), lambda i, j: (i, j, 0)),
                  pl.BlockSpec((1, 1, SUBLANE, d), lambda i, j: (i, (j >= n_ctx_tiles).astype(I32), 0, 0)),
                  pl.BlockSpec((1, d), lambda i, j: (0, 0)),
                  pl.BlockSpec(w_perm.shape, lambda i, j: (0, 0))],
        out_specs=[pl.BlockSpec((1, ROW_TILE, n), lambda i, j: (i, j, 0)) for _, n, _ in outs],
        out_shape=[jax.ShapeDtypeStruct((b, t, n), dt) for _, n, dt in outs],
        compiler_params=_cp(("parallel", "arbitrary"), 48),
        name="in_proj",
    )(xa, modseg, g, w_perm)


def _gla_kernel(qk_ref, v_ref, z_ref, cs_ref, sn_ref, wa_ref, ba_ref, o_ref, st_ref, *, reverse):
    @pl.when(pl.program_id(1) == 0)
    def _init():
        st_ref[...] = jnp.zeros_like(st_ref)

    kw2 = 2 * GLA_KW
    blk = qk_ref.shape[1]
    qk = qk_ref[0]
    lane = lax.broadcasted_iota(I32, (1, kw2), 1)
    first = (lane % 32) < 16
    partner = jnp.where(first, pltpu.roll(qk, kw2 - 16, 1), pltpu.roll(qk, 16, 1))
    qk = qk * cs_ref[...] + partner * sn_ref[...]
    zz = jnp.dot(z_ref[0], wa_ref[...], preferred_element_type=F32, precision=HIGHEST) + ba_ref[...]
    la = (jnp.minimum(zz, 0.0) - jnp.log1p(jnp.exp(-jnp.abs(zz)))) * (1.0 / GATE_TAU)

    ri = lax.broadcasted_iota(I32, (CHUNK, CHUNK), 0)
    ci = lax.broadcasted_iota(I32, (CHUNK, CHUNK), 1)
    keep = (ci >= ri) if reverse else (ci <= ri)
    tri = keep.astype(F32)
    khead = lax.broadcasted_iota(I32, (1, GLA_KW), 1) // GLA_DK
    vrow = lax.broadcasted_iota(I32, (GLA_VW, GLA_KW), 0) // GLA_DV
    kcol = lax.broadcasted_iota(I32, (GLA_VW, GLA_KW), 1) // GLA_DK
    same_head = vrow == kcol

    order = range(blk // CHUNK)
    if reverse:
        order = reversed(order)
    for c in order:
        lo = c * CHUNK
        q = qk[lo:lo + CHUNK, :GLA_KW]
        k = qk[lo:lo + CHUNK, GLA_KW:]
        v = v_ref[0, lo:lo + CHUNK, :].astype(BF16)
        bcum = jnp.dot(tri, la[lo:lo + CHUNK], preferred_element_type=F32, precision=HIGHEST)
        bend = bcum[0:1] if reverse else bcum[CHUNK - 1:CHUNK]
        qd = q * jnp.exp(bcum)
        kd = (k * jnp.exp(-bcum)).astype(BF16)
        kd2 = (k * jnp.exp(bend - bcum)).astype(BF16)
        st = st_ref[...]
        o_inter = lax.dot_general(qd.astype(BF16), st.astype(BF16), _NT, preferred_element_type=F32)
        for h in range(GLA_HEADS):
            qm = jnp.where(khead == h, qd, 0.0).astype(BF16)
            s = lax.dot_general(qm, kd, _NT, preferred_element_type=F32)
            s = jnp.where(keep, s, 0.0).astype(BF16)
            vs = slice(h * GLA_DV, (h + 1) * GLA_DV)
            o_ref[0, lo:lo + CHUNK, vs] = (jnp.dot(s, v[:, vs], preferred_element_type=F32)
                                           + o_inter[:, vs])
        upd = jnp.dot(v.T, kd2, preferred_element_type=F32)
        st_ref[...] = st * jnp.exp(bend) + jnp.where(same_head, upd, 0.0)


def _gla(qk, v, z, cs, sn, wa, ba, n_ctx_blocks, reverse):
    b, t, _ = qk.shape
    nblk = t // GLA_BLOCK

    if reverse:
        def blk(j):
            return jnp.where(j < n_ctx_blocks, n_ctx_blocks - 1 - j, nblk - 1 - (j - n_ctx_blocks))
    else:
        def blk(j):
            return j

    tok = lambda n: pl.BlockSpec((1, GLA_BLOCK, n), lambda i, j: (i, blk(j), 0))
    tab = lambda n: pl.BlockSpec((GLA_BLOCK, n), lambda i, j: (blk(j), 0))
    return pl.pallas_call(
        functools.partial(_gla_kernel, reverse=reverse),
        grid=(b, nblk),
        in_specs=[tok(2 * GLA_KW), tok(GLA_VW), tok(LANE), tab(2 * GLA_KW), tab(2 * GLA_KW),
                  pl.BlockSpec((LANE, GLA_KW), lambda i, j: (0, 0)),
                  pl.BlockSpec((1, GLA_KW), lambda i, j: (0, 0))],
        out_specs=tok(GLA_VW),
        out_shape=jax.ShapeDtypeStruct((b, t, GLA_VW), F32),
        scratch_shapes=[pltpu.VMEM((GLA_VW, GLA_KW), F32)],
        compiler_params=_cp(("parallel", "arbitrary"), 40),
        name="gla_bwd" if reverse else "gla_fwd",
    )(qk, v, z, cs, sn, wa, ba)


def _na_kernel(*refs):
    q_ref = refs[0]
    k_refs = refs[1:1 + NA_KH]
    v_refs = refs[1 + NA_KH:1 + 2 * NA_KH]
    kc_ref, vc_ref, tb_ref, o_ref = refs[1 + 2 * NA_KH:]
    scale = NA_DH ** -0.5
    half = lax.broadcasted_iota(I32, (1, LANE), 1) // NA_DH
    for p in range(NA_HEADS // 2):
        ls = slice(p * LANE, (p + 1) * LANE)
        qp = q_ref[0, :, ls]
        kw = jnp.concatenate([r[0, :, ls] for r in k_refs], axis=0)
        vw = jnp.concatenate([r[0, :, ls] for r in v_refs], axis=0)
        kc = kc_ref[0, :, ls]
        vc = vc_ref[0, :, ls]
        outs = []
        for s in range(2):
            qm = jnp.where(half == s, qp, jnp.zeros_like(qp))
            sw = lax.dot_general(qm, kw, _NT, preferred_element_type=F32) * scale + tb_ref[0, 2 * p + s]
            sc = lax.dot_general(qm, kc, _NT, preferred_element_type=F32) * scale
            m = jnp.maximum(jnp.max(sw, axis=-1, keepdims=True), jnp.max(sc, axis=-1, keepdims=True))
            pw = jnp.exp(sw - m)
            pc = jnp.exp(sc - m)
            den = jnp.sum(pw, axis=-1, keepdims=True) + jnp.sum(pc, axis=-1, keepdims=True)
            o = (jnp.dot(pw.astype(BF16), vw, preferred_element_type=F32)
                 + jnp.dot(pc.astype(BF16), vc, preferred_element_type=F32))
            outs.append(o / den)
        o_ref[0, :, ls] = jnp.where(half == 0, outs[0], outs[1]).astype(o_ref.dtype)


def _na_bias_tables(rpb):
    w = jnp.arange(GRID_W, dtype=I32)
    c0 = jnp.clip(w - NA_KW // 2, 0, GRID_W - NA_KW)
    cp = w[None, :]
    valid = (cp >= c0[:, None]) & (cp < c0[:, None] + NA_KW)
    off = jnp.clip(cp - w[:, None] + (NA_KW - 1), 0, 2 * NA_KW - 2)
    t = jnp.where(valid[None, None], rpb[:, :, off], MASK_NEG)
    tabs = [jnp.concatenate([t[:, s + a] for a in range(NA_KH)], axis=-1) for s in range(NA_KH)]
    tabs.append(jnp.full_like(tabs[0], MASK_NEG))
    return jnp.stack(tabs, axis=0).astype(F32)


def _na(nq, nk, nv, tables, lc):
    b, t, _ = nq.shape
    n_ctx = lc // GRID_W
    rows = (t - lc) // GRID_W
    assert rows >= NA_KH

    def row0(j):
        r = jnp.maximum(j - n_ctx, 0)
        return r, jnp.clip(r - NA_KH // 2, 0, rows - NA_KH)

    def win(a):
        return pl.BlockSpec((1, GRID_W, NA_W), lambda i, j: (i, n_ctx + row0(j)[1] + a, 0))

    def tab_idx(i, j):
        r, r0 = row0(j)
        return (jnp.where(j < n_ctx, NA_KH, r0 - r + NA_KH - 1), 0, 0, 0)

    return pl.pallas_call(
        _na_kernel,
        grid=(b, t // GRID_W),
        in_specs=([pl.BlockSpec((1, GRID_W, NA_W), lambda i, j: (i, j, 0))]
                  + [win(a) for a in range(NA_KH)] + [win(a) for a in range(NA_KH)]
                  + [pl.BlockSpec((1, lc, NA_W), lambda i, j: (i, 0, 0))] * 2
                  + [pl.BlockSpec((1, NA_HEADS, GRID_W, NA_KH * GRID_W), tab_idx)]),
        out_specs=pl.BlockSpec((1, GRID_W, NA_W), lambda i, j: (i, j, 0)),
        out_shape=jax.ShapeDtypeStruct((b, t, NA_W), BF16),
        compiler_params=_cp(("parallel", "arbitrary"), 40),
        name="natten",
    )(nq, *([nk] * NA_KH), *([nv] * NA_KH), nk, nv, tables)


def _merge_kernel(x_ref, of_ref, ob_ref, r_ref, na_ref, gt_ref, mod_ref, gg_ref, wpg_ref, wpn_ref, wo_ref,
                  o_ref):
    d = x_ref.shape[-1]
    o = of_ref[0] + ob_ref[0]
    r = r_ref[0]
    parts = []
    for h in range(GLA_HEADS):
        oh = o[:, h * GLA_DV:(h + 1) * GLA_DV]
        ms = jnp.mean(oh * oh, axis=-1, keepdims=True)
        parts.append(oh * lax.rsqrt(ms + EPS) * gg_ref[...])
    on = jnp.concatenate(parts, axis=-1) * (r * jax.nn.sigmoid(r))
    ya = jnp.dot(on.astype(BF16), wpg_ref[...], preferred_element_type=F32)
    yb = jnp.dot(na_ref[0], wpn_ref[...], preferred_element_type=F32)
    gt = gt_ref[0]
    y = jax.nn.sigmoid(gt[:, :d]) * ya + jax.nn.sigmoid(gt[:, d:]) * yb
    y = jnp.dot(y.astype(BF16), wo_ref[...], preferred_element_type=F32)
    o_ref[0] = x_ref[0] + mod_ref[0, 0][2:3] * y


def _merge(xa, o_f, o_b, r, na, gates, modseg, g_gla, wpg, wpn, wo, n_ctx_tiles):
    b, t, d = xa.shape
    tok = lambda n: pl.BlockSpec((1, ROW_TILE, n), lambda i, j: (i, j, 0))
    full = lambda a: pl.BlockSpec(a.shape, lambda i, j: (0,) * a.ndim)
    return pl.pallas_call(
        _merge_kernel,
        grid=(b, t // ROW_TILE),
        in_specs=[tok(d), tok(GLA_VW), tok(GLA_VW), tok(GLA_VW), tok(NA_W), tok(2 * d),
                  pl.BlockSpec((1, 1, SUBLANE, d), lambda i, j: (i, (j >= n_ctx_tiles).astype(I32), 0, 0)),
                  full(g_gla), full(wpg), full(wpn), full(wo)],
        out_specs=tok(d),
        out_shape=jax.ShapeDtypeStruct((b, t, d), F32),
        compiler_params=_cp(("parallel", "arbitrary"), 40),
        name="merge",
    )(xa, o_f, o_b, r, na, gates, modseg, g_gla, wpg, wpn, wo)


def _topk_rows(s, ids, k):
    rows = s.shape[0]
    rid = lax.broadcasted_iota(I32, s.shape, 0).astype(F32)
    vals, out_ids = [], []
    for _ in range(k):
        m = jnp.max(s, axis=0, keepdims=True)
        pos = jnp.min(jnp.where(s == m, rid, float(rows)), axis=0, keepdims=True)
        hit = rid == pos
        vals.append(m)
        out_ids.append(pos if ids is None else jnp.max(jnp.where(hit, ids, -1.0), axis=0, keepdims=True))
        s = jnp.where(hit, -jnp.inf, s)
    return jnp.concatenate(vals, axis=0), jnp.concatenate(out_ids, axis=0)


def _staircase_candidates(v0, i0, v1, i1):
    k = PEER_TOPK
    sub = lax.broadcasted_iota(I32, (SUBLANE, v0.shape[1]), 0)
    vals = [v0[0:1] + v1]
    ids = [i0[0:1] * float(N_KEYS) + i1]
    for a in range(1, k // 2):
        nb = k // (a + 1)
        live = sub < nb
        vals.append(jnp.where(live, v0[a:a + 1] + v1[:SUBLANE], -jnp.inf))
        ids.append(i0[a:a + 1] * float(N_KEYS) + i1[:SUBLANE])
    vals.append(v0[k // 2:] + v1[0:1])
    ids.append(i0[k // 2:] * float(N_KEYS) + i1[0:1])
    return jnp.concatenate(vals, axis=0), jnp.concatenate(ids, axis=0)


def _peer1_kernel(x_ref, mod_ref, g_ref, wq_ref, sk_ref, idx_ref, pair_ref, gate_ref):
    m = mod_ref[0, 0]
    h = _norm_mod(x_ref[0], g_ref[...], m[3:4], m[4:5]).astype(BF16)
    half = PEER_DK // 2
    chosen = []
    for hd in range(PEER_HEADS):
        tops = []
        for p in range(2):
            col = (2 * hd + p) * half
            qp = jnp.dot(h, wq_ref[:, col:col + half], preferred_element_type=F32).astype(BF16)
            st = lax.dot_general(sk_ref[hd, p], qp, _NT, preferred_element_type=F32)
            tops.append(_topk_rows(st, None, PEER_TOPK))
        (v0, i0), (v1, i1) = tops
        cand, cidx = _staircase_candidates(v0, i0, v1, i1)
        best, experts = _topk_rows(cand, cidx, PEER_TOPK)
        e = jnp.exp(best - best[0:1])
        rs = slice(hd * PEER_TOPK, (hd + 1) * PEER_TOPK)
        gate_ref[0, rs, :] = e / jnp.sum(e, axis=0, keepdims=True)
        idx_ref[0, rs, :] = experts.astype(I32)
        chosen.append(experts)
    pair_ref[0] = lax.shift_right_logical(jnp.concatenate(chosen, axis=0).T.astype(I32), 1) * SUBLANE


def _peer_select(xa, modseg, g, wq, sk, lc):
    b, t, d = xa.shape
    ne = PEER_HEADS * PEER_TOPK
    tm = SELECT_TILE
    n_ctx_tiles = lc // tm
    return pl.pallas_call(
        _peer1_kernel,
        grid=(b, t // tm),
        in_specs=[pl.BlockSpec((1, tm, d), lambda i, j: (i, j, 0)),
                  pl.BlockSpec((1, 1, SUBLANE, d), lambda i, j: (i, (j >= n_ctx_tiles).astype(I32), 0, 0)),
                  pl.BlockSpec((1, d), lambda i, j: (0, 0)),
                  pl.BlockSpec(wq.shape, lambda i, j: (0, 0)),
                  pl.BlockSpec(sk.shape, lambda i, j: (0, 0, 0, 0))],
        out_specs=[pl.BlockSpec((1, ne, tm), lambda i, j: (i, 0, j)),
                   pl.BlockSpec((1, tm, ne), lambda i, j: (i, j, 0)),
                   pl.BlockSpec((1, ne, tm), lambda i, j: (i, 0, j))],
        out_shape=[jax.ShapeDtypeStruct((b, ne, t), I32), jax.ShapeDtypeStruct((b, t, ne), I32),
                   jax.ShapeDtypeStruct((b, ne, t), F32)],
        compiler_params=_cp(("parallel", "arbitrary"), 40),
        name="peer_select",
    )(xa, modseg, g, wq, sk)


PAIR_ROWS = 2 * SUBLANE


def _gather_tiles(tbl_ref, pair_ref, t, ne):
    base = pl.multiple_of(t * ne, ne)
    tiles = []
    for j in range(ne):
        row = pl.multiple_of(pair_ref[base + j], SUBLANE)
        tiles.append(pltpu.bitcast(tbl_ref[pl.ds(row, SUBLANE), :], BF16))
    return jnp.concatenate(tiles, axis=0)


TOKENS_PER_TRIP = 8


def _for_token_groups(tb, body):
    def trip(i, carry):
        for u in range(TOKENS_PER_TRIP):
            body(TOKENS_PER_TRIP * i + u)
        return carry

    lax.fori_loop(0, tb // TOKENS_PER_TRIP, trip, 0)


def _peer2_kernel(pair_ref, idx_ref, x_ref, mod_ref, g_ref, gate_ref, tbl_ref, cc_ref, acc_ref):
    tb, kk = acc_ref.shape
    ne = kk // PAIR_ROWS
    d = x_ref.shape[1] * x_ref.shape[2]
    a = g_ref[0] * (1.0 + mod_ref[4])
    shift = mod_ref[3]
    lane = lax.broadcasted_iota(I32, (SUBLANE, kk), 1)
    sub = lax.broadcasted_iota(I32, (SUBLANE, kk), 0)
    diag = ((lane % PAIR_ROWS) // 2) == sub

    def token(t):
        x = x_ref[t]
        ms = jnp.sum(x * x, keepdims=True) * (1.0 / d)
        h = x * lax.rsqrt(ms + EPS) * a + shift
        hb = jnp.concatenate([h, jnp.zeros_like(h)], axis=0).astype(BF16)
        tiles = _gather_tiles(tbl_ref, pair_ref, t, ne)
        out = lax.dot_general(hb, tiles, _NT, preferred_element_type=F32)
        acc_ref[pl.ds(t, 1), :] = jnp.sum(jnp.where(diag, out[:SUBLANE], 0.0), axis=0, keepdims=True)

    _for_token_groups(tb, token)

    kq = lax.broadcasted_iota(I32, (2 * ne, kk), 1)
    rq = lax.broadcasted_iota(I32, (2 * ne, kk), 0)
    sel = (((kq // PAIR_ROWS) == (rq % ne)) & ((kq % 2) == (rq // ne))).astype(F32)
    at = lax.dot_general(sel, acc_ref[...], _NT, preferred_element_type=F32, precision=HIGHEST)
    par = idx_ref[0] & 1
    act = jnp.where(par == 0, at[:ne], at[ne:])
    coef = jax.nn.gelu(act, approximate=True) * gate_ref[0]
    ke = lax.broadcasted_iota(I32, (ne, kk), 1)
    re = lax.broadcasted_iota(I32, (ne, kk), 0)
    expand = ((ke // PAIR_ROWS) == re).astype(BF16)
    cexp = jnp.dot(coef.T.astype(BF16), expand, preferred_element_type=F32)
    pexp = jnp.dot(par.astype(F32).T.astype(BF16), expand, preferred_element_type=F32)
    lanepar = (lax.broadcasted_iota(I32, (tb, kk), 1) % 2).astype(F32)
    cc_ref[...] = jnp.where(pexp == lanepar, cexp, 0.0)


def _peer3_kernel(pair_ref, cc_ref, x_ref, mod_ref, tbl_ref, o_ref):
    tb, kk = cc_ref.shape
    ne = kk // PAIR_ROWS
    ga = mod_ref[5]
    lane = lax.broadcasted_iota(I32, (PAIR_ROWS, kk), 1)
    sub = lax.broadcasted_iota(I32, (PAIR_ROWS, kk), 0)
    diag = ((lane % PAIR_ROWS) // 2) == sub

    def token(t):
        row = cc_ref[pl.ds(t, 1), :]
        cm = jnp.where(diag, jnp.broadcast_to(row, (PAIR_ROWS, kk)), 0.0).astype(BF16)
        tiles = _gather_tiles(tbl_ref, pair_ref, t, ne)
        o = jnp.dot(cm, tiles, preferred_element_type=F32)
        o_ref[t] = x_ref[t] + ga * o[:SUBLANE]

    _for_token_groups(tb, token)


def _peer_blocks(b, t, lc):
    nj = t // PEER_TOKENS
    n_ctx = lc // PEER_TOKENS
    tokb = pl.BlockSpec((PEER_TOKENS, SUBLANE, LANE), lambda n: (n, 0, 0))
    modb = pl.BlockSpec((SUBLANE, SUBLANE, LANE), lambda n: ((n // nj) * 2 + ((n % nj) >= n_ctx).astype(I32), 0, 0))
    return nj, tokb, modb


def _peer_act(x3, pair2, idx, gate, modv, g3, tbl, b, t, lc):
    ne = PEER_HEADS * PEER_TOPK
    kk = ne * PAIR_ROWS
    nj, tokb, modb = _peer_blocks(b, t, lc)
    expb = pl.BlockSpec((1, ne, PEER_TOKENS), lambda n: (n // nj, 0, n % nj))
    return pl.pallas_call(
        _peer2_kernel,
        grid=(b * nj,),
        in_specs=[pl.BlockSpec((PEER_TOKENS * ne,), lambda n: (n,), memory_space=pltpu.SMEM),
                  expb, tokb, modb,
                  pl.BlockSpec((1, SUBLANE, LANE), lambda n: (0, 0, 0)),
                  expb,
                  pl.BlockSpec(tbl.shape, lambda n: (0, 0), pipeline_mode=pl.Buffered(1))],
        out_specs=pl.BlockSpec((PEER_TOKENS, kk), lambda n: (n, 0)),
        out_shape=jax.ShapeDtypeStruct((b * t, kk), F32),
        scratch_shapes=[pltpu.VMEM((PEER_TOKENS, kk), F32)],
        compiler_params=_cp(("arbitrary",), 56),
        name="peer_act",
    )(pair2, idx, x3, modv, g3, gate, tbl)


def _peer_out(x3, pair2, cc, modv, tbl, b, t, lc):
    ne = PEER_HEADS * PEER_TOPK
    kk = ne * PAIR_ROWS
    nj, tokb, modb = _peer_blocks(b, t, lc)
    return pl.pallas_call(
        _peer3_kernel,
        grid=(b * nj,),
        in_specs=[pl.BlockSpec((PEER_TOKENS * ne,), lambda n: (n,), memory_space=pltpu.SMEM),
                  pl.BlockSpec((PEER_TOKENS, kk), lambda n: (n, 0)),
                  tokb, modb,
                  pl.BlockSpec(tbl.shape, lambda n: (0, 0), pipeline_mode=pl.Buffered(1))],
        out_specs=tokb,
        out_shape=jax.ShapeDtypeStruct(x3.shape, F32),
        compiler_params=_cp(("arbitrary",), 56),
        name="peer_out",
    )(pair2, cc, x3, modv, tbl)


def _pack_experts(w):
    e, d = w.shape
    assert d == VREG_ELEMS
    bits = lax.bitcast_convert_type(w.astype(BF16), jnp.uint16).astype(U32).reshape(e // 2, 2, SUBLANE, LANE)
    return (bits[:, 0] | (bits[:, 1] << 16)).reshape(e // 2 * SUBLANE, LANE)


def _peer(xa, modseg, modv, g2, w_query, sub_keys, expert_u, expert_v, lc):
    b, t, d = xa.shape
    idx, pair, gate = _peer_select(xa, modseg, g2.reshape(1, d), w_query.astype(BF16), sub_keys.astype(BF16), lc)
    pair2 = pair.reshape(b * t * idx.shape[1])
    x3 = xa.reshape(b * t, SUBLANE, LANE)
    cc = _peer_act(x3, pair2, idx, gate, modv, g2.reshape(1, SUBLANE, LANE), _pack_experts(expert_u), b, t, lc)
    x3 = _peer_out(x3, pair2, cc, modv, _pack_experts(expert_v), b, t, lc)
    return x3.reshape(b, t, d)


def _final_kernel(x_ref, g_ref, o_ref):
    x = x_ref[0]
    ms = jnp.mean(x * x, axis=-1, keepdims=True)
    o_ref[0] = x * lax.rsqrt(ms + EPS) * g_ref[...]


def _final_norm(xa, g, lc):
    b, t, d = xa.shape
    off = lc // ROW_TILE
    return pl.pallas_call(
        _final_kernel,
        grid=(b, (t - lc) // ROW_TILE),
        in_specs=[pl.BlockSpec((1, ROW_TILE, d), lambda i, j: (i, j + off, 0)),
                  pl.BlockSpec((1, d), lambda i, j: (0, 0))],
        out_specs=pl.BlockSpec((1, ROW_TILE, d), lambda i, j: (i, j, 0)),
        out_shape=jax.ShapeDtypeStruct((b, t - lc, d), F32),
        compiler_params=_cp(("parallel", "arbitrary"), 32),
        name="final_norm",
    )(xa, g)


def _rope_tables(l, lc):
    t = jnp.arange(l, dtype=I32)
    quarter = GLA_DK // 4
    freqs = ROPE_BASE ** (-jnp.arange(quarter, dtype=F32) / quarter)
    ang_r = (t // GRID_W).astype(F32)[:, None] * freqs
    ang_c = (t % GRID_W).astype(F32)[:, None] * freqs
    cos = jnp.concatenate([jnp.cos(ang_r)] * 2 + [jnp.cos(ang_c)] * 2, axis=-1)
    sin = jnp.concatenate([-jnp.sin(ang_r), jnp.sin(ang_r), -jnp.sin(ang_c), jnp.sin(ang_c)], axis=-1)
    cos = jnp.tile(cos, (1, GLA_HEADS))
    sin = jnp.tile(sin, (1, GLA_HEADS))
    qs = GLA_DK ** -0.5
    cs = jnp.concatenate([cos * qs, cos], axis=-1)
    sn = jnp.concatenate([sin * qs, sin], axis=-1)
    cs_ctx = jnp.concatenate([jnp.full((lc, GLA_KW), qs, F32), jnp.ones((lc, GLA_KW), F32)], axis=-1)
    return (jnp.concatenate([cs_ctx, cs], axis=0),
            jnp.concatenate([jnp.zeros((lc, 2 * GLA_KW), F32), sn], axis=0))


def _permute_w_in(w, d):
    sizes = (GLA_KW, GLA_KW, GLA_VW, 2 * GATE_RANK, GLA_VW, NA_W, NA_W, NA_W, 2 * d)
    offs = [0]
    for s in sizes:
        offs.append(offs[-1] + s)
    q, k, v, z, r, nq, nk, nv, gates = [w[:, offs[i]:offs[i + 1]] for i in range(len(sizes))]
    zpad = jnp.pad(z, ((0, 0), (0, LANE - 2 * GATE_RANK)))
    return jnp.concatenate([q, k, v, r, nq, nk, nv, gates, zpad], axis=1).astype(BF16)


def _gate_weights(w_a, b_a, first_row):
    wa = jnp.zeros((LANE, GLA_KW), F32).at[first_row:first_row + GATE_RANK].set(w_a)
    return wa, b_a.reshape(1, GLA_KW)


def kernel(x, c, ctx, c_ctx, w_mod, b_mod, g_norm1, w_in, w_alpha_f, b_alpha_f, w_alpha_b, b_alpha_b, g_gla, rpb,
           w_proj_gla, w_proj_na, w_out, g_norm2, w_query, sub_keys, expert_u, expert_v, g_final):
    b, l, d = x.shape
    lc = ctx.shape[1]
    depth = w_mod.shape[0]
    t = lc + l
    assert d == VREG_ELEMS and lc % ROW_TILE == 0 and l % ROW_TILE == 0 and l % GRID_W == 0
    n_ctx_tiles = lc // ROW_TILE

    rows = -(-(b + 1) // SUBLANE) * SUBLANE
    cin = jnp.zeros((rows, d), F32).at[:b].set(c).at[b].set(c_ctx)
    mod_all = _modulation(cin, w_mod, b_mod)

    cs, sn = _rope_tables(l, lc)
    xa = jnp.concatenate([ctx, x], axis=1)

    for i in range(depth):
        mod = mod_all[i].reshape(rows, N_MOD, d)
        modseg = jnp.stack([jnp.broadcast_to(mod[b], (b, N_MOD, d)), mod[:b]], axis=1)
        modseg = jnp.pad(modseg, ((0, 0), (0, 0), (0, SUBLANE - N_MOD), (0, 0)))
        modv = modseg.reshape(b * 2 * SUBLANE, SUBLANE, LANE)

        qk, v, r, nq, nk, nv, gates, z = _project(xa, modseg, g_norm1[i].reshape(1, d),
                                                  _permute_w_in(w_in[i], d), n_ctx_tiles)
        waf, baf = _gate_weights(w_alpha_f[i], b_alpha_f[i], 0)
        wab, bab = _gate_weights(w_alpha_b[i], b_alpha_b[i], GATE_RANK)
        o_f = _gla(qk, v, z, cs, sn, waf, baf, lc // GLA_BLOCK, reverse=False)
        o_b = _gla(qk, v, z, cs, sn, wab, bab, lc // GLA_BLOCK, reverse=True)
        na = _na(nq, nk, nv, _na_bias_tables(rpb[i]), lc)
        xa = _merge(xa, o_f, o_b, r, na, gates, modseg, g_gla[i].reshape(1, GLA_DV),
                    w_proj_gla[i].astype(BF16), w_proj_na[i].astype(BF16), w_out[i].astype(BF16), n_ctx_tiles)

        xa = _peer(xa, modseg, modv, g_norm2[i], w_query[i], sub_keys[i], expert_u[i], expert_v[i], lc)

    return _final_norm(xa, g_final.reshape(1, d), lc)
```

```python
import functools

import jax
import jax.numpy as jnp
from jax import lax
from jax.experimental import pallas as pl
from jax.experimental.pallas import tpu as pltpu

F32 = jnp.float32
BF16 = jnp.bfloat16
I32 = jnp.int32
U32 = jnp.uint32
HIGHEST = lax.Precision.HIGHEST

GRID_W = 64
EPS = 1e-6
N_MOD = 6
GLA_HEADS = 4
GLA_DK = 64
GLA_DV = 128
GLA_KW = GLA_HEADS * GLA_DK
GLA_VW = GLA_HEADS * GLA_DV
GATE_RANK = 16
GATE_TAU = 16.0
CHUNK = 64
ROPE_BASE = 10000.0
NA_HEADS = 8
NA_DH = 64
NA_W = NA_HEADS * NA_DH
NA_KH = 8
NA_KW = 16
PEER_HEADS = 8
N_KEYS = 128
PEER_DK = 256
PEER_TOPK = 16

LANE = 128
SUBLANE = 8
VREG_ELEMS = LANE * SUBLANE

ROW_TILE = 256
GLA_BLOCK = 256
SELECT_TILE = 256
PEER_TOKENS = 128
MASK_NEG = -1e30

_NT = (((1,), (1,)), ((), ()))


def _cp(sem, vmem_mb):
    return pltpu.CompilerParams(dimension_semantics=sem, vmem_limit_bytes=vmem_mb * 1024 * 1024)


def _norm_mod(x, g, shift, scale):
    ms = jnp.mean(x * x, axis=-1, keepdims=True)
    return (x * lax.rsqrt(ms + EPS) * g) * (1.0 + scale) + shift


def _mod_kernel(c_ref, w_ref, b_ref, o_ref):
    a = c_ref[...]
    a = a * jax.nn.sigmoid(a)
    o_ref[0] = jnp.dot(a, w_ref[0], preferred_element_type=F32, precision=HIGHEST) + b_ref[0]


def _modulation(cin, w_mod, b_mod):
    depth, d, n = w_mod.shape
    rows = cin.shape[0]
    tn = n // 4
    return pl.pallas_call(
        _mod_kernel,
        grid=(depth, n // tn),
        in_specs=[pl.BlockSpec((rows, d), lambda l, j: (0, 0)),
                  pl.BlockSpec((1, d, tn), lambda l, j: (l, 0, j)),
                  pl.BlockSpec((1, 1, tn), lambda l, j: (l, 0, j))],
        out_specs=pl.BlockSpec((1, rows, tn), lambda l, j: (l, 0, j)),
        out_shape=jax.ShapeDtypeStruct((depth, rows, n), F32),
        compiler_params=_cp(("arbitrary", "arbitrary"), 40),
        name="modulation",
    )(cin, w_mod, b_mod.reshape(depth, 1, n))


_PROJ_OUTS = (("qk", 2 * GLA_KW, F32), ("v", GLA_VW, F32), ("r", GLA_VW, F32),
              ("nq", NA_W, BF16), ("nk", NA_W, BF16), ("nv", NA_W, BF16),
              ("gates", None, F32), ("z", LANE, F32))


def _proj_kernel(x_ref, mod_ref, g_ref, w_ref, *out_refs):
    m = mod_ref[0, 0]
    h = _norm_mod(x_ref[0], g_ref[...], m[0:1], m[1:2]).astype(BF16)
    col = 0
    for ref in out_refs:
        n = ref.shape[-1]
        ref[0] = jnp.dot(h, w_ref[:, col:col + n], preferred_element_type=F32).astype(ref.dtype)
        col += n


def _project(xa, modseg, g, w_perm, n_ctx_tiles):
    b, t, d = xa.shape
    outs = [(name, (2 * d if n is None else n), dt) for name, n, dt in _PROJ_OUTS]
    assert sum(n for _, n, _ in outs) == w_perm.shape[1]
    return pl.pallas_call(
        _proj_kernel,
        grid=(b, t // ROW_TILE),
        in_specs=[pl.BlockSpec((1, ROW_TILE, d), lambda i, j: (i, j, 0)),
                  pl.BlockSpec((1, 1, SUBLANE, d), lambda i, j: (i, (j >= n_ctx_tiles).astype(I32), 0, 0)),
                  pl.BlockSpec((1, d), lambda i, j: (0, 0)),
                  pl.BlockSpec(w_perm.shape, lambda i, j: (0, 0))],
        out_specs=[pl.BlockSpec((1, ROW_TILE, n), lambda i, j: (i, j, 0)) for _, n, _ in outs],
        out_shape=[jax.ShapeDtypeStruct((b, t, n), dt) for _, n, dt in outs],
        compiler_params=_cp(("parallel", "arbitrary"), 48),
        name="in_proj",
    )(xa, modseg, g, w_perm)


def _gla_kernel(qk_ref, v_ref, z_ref, cs_ref, sn_ref, wa_ref, ba_ref, o_ref, st_ref, *, reverse):
    @pl.when(pl.program_id(1) == 0)
    def _init():
        st_ref[...] = jnp.zeros_like(st_ref)

    kw2 = 2 * GLA_KW
    blk = qk_ref.shape[1]
    qk = qk_ref[0]
    lane = lax.broadcasted_iota(I32, (1, kw2), 1)
    first = (lane % 32) < 16
    partner = jnp.where(first, pltpu.roll(qk, kw2 - 16, 1), pltpu.roll(qk, 16, 1))
    qk = qk * cs_ref[...] + partner * sn_ref[...]
    zz = jnp.dot(z_ref[0], wa_ref[...], preferred_element_type=F32, precision=HIGHEST) + ba_ref[...]
    la = (jnp.minimum(zz, 0.0) - jnp.log1p(jnp.exp(-jnp.abs(zz)))) * (1.0 / GATE_TAU)

    ri = lax.broadcasted_iota(I32, (CHUNK, CHUNK), 0)
    ci = lax.broadcasted_iota(I32, (CHUNK, CHUNK), 1)
    keep = (ci >= ri) if reverse else (ci <= ri)
    tri = keep.astype(F32)
    khead = lax.broadcasted_iota(I32, (1, GLA_KW), 1) // GLA_DK
    vrow = lax.broadcasted_iota(I32, (GLA_VW, GLA_KW), 0) // GLA_DV
    kcol = lax.broadcasted_iota(I32, (GLA_VW, GLA_KW), 1) // GLA_DK
    same_head = vrow == kcol

    order = range(blk // CHUNK)
    if reverse:
        order = reversed(order)
    for c in order:
        lo = c * CHUNK
        q = qk[lo:lo + CHUNK, :GLA_KW]
        k = qk[lo:lo + CHUNK, GLA_KW:]
        v = v_ref[0, lo:lo + CHUNK, :].astype(BF16)
        bcum = jnp.dot(tri, la[lo:lo + CHUNK], preferred_element_type=F32, precision=HIGHEST)
        bend = bcum[0:1] if reverse else bcum[CHUNK - 1:CHUNK]
        qd = q * jnp.exp(bcum)
        kd = (k * jnp.exp(-bcum)).astype(BF16)
        kd2 = (k * jnp.exp(bend - bcum)).astype(BF16)
        st = st_ref[...]
        o_inter = lax.dot_general(qd.astype(BF16), st.astype(BF16), _NT, preferred_element_type=F32)
        for h in range(GLA_HEADS):
            qm = jnp.where(khead == h, qd, 0.0).astype(BF16)
            s = lax.dot_general(qm, kd, _NT, preferred_element_type=F32)
            s = jnp.where(keep, s, 0.0).astype(BF16)
            vs = slice(h * GLA_DV, (h + 1) * GLA_DV)
            o_ref[0, lo:lo + CHUNK, vs] = (jnp.dot(s, v[:, vs], preferred_element_type=F32)
                                           + o_inter[:, vs])
        upd = jnp.dot(v.T, kd2, preferred_element_type=F32)
        st_ref[...] = st * jnp.exp(bend) + jnp.where(same_head, upd, 0.0)


def _gla(qk, v, z, cs, sn, wa, ba, n_ctx_blocks, reverse):
    b, t, _ = qk.shape
    nblk = t // GLA_BLOCK

    if reverse:
        def blk(j):
            return jnp.where(j < n_ctx_blocks, n_ctx_blocks - 1 - j, nblk - 1 - (j - n_ctx_blocks))
    else:
        def blk(j):
            return j

    tok = lambda n: pl.BlockSpec((1, GLA_BLOCK, n), lambda i, j: (i, blk(j), 0))
    tab = lambda n: pl.BlockSpec((GLA_BLOCK, n), lambda i, j: (blk(j), 0))
    return pl.pallas_call(
        functools.partial(_gla_kernel, reverse=reverse),
        grid=(b, nblk),
        in_specs=[tok(2 * GLA_KW), tok(GLA_VW), tok(LANE), tab(2 * GLA_KW), tab(2 * GLA_KW),
                  pl.BlockSpec((LANE, GLA_KW), lambda i, j: (0, 0)),
                  pl.BlockSpec((1, GLA_KW), lambda i, j: (0, 0))],
        out_specs=tok(GLA_VW),
        out_shape=jax.ShapeDtypeStruct((b, t, GLA_VW), F32),
        scratch_shapes=[pltpu.VMEM((GLA_VW, GLA_KW), F32)],
        compiler_params=_cp(("parallel", "arbitrary"), 40),
        name="gla_bwd" if reverse else "gla_fwd",
    )(qk, v, z, cs, sn, wa, ba)


def _na_kernel(*refs):
    q_ref = refs[0]
    k_refs = refs[1:1 + NA_KH]
    v_refs = refs[1 + NA_KH:1 + 2 * NA_KH]
    kc_ref, vc_ref, tb_ref, o_ref = refs[1 + 2 * NA_KH:]
    scale = NA_DH ** -0.5
    half = lax.broadcasted_iota(I32, (1, LANE), 1) // NA_DH
    for p in range(NA_HEADS // 2):
        ls = slice(p * LANE, (p + 1) * LANE)
        qp = q_ref[0, :, ls]
        kw = jnp.concatenate([r[0, :, ls] for r in k_refs], axis=0)
        vw = jnp.concatenate([r[0, :, ls] for r in v_refs], axis=0)
        kc = kc_ref[0, :, ls]
        vc = vc_ref[0, :, ls]
        outs = []
        for s in range(2):
            qm = jnp.where(half == s, qp, jnp.zeros_like(qp))
            sw = lax.dot_general(qm, kw, _NT, preferred_element_type=F32) * scale + tb_ref[0, 2 * p + s]
            sc = lax.dot_general(qm, kc, _NT, preferred_element_type=F32) * scale
            m = jnp.maximum(jnp.max(sw, axis=-1, keepdims=True), jnp.max(sc, axis=-1, keepdims=True))
            pw = jnp.exp(sw - m)
            pc = jnp.exp(sc - m)
            den = jnp.sum(pw, axis=-1, keepdims=True) + jnp.sum(pc, axis=-1, keepdims=True)
            o = (jnp.dot(pw.astype(BF16), vw, preferred_element_type=F32)
                 + jnp.dot(pc.astype(BF16), vc, preferred_element_type=F32))
            outs.append(o / den)
        o_ref[0, :, ls] = jnp.where(half == 0, outs[0], outs[1]).astype(o_ref.dtype)


def _na_bias_tables(rpb):
    w = jnp.arange(GRID_W, dtype=I32)
    c0 = jnp.clip(w - NA_KW // 2, 0, GRID_W - NA_KW)
    cp = w[None, :]
    valid = (cp >= c0[:, None]) & (cp < c0[:, None] + NA_KW)
    off = jnp.clip(cp - w[:, None] + (NA_KW - 1), 0, 2 * NA_KW - 2)
    t = jnp.where(valid[None, None], rpb[:, :, off], MASK_NEG)
    tabs = [jnp.concatenate([t[:, s + a] for a in range(NA_KH)], axis=-1) for s in range(NA_KH)]
    tabs.append(jnp.full_like(tabs[0], MASK_NEG))
    return jnp.stack(tabs, axis=0).astype(F32)


def _na(nq, nk, nv, tables, lc):
    b, t, _ = nq.shape
    n_ctx = lc // GRID_W
    rows = (t - lc) // GRID_W
    assert rows >= NA_KH

    def row0(j):
        r = jnp.maximum(j - n_ctx, 0)
        return r, jnp.clip(r - NA_KH // 2, 0, rows - NA_KH)

    def win(a):
        return pl.BlockSpec((1, GRID_W, NA_W), lambda i, j: (i, n_ctx + row0(j)[1] + a, 0))

    def tab_idx(i, j):
        r, r0 = row0(j)
        return (jnp.where(j < n_ctx, NA_KH, r0 - r + NA_KH - 1), 0, 0, 0)

    return pl.pallas_call(
        _na_kernel,
        grid=(b, t // GRID_W),
        in_specs=([pl.BlockSpec((1, GRID_W, NA_W), lambda i, j: (i, j, 0))]
                  + [win(a) for a in range(NA_KH)] + [win(a) for a in range(NA_KH)]
                  + [pl.BlockSpec((1, lc, NA_W), lambda i, j: (i, 0, 0))] * 2
                  + [pl.BlockSpec((1, NA_HEADS, GRID_W, NA_KH * GRID_W), tab_idx)]),
        out_specs=pl.BlockSpec((1, GRID_W, NA_W), lambda i, j: (i, j, 0)),
        out_shape=jax.ShapeDtypeStruct((b, t, NA_W), BF16),
        compiler_params=_cp(("parallel", "arbitrary"), 40),
        name="natten",
    )(nq, *([nk] * NA_KH), *([nv] * NA_KH), nk, nv, tables)


def _merge_kernel(x_ref, of_ref, ob_ref, r_ref, na_ref, gt_ref, mod_ref, gg_ref, wpg_ref, wpn_ref, wo_ref,
                  o_ref):
    d = x_ref.shape[-1]
    o = of_ref[0] + ob_ref[0]
    r = r_ref[0]
    parts = []
    for h in range(GLA_HEADS):
        oh = o[:, h * GLA_DV:(h + 1) * GLA_DV]
        ms = jnp.mean(oh * oh, axis=-1, keepdims=True)
        parts.append(oh * lax.rsqrt(ms + EPS) * gg_ref[...])
    on = jnp.concatenate(parts, axis=-1) * (r * jax.nn.sigmoid(r))
    ya = jnp.dot(on.astype(BF16), wpg_ref[...], preferred_element_type=F32)
    yb = jnp.dot(na_ref[0], wpn_ref[...], preferred_element_type=F32)
    gt = gt_ref[0]
    y = jax.nn.sigmoid(gt[:, :d]) * ya + jax.nn.sigmoid(gt[:, d:]) * yb
    y = jnp.dot(y.astype(BF16), wo_ref[...], preferred_element_type=F32)
    o_ref[0] = x_ref[0] + mod_ref[0, 0][2:3] * y


def _merge(xa, o_f, o_b, r, na, gates, modseg, g_gla, wpg, wpn, wo, n_ctx_tiles):
    b, t, d = xa.shape
    tok = lambda n: pl.BlockSpec((1, ROW_TILE, n), lambda i, j: (i, j, 0))
    full = lambda a: pl.BlockSpec(a.shape, lambda i, j: (0,) * a.ndim)
    return pl.pallas_call(
        _merge_kernel,
        grid=(b, t // ROW_TILE),
        in_specs=[tok(d), tok(GLA_VW), tok(GLA_VW), tok(GLA_VW), tok(NA_W), tok(2 * d),
                  pl.BlockSpec((1, 1, SUBLANE, d), lambda i, j: (i, (j >= n_ctx_tiles).astype(I32), 0, 0)),
                  full(g_gla), full(wpg), full(wpn), full(wo)],
        out_specs=tok(d),
        out_shape=jax.ShapeDtypeStruct((b, t, d), F32),
        compiler_params=_cp(("parallel", "arbitrary"), 40),
        name="merge",
    )(xa, o_f, o_b, r, na, gates, modseg, g_gla, wpg, wpn, wo)


def _topk_rows(s, ids, k):
    rows = s.shape[0]
    rid = lax.broadcasted_iota(I32, s.shape, 0).astype(F32)
    vals, out_ids = [], []
    for _ in range(k):
        m = jnp.max(s, axis=0, keepdims=True)
        pos = jnp.min(jnp.where(s == m, rid, float(rows)), axis=0, keepdims=True)
        hit = rid == pos
        vals.append(m)
        out_ids.append(pos if ids is None else jnp.max(jnp.where(hit, ids, -1.0), axis=0, keepdims=True))
        s = jnp.where(hit, -jnp.inf, s)
    return jnp.concatenate(vals, axis=0), jnp.concatenate(out_ids, axis=0)


def _staircase_candidates(v0, i0, v1, i1):
    k = PEER_TOPK
    sub = lax.broadcasted_iota(I32, (SUBLANE, v0.shape[1]), 0)
    vals = [v0[0:1] + v1]
    ids = [i0[0:1] * float(N_KEYS) + i1]
    for a in range(1, k // 2):
        nb = k // (a + 1)
        live = sub < nb
        vals.append(jnp.where(live, v0[a:a + 1] + v1[:SUBLANE], -jnp.inf))
        ids.append(i0[a:a + 1] * float(N_KEYS) + i1[:SUBLANE])
    vals.append(v0[k // 2:] + v1[0:1])
    ids.append(i0[k // 2:] * float(N_KEYS) + i1[0:1])
    return jnp.concatenate(vals, axis=0), jnp.concatenate(ids, axis=0)


def _peer1_kernel(x_ref, mod_ref, g_ref, wq_ref, sk_ref, idx_ref, pair_ref, gate_ref):
    m = mod_ref[0, 0]
    h = _norm_mod(x_ref[0], g_ref[...], m[3:4], m[4:5]).astype(BF16)
    half = PEER_DK // 2
    chosen = []
    for hd in range(PEER_HEADS):
        tops = []
        for p in range(2):
            col = (2 * hd + p) * half
            qp = jnp.dot(h, wq_ref[:, col:col + half], preferred_element_type=F32).astype(BF16)
            st = lax.dot_general(sk_ref[hd, p], qp, _NT, preferred_element_type=F32)
            tops.append(_topk_rows(st, None, PEER_TOPK))
        (v0, i0), (v1, i1) = tops
        cand, cidx = _staircase_candidates(v0, i0, v1, i1)
        best, experts = _topk_rows(cand, cidx, PEER_TOPK)
        e = jnp.exp(best - best[0:1])
        rs = slice(hd * PEER_TOPK, (hd + 1) * PEER_TOPK)
        gate_ref[0, rs, :] = e / jnp.sum(e, axis=0, keepdims=True)
        idx_ref[0, rs, :] = experts.astype(I32)
        chosen.append(experts)
    pair_ref[0] = lax.shift_right_logical(jnp.concatenate(chosen, axis=0).T.astype(I32), 1) * SUBLANE


def _peer_select(xa, modseg, g, wq, sk, lc):
    b, t, d = xa.shape
    ne = PEER_HEADS * PEER_TOPK
    tm = SELECT_TILE
    n_ctx_tiles = lc // tm
    return pl.pallas_call(
        _peer1_kernel,
        grid=(b, t // tm),
        in_specs=[pl.BlockSpec((1, tm, d), lambda i, j: (i, j, 0)),
                  pl.BlockSpec((1, 1, SUBLANE, d), lambda i, j: (i, (j >= n_ctx_tiles).astype(I32), 0, 0)),
                  pl.BlockSpec((1, d), lambda i, j: (0, 0)),
                  pl.BlockSpec(wq.shape, lambda i, j: (0, 0)),
                  pl.BlockSpec(sk.shape, lambda i, j: (0, 0, 0, 0))],
        out_specs=[pl.BlockSpec((1, ne, tm), lambda i, j: (i, 0, j)),
                   pl.BlockSpec((1, tm, ne), lambda i, j: (i, j, 0)),
                   pl.BlockSpec((1, ne, tm), lambda i, j: (i, 0, j))],
        out_shape=[jax.ShapeDtypeStruct((b, ne, t), I32), jax.ShapeDtypeStruct((b, t, ne), I32),
                   jax.ShapeDtypeStruct((b, ne, t), F32)],
        compiler_params=_cp(("parallel", "arbitrary"), 40),
        name="peer_select",
    )(xa, modseg, g, wq, sk)


PAIR_ROWS = 2 * SUBLANE


def _gather_tiles(tbl_ref, pair_ref, t, ne):
    base = pl.multiple_of(t * ne, ne)
    tiles = []
    for j in range(ne):
        row = pl.multiple_of(pair_ref[base + j], SUBLANE)
        tiles.append(pltpu.bitcast(tbl_ref[pl.ds(row, SUBLANE), :], BF16))
    return jnp.concatenate(tiles, axis=0)


TOKENS_PER_TRIP = 16


def _for_token_groups(tb, body):
    def trip(i, carry):
        for u in range(TOKENS_PER_TRIP):
            body(TOKENS_PER_TRIP * i + u)
        return carry

    lax.fori_loop(0, tb // TOKENS_PER_TRIP, trip, 0)


def _peer2_kernel(pair_ref, idx_ref, x_ref, mod_ref, g_ref, gate_ref, tbl_ref, cc_ref, acc_ref):
    tb, nrow = acc_ref.shape
    ne = nrow // 2
    kk = ne * PAIR_ROWS
    grp = 2 * PAIR_ROWS
    d = x_ref.shape[1] * x_ref.shape[2]
    a = g_ref[0] * (1.0 + mod_ref[4])
    shift = mod_ref[3]
    srow = lax.broadcasted_iota(I32, (grp, PAIR_ROWS * PAIR_ROWS), 0)
    scol = lax.broadcasted_iota(I32, (grp, PAIR_ROWS * PAIR_ROWS), 1)
    sel = (((scol // PAIR_ROWS) == (srow % PAIR_ROWS)) & ((scol % 2) == (srow // PAIR_ROWS))).astype(BF16)

    def token(t):
        x = x_ref[t]
        ms = jnp.sum(x * x, keepdims=True) * (1.0 / d)
        h = x * lax.rsqrt(ms + EPS) * a + shift
        hu = lax.bitcast_convert_type(h.astype(BF16).astype(F32), U32)
        hp = pltpu.bitcast(hu | lax.shift_right_logical(hu, jnp.uint32(16)), BF16)
        base = pl.multiple_of(t * ne, ne)
        parts = []
        for gi in range(ne // PAIR_ROWS):
            prods = []
            for j in range(gi * PAIR_ROWS, (gi + 1) * PAIR_ROWS):
                row = pl.multiple_of(pair_ref[base + j], SUBLANE)
                prods.append(pltpu.bitcast(tbl_ref[pl.ds(row, SUBLANE), :], BF16) * hp)
            parts.append(jnp.dot(sel, jnp.concatenate(prods, axis=0), preferred_element_type=F32))
        r = jnp.concatenate(parts, axis=0)
        acc_ref[pl.ds(t, 1), :] = jnp.sum(r.T, axis=0, keepdims=True)

    _for_token_groups(tb, token)

    at = acc_ref[...].T
    a0 = jnp.concatenate([at[gi * grp:gi * grp + PAIR_ROWS] for gi in range(ne // PAIR_ROWS)], axis=0)
    a1 = jnp.concatenate([at[gi * grp + PAIR_ROWS:(gi + 1) * grp] for gi in range(ne // PAIR_ROWS)], axis=0)
    par = idx_ref[0] & 1
    act = jnp.where(par == 0, a0, a1)
    coef = jax.nn.gelu(act, approximate=True) * gate_ref[0]
    ke = lax.broadcasted_iota(I32, (ne, kk), 1)
    re = lax.broadcasted_iota(I32, (ne, kk), 0)
    expand = ((ke // PAIR_ROWS) == re).astype(BF16)
    cexp = jnp.dot(coef.T.astype(BF16), expand, preferred_element_type=F32)
    pexp = jnp.dot(par.astype(F32).T.astype(BF16), expand, preferred_element_type=F32)
    lanepar = (lax.broadcasted_iota(I32, (tb, kk), 1) % 2).astype(F32)
    cc_ref[...] = jnp.where(pexp == lanepar, cexp, 0.0)


def _peer3_kernel(pair_ref, cc_ref, x_ref, mod_ref, tbl_ref, o_ref):
    tb, kk = cc_ref.shape
    ne = kk // PAIR_ROWS
    ga = mod_ref[5]
    lane = lax.broadcasted_iota(I32, (PAIR_ROWS, kk), 1)
    sub = lax.broadcasted_iota(I32, (PAIR_ROWS, kk), 0)
    diag = ((lane % PAIR_ROWS) // 2) == sub

    def token(t):
        row = cc_ref[pl.ds(t, 1), :]
        cm = jnp.where(diag, jnp.broadcast_to(row, (PAIR_ROWS, kk)), 0.0).astype(BF16)
        tiles = _gather_tiles(tbl_ref, pair_ref, t, ne)
        o = jnp.dot(cm, tiles, preferred_element_type=F32)
        o_ref[t] = x_ref[t] + ga * o[:SUBLANE]

    _for_token_groups(tb, token)


def _peer_blocks(b, t, lc):
    nj = t // PEER_TOKENS
    n_ctx = lc // PEER_TOKENS
    tokb = pl.BlockSpec((PEER_TOKENS, SUBLANE, LANE), lambda n: (n, 0, 0))
    modb = pl.BlockSpec((SUBLANE, SUBLANE, LANE), lambda n: ((n // nj) * 2 + ((n % nj) >= n_ctx).astype(I32), 0, 0))
    return nj, tokb, modb


def _peer_act(x3, pair2, idx, gate, modv, g3, tbl, b, t, lc):
    ne = PEER_HEADS * PEER_TOPK
    kk = ne * PAIR_ROWS
    nj, tokb, modb = _peer_blocks(b, t, lc)
    expb = pl.BlockSpec((1, ne, PEER_TOKENS), lambda n: (n // nj, 0, n % nj))
    return pl.pallas_call(
        _peer2_kernel,
        grid=(b * nj,),
        in_specs=[pl.BlockSpec((PEER_TOKENS * ne,), lambda n: (n,), memory_space=pltpu.SMEM),
                  expb, tokb, modb,
                  pl.BlockSpec((1, SUBLANE, LANE), lambda n: (0, 0, 0)),
                  expb,
                  pl.BlockSpec(tbl.shape, lambda n: (0, 0), pipeline_mode=pl.Buffered(1))],
        out_specs=pl.BlockSpec((PEER_TOKENS, kk), lambda n: (n, 0)),
        out_shape=jax.ShapeDtypeStruct((b * t, kk), F32),
        scratch_shapes=[pltpu.VMEM((PEER_TOKENS, 2 * ne), F32)],
        compiler_params=_cp(("arbitrary",), 56),
        name="peer_act",
    )(pair2, idx, x3, modv, g3, gate, tbl)


def _peer_out(x3, pair2, cc, modv, tbl, b, t, lc):
    ne = PEER_HEADS * PEER_TOPK
    kk = ne * PAIR_ROWS
    nj, tokb, modb = _peer_blocks(b, t, lc)
    return pl.pallas_call(
        _peer3_kernel,
        grid=(b * nj,),
        in_specs=[pl.BlockSpec((PEER_TOKENS * ne,), lambda n: (n,), memory_space=pltpu.SMEM),
                  pl.BlockSpec((PEER_TOKENS, kk), lambda n: (n, 0)),
                  tokb, modb,
                  pl.BlockSpec(tbl.shape, lambda n: (0, 0), pipeline_mode=pl.Buffered(1))],
        out_specs=tokb,
        out_shape=jax.ShapeDtypeStruct(x3.shape, F32),
        compiler_params=_cp(("arbitrary",), 56),
        name="peer_out",
    )(pair2, cc, x3, modv, tbl)


def _pack_experts(w):
    e, d = w.shape
    assert d == VREG_ELEMS
    bits = lax.bitcast_convert_type(w.astype(BF16), jnp.uint16).astype(U32).reshape(e // 2, 2, SUBLANE, LANE)
    return (bits[:, 0] | (bits[:, 1] << 16)).reshape(e // 2 * SUBLANE, LANE)


def _peer(xa, modseg, modv, g2, w_query, sub_keys, expert_u, expert_v, lc):
    b, t, d = xa.shape
    idx, pair, gate = _peer_select(xa, modseg, g2.reshape(1, d), w_query.astype(BF16), sub_keys.astype(BF16), lc)
    pair2 = pair.reshape(b * t * idx.shape[1])
    x3 = xa.reshape(b * t, SUBLANE, LANE)
    cc = _peer_act(x3, pair2, idx, gate, modv, g2.reshape(1, SUBLANE, LANE), _pack_experts(expert_u), b, t, lc)
    x3 = _peer_out(x3, pair2, cc, modv, _pack_experts(expert_v), b, t, lc)
    return x3.reshape(b, t, d)


def _final_kernel(x_ref, g_ref, o_ref):
    x = x_ref[0]
    ms = jnp.mean(x * x, axis=-1, keepdims=True)
    o_ref[0] = x * lax.rsqrt(ms + EPS) * g_ref[...]


def _final_norm(xa, g, lc):
    b, t, d = xa.shape
    off = lc // ROW_TILE
    return pl.pallas_call(
        _final_kernel,
        grid=(b, (t - lc) // ROW_TILE),
        in_specs=[pl.BlockSpec((1, ROW_TILE, d), lambda i, j: (i, j + off, 0)),
                  pl.BlockSpec((1, d), lambda i, j: (0, 0))],
        out_specs=pl.BlockSpec((1, ROW_TILE, d), lambda i, j: (i, j, 0)),
        out_shape=jax.ShapeDtypeStruct((b, t - lc, d), F32),
        compiler_params=_cp(("parallel", "arbitrary"), 32),
        name="final_norm",
    )(xa, g)


def _rope_tables(l, lc):
    t = jnp.arange(l, dtype=I32)
    quarter = GLA_DK // 4
    freqs = ROPE_BASE ** (-jnp.arange(quarter, dtype=F32) / quarter)
    ang_r = (t // GRID_W).astype(F32)[:, None] * freqs
    ang_c = (t % GRID_W).astype(F32)[:, None] * freqs
    cos = jnp.concatenate([jnp.cos(ang_r)] * 2 + [jnp.cos(ang_c)] * 2, axis=-1)
    sin = jnp.concatenate([-jnp.sin(ang_r), jnp.sin(ang_r), -jnp.sin(ang_c), jnp.sin(ang_c)], axis=-1)
    cos = jnp.tile(cos, (1, GLA_HEADS))
    sin = jnp.tile(sin, (1, GLA_HEADS))
    qs = GLA_DK ** -0.5
    cs = jnp.concatenate([cos * qs, cos], axis=-1)
    sn = jnp.concatenate([sin * qs, sin], axis=-1)
    cs_ctx = jnp.concatenate([jnp.full((lc, GLA_KW), qs, F32), jnp.ones((lc, GLA_KW), F32)], axis=-1)
    return (jnp.concatenate([cs_ctx, cs], axis=0),
            jnp.concatenate([jnp.zeros((lc, 2 * GLA_KW), F32), sn], axis=0))


def _permute_w_in(w, d):
    sizes = (GLA_KW, GLA_KW, GLA_VW, 2 * GATE_RANK, GLA_VW, NA_W, NA_W, NA_W, 2 * d)
    offs = [0]
    for s in sizes:
        offs.append(offs[-1] + s)
    q, k, v, z, r, nq, nk, nv, gates = [w[:, offs[i]:offs[i + 1]] for i in range(len(sizes))]
    zpad = jnp.pad(z, ((0, 0), (0, LANE - 2 * GATE_RANK)))
    return jnp.concatenate([q, k, v, r, nq, nk, nv, gates, zpad], axis=1).astype(BF16)


def _gate_weights(w_a, b_a, first_row):
    wa = jnp.zeros((LANE, GLA_KW), F32).at[first_row:first_row + GATE_RANK].set(w_a)
    return wa, b_a.reshape(1, GLA_KW)


def kernel(x, c, ctx, c_ctx, w_mod, b_mod, g_norm1, w_in, w_alpha_f, b_alpha_f, w_alpha_b, b_alpha_b, g_gla, rpb,
           w_proj_gla, w_proj_na, w_out, g_norm2, w_query, sub_keys, expert_u, expert_v, g_final):
    b, l, d = x.shape
    lc = ctx.shape[1]
    depth = w_mod.shape[0]
    t = lc + l
    assert d == VREG_ELEMS and lc % ROW_TILE == 0 and l % ROW_TILE == 0 and l % GRID_W == 0
    n_ctx_tiles = lc // ROW_TILE

    rows = -(-(b + 1) // SUBLANE) * SUBLANE
    cin = jnp.zeros((rows, d), F32).at[:b].set(c).at[b].set(c_ctx)
    mod_all = _modulation(cin, w_mod, b_mod)

    cs, sn = _rope_tables(l, lc)
    xa = jnp.concatenate([ctx, x], axis=1)

    for i in range(depth):
        mod = mod_all[i].reshape(rows, N_MOD, d)
        modseg = jnp.stack([jnp.broadcast_to(mod[b], (b, N_MOD, d)), mod[:b]], axis=1)
        modseg = jnp.pad(modseg, ((0, 0), (0, 0), (0, SUBLANE - N_MOD), (0, 0)))
        modv = modseg.reshape(b * 2 * SUBLANE, SUBLANE, LANE)

        qk, v, r, nq, nk, nv, gates, z = _project(xa, modseg, g_norm1[i].reshape(1, d),
                                                  _permute_w_in(w_in[i], d), n_ctx_tiles)
        waf, baf = _gate_weights(w_alpha_f[i], b_alpha_f[i], 0)
        wab, bab = _gate_weights(w_alpha_b[i], b_alpha_b[i], GATE_RANK)
        o_f = _gla(qk, v, z, cs, sn, waf, baf, lc // GLA_BLOCK, reverse=False)
        o_b = _gla(qk, v, z, cs, sn, wab, bab, lc // GLA_BLOCK, reverse=True)
        na = _na(nq, nk, nv, _na_bias_tables(rpb[i]), lc)
        xa = _merge(xa, o_f, o_b, r, na, gates, modseg, g_gla[i].reshape(1, GLA_DV),
                    w_proj_gla[i].astype(BF16), w_proj_na[i].astype(BF16), w_out[i].astype(BF16), n_ctx_tiles)

        xa = _peer(xa, modseg, modv, g_norm2[i], w_query[i], sub_keys[i], expert_u[i], expert_v[i], lc)

    return _final_norm(xa, g_final.reshape(1, d), lc)
```

```python
import functools

import jax
import jax.numpy as jnp
from jax import lax
from jax.experimental import pallas as pl
from jax.experimental.pallas import tpu as pltpu

F32 = jnp.float32
BF16 = jnp.bfloat16
I32 = jnp.int32
U32 = jnp.uint32
HIGHEST = lax.Precision.HIGHEST

GRID_W = 64
EPS = 1e-6
N_MOD = 6
GLA_HEADS = 4
GLA_DK = 64
GLA_DV = 128
GLA_KW = GLA_HEADS * GLA_DK
GLA_VW = GLA_HEADS * GLA_DV
GATE_RANK = 16
GATE_TAU = 16.0
CHUNK = 64
ROPE_BASE = 10000.0
NA_HEADS = 8
NA_DH = 64
NA_W = NA_HEADS * NA_DH
NA_KH = 8
NA_KW = 16
PEER_HEADS = 8
N_KEYS = 128
PEER_DK = 256
PEER_TOPK = 16

LANE = 128
SUBLANE = 8
VREG_ELEMS = LANE * SUBLANE

ROW_TILE = 256
GLA_BLOCK = 256
SELECT_TILE = 256
PEER_TOKENS = 128
MASK_NEG = -1e30

_NT = (((1,), (1,)), ((), ()))


def _cp(sem, vmem_mb):
    return pltpu.CompilerParams(dimension_semantics=sem, vmem_limit_bytes=vmem_mb * 1024 * 1024)


def _norm_mod(x, g, shift, scale):
    ms = jnp.mean(x * x, axis=-1, keepdims=True)
    return (x * lax.rsqrt(ms + EPS) * g) * (1.0 + scale) + shift


def _mod_kernel(c_ref, w_ref, b_ref, o_ref):
    a = c_ref[...]
    a = a * jax.nn.sigmoid(a)
    o_ref[0] = jnp.dot(a, w_ref[0], preferred_element_type=F32, precision=HIGHEST) + b_ref[0]


def _modulation(cin, w_mod, b_mod):
    depth, d, n = w_mod.shape
    rows = cin.shape[0]
    tn = n // 4
    return pl.pallas_call(
        _mod_kernel,
        grid=(depth, n // tn),
        in_specs=[pl.BlockSpec((rows, d), lambda l, j: (0, 0)),
                  pl.BlockSpec((1, d, tn), lambda l, j: (l, 0, j)),
                  pl.BlockSpec((1, 1, tn), lambda l, j: (l, 0, j))],
        out_specs=pl.BlockSpec((1, rows, tn), lambda l, j: (l, 0, j)),
        out_shape=jax.ShapeDtypeStruct((depth, rows, n), F32),
        compiler_params=_cp(("arbitrary", "arbitrary"), 40),
        name="modulation",
    )(cin, w_mod, b_mod.reshape(depth, 1, n))


_PROJ_OUTS = (("qk", 2 * GLA_KW, F32), ("v", GLA_VW, F32), ("r", GLA_VW, F32),
              ("nq", NA_W, BF16), ("nk", NA_W, BF16), ("nv", NA_W, BF16),
              ("gates", None, F32), ("z", LANE, F32))


def _proj_kernel(x_ref, mod_ref, g_ref, w_ref, *out_refs):
    m = mod_ref[0, 0]
    h = _norm_mod(x_ref[0], g_ref[...], m[0:1], m[1:2]).astype(BF16)
    col = 0
    for ref in out_refs:
        n = ref.shape[-1]
        ref[0] = jnp.dot(h, w_ref[:, col:col + n], preferred_element_type=F32).astype(ref.dtype)
        col += n


def _project(xa, modseg, g, w_perm, n_ctx_tiles):
    b, t, d = xa.shape
    outs = [(name, (2 * d if n is None else n), dt) for name, n, dt in _PROJ_OUTS]
    assert sum(n for _, n, _ in outs) == w_perm.shape[1]
    return pl.pallas_call(
        _proj_kernel,
        grid=(b, t // ROW_TILE),
        in_specs=[pl.BlockSpec((1, ROW_TILE, d), lambda i, j: (i, j, 0)),
                  pl.BlockSpec((1, 1, SUBLANE, d), lambda i, j: (i, (j >= n_ctx_tiles).astype(I32), 0, 0)),
                  pl.BlockSpec((1, d), lambda i, j: (0, 0)),
                  pl.BlockSpec(w_perm.shape, lambda i, j: (0, 0))],
        out_specs=[pl.BlockSpec((1, ROW_TILE, n), lambda i, j: (i, j, 0)) for _, n, _ in outs],
        out_shape=[jax.ShapeDtypeStruct((b, t, n), dt) for _, n, dt in outs],
        compiler_params=_cp(("parallel", "arbitrary"), 48),
        name="in_proj",
    )(xa, modseg, g, w_perm)


def _gla_kernel(qk_ref, v_ref, z_ref, cs_ref, sn_ref, wa_ref, ba_ref, o_ref, st_ref, *, reverse):
    @pl.when(pl.program_id(1) == 0)
    def _init():
        st_ref[...] = jnp.zeros_like(st_ref)

    kw2 = 2 * GLA_KW
    blk = qk_ref.shape[1]
    qk = qk_ref[0]
    lane = lax.broadcasted_iota(I32, (1, kw2), 1)
    first = (lane % 32) < 16
    partner = jnp.where(first, pltpu.roll(qk, kw2 - 16, 1), pltpu.roll(qk, 16, 1))
    qk = qk * cs_ref[...] + partner * sn_ref[...]
    zz = jnp.dot(z_ref[0], wa_ref[...], preferred_element_type=F32, precision=HIGHEST) + ba_ref[...]
    la = (jnp.minimum(zz, 0.0) - jnp.log1p(jnp.exp(-jnp.abs(zz)))) * (1.0 / GATE_TAU)

    ri = lax.broadcasted_iota(I32, (CHUNK, CHUNK), 0)
    ci = lax.broadcasted_iota(I32, (CHUNK, CHUNK), 1)
    keep = (ci >= ri) if reverse else (ci <= ri)
    tri = keep.astype(F32)
    khead = lax.broadcasted_iota(I32, (1, GLA_KW), 1) // GLA_DK
    vrow = lax.broadcasted_iota(I32, (GLA_VW, GLA_KW), 0) // GLA_DV
    kcol = lax.broadcasted_iota(I32, (GLA_VW, GLA_KW), 1) // GLA_DK
    same_head = vrow == kcol

    order = range(blk // CHUNK)
    if reverse:
        order = reversed(order)
    for c in order:
        lo = c * CHUNK
        q = qk[lo:lo + CHUNK, :GLA_KW]
        k = qk[lo:lo + CHUNK, GLA_KW:]
        v = v_ref[0, lo:lo + CHUNK, :].astype(BF16)
        bcum = jnp.dot(tri, la[lo:lo + CHUNK], preferred_element_type=F32, precision=HIGHEST)
        bend = bcum[0:1] if reverse else bcum[CHUNK - 1:CHUNK]
        qd = q * jnp.exp(bcum)
        kd = (k * jnp.exp(-bcum)).astype(BF16)
        kd2 = (k * jnp.exp(bend - bcum)).astype(BF16)
        st = st_ref[...]
        o_inter = lax.dot_general(qd.astype(BF16), st.astype(BF16), _NT, preferred_element_type=F32)
        for h in range(GLA_HEADS):
            qm = jnp.where(khead == h, qd, 0.0).astype(BF16)
            s = lax.dot_general(qm, kd, _NT, preferred_element_type=F32)
            s = jnp.where(keep, s, 0.0).astype(BF16)
            vs = slice(h * GLA_DV, (h + 1) * GLA_DV)
            o_ref[0, lo:lo + CHUNK, vs] = (jnp.dot(s, v[:, vs], preferred_element_type=F32)
                                           + o_inter[:, vs])
        upd = jnp.dot(v.T, kd2, preferred_element_type=F32)
        st_ref[...] = st * jnp.exp(bend) + jnp.where(same_head, upd, 0.0)


def _gla(qk, v, z, cs, sn, wa, ba, n_ctx_blocks, reverse):
    b, t, _ = qk.shape
    nblk = t // GLA_BLOCK

    if reverse:
        def blk(j):
            return jnp.where(j < n_ctx_blocks, n_ctx_blocks - 1 - j, nblk - 1 - (j - n_ctx_blocks))
    else:
        def blk(j):
            return j

    tok = lambda n: pl.BlockSpec((1, GLA_BLOCK, n), lambda i, j: (i, blk(j), 0))
    tab = lambda n: pl.BlockSpec((GLA_BLOCK, n), lambda i, j: (blk(j), 0))
    return pl.pallas_call(
        functools.partial(_gla_kernel, reverse=reverse),
        grid=(b, nblk),
        in_specs=[tok(2 * GLA_KW), tok(GLA_VW), tok(LANE), tab(2 * GLA_KW), tab(2 * GLA_KW),
                  pl.BlockSpec((LANE, GLA_KW), lambda i, j: (0, 0)),
                  pl.BlockSpec((1, GLA_KW), lambda i, j: (0, 0))],
        out_specs=tok(GLA_VW),
        out_shape=jax.ShapeDtypeStruct((b, t, GLA_VW), F32),
        scratch_shapes=[pltpu.VMEM((GLA_VW, GLA_KW), F32)],
        compiler_params=_cp(("parallel", "arbitrary"), 40),
        name="gla_bwd" if reverse else "gla_fwd",
    )(qk, v, z, cs, sn, wa, ba)


def _na_kernel(*refs):
    q_ref = refs[0]
    k_refs = refs[1:1 + NA_KH]
    v_refs = refs[1 + NA_KH:1 + 2 * NA_KH]
    kc_ref, vc_ref, tb_ref, o_ref, s_ref, p_ref = refs[1 + 2 * NA_KH:]
    scale = NA_DH ** -0.5
    nw = NA_KH * GRID_W
    half = lax.broadcasted_iota(I32, (1, LANE), 1) // NA_DH
    for p in range(NA_HEADS // 2):
        ls = slice(p * LANE, (p + 1) * LANE)
        qp = q_ref[0, :, ls]
        kw = jnp.concatenate([r[0, :, ls] for r in k_refs], axis=0)
        kc = kc_ref[0, :, ls]
        for s in range(2):
            qm = jnp.where(half == s, qp, jnp.zeros_like(qp))
            s_ref[2 * p + s, :, :nw] = (lax.dot_general(qm, kw, _NT, preferred_element_type=F32) * scale
                                        + tb_ref[0, 2 * p + s])
            s_ref[2 * p + s, :, nw:] = lax.dot_general(qm, kc, _NT, preferred_element_type=F32) * scale
    for h in range(NA_HEADS):
        sc = s_ref[h]
        e = jnp.exp(sc - jnp.max(sc, axis=-1, keepdims=True))
        p_ref[h] = (e * (1.0 / jnp.sum(e, axis=-1, keepdims=True))).astype(BF16)
    for p in range(NA_HEADS // 2):
        ls = slice(p * LANE, (p + 1) * LANE)
        vw = jnp.concatenate([r[0, :, ls] for r in v_refs], axis=0)
        vc = vc_ref[0, :, ls]
        outs = [jnp.dot(p_ref[2 * p + s, :, :nw], vw, preferred_element_type=F32)
                + jnp.dot(p_ref[2 * p + s, :, nw:], vc, preferred_element_type=F32) for s in range(2)]
        o_ref[0, :, ls] = jnp.where(half == 0, outs[0], outs[1]).astype(o_ref.dtype)


def _na_bias_tables(rpb):
    w = jnp.arange(GRID_W, dtype=I32)
    c0 = jnp.clip(w - NA_KW // 2, 0, GRID_W - NA_KW)
    cp = w[None, :]
    valid = (cp >= c0[:, None]) & (cp < c0[:, None] + NA_KW)
    off = jnp.clip(cp - w[:, None] + (NA_KW - 1), 0, 2 * NA_KW - 2)
    t = jnp.where(valid[None, None], rpb[:, :, off], MASK_NEG)
    tabs = [jnp.concatenate([t[:, s + a] for a in range(NA_KH)], axis=-1) for s in range(NA_KH)]
    tabs.append(jnp.full_like(tabs[0], MASK_NEG))
    return jnp.stack(tabs, axis=0).astype(F32)


def _na(nq, nk, nv, tables, lc):
    b, t, _ = nq.shape
    n_ctx = lc // GRID_W
    rows = (t - lc) // GRID_W
    assert rows >= NA_KH

    def row0(j):
        r = jnp.maximum(j - n_ctx, 0)
        return r, jnp.clip(r - NA_KH // 2, 0, rows - NA_KH)

    def win(a):
        return pl.BlockSpec((1, GRID_W, NA_W), lambda i, j: (i, n_ctx + row0(j)[1] + a, 0))

    def tab_idx(i, j):
        r, r0 = row0(j)
        return (jnp.where(j < n_ctx, NA_KH, r0 - r + NA_KH - 1), 0, 0, 0)

    return pl.pallas_call(
        _na_kernel,
        grid=(b, t // GRID_W),
        in_specs=([pl.BlockSpec((1, GRID_W, NA_W), lambda i, j: (i, j, 0))]
                  + [win(a) for a in range(NA_KH)] + [win(a) for a in range(NA_KH)]
                  + [pl.BlockSpec((1, lc, NA_W), lambda i, j: (i, 0, 0))] * 2
                  + [pl.BlockSpec((1, NA_HEADS, GRID_W, NA_KH * GRID_W), tab_idx)]),
        out_specs=pl.BlockSpec((1, GRID_W, NA_W), lambda i, j: (i, j, 0)),
        out_shape=jax.ShapeDtypeStruct((b, t, NA_W), BF16),
        scratch_shapes=[pltpu.VMEM((NA_HEADS, GRID_W, NA_KH * GRID_W + lc), F32),
                        pltpu.VMEM((NA_HEADS, GRID_W, NA_KH * GRID_W + lc), BF16)],
        compiler_params=_cp(("parallel", "arbitrary"), 40),
        name="natten",
    )(nq, *([nk] * NA_KH), *([nv] * NA_KH), nk, nv, tables)


def _merge_kernel(x_ref, of_ref, ob_ref, r_ref, na_ref, gt_ref, mod_ref, gg_ref, wpg_ref, wpn_ref, wo_ref,
                  o_ref):
    d = x_ref.shape[-1]
    o = of_ref[0] + ob_ref[0]
    r = r_ref[0]
    parts = []
    for h in range(GLA_HEADS):
        oh = o[:, h * GLA_DV:(h + 1) * GLA_DV]
        ms = jnp.mean(oh * oh, axis=-1, keepdims=True)
        parts.append(oh * lax.rsqrt(ms + EPS) * gg_ref[...])
    on = jnp.concatenate(parts, axis=-1) * (r * jax.nn.sigmoid(r))
    ya = jnp.dot(on.astype(BF16), wpg_ref[...], preferred_element_type=F32)
    yb = jnp.dot(na_ref[0], wpn_ref[...], preferred_element_type=F32)
    gt = gt_ref[0]
    y = jax.nn.sigmoid(gt[:, :d]) * ya + jax.nn.sigmoid(gt[:, d:]) * yb
    y = jnp.dot(y.astype(BF16), wo_ref[...], preferred_element_type=F32)
    o_ref[0] = x_ref[0] + mod_ref[0, 0][2:3] * y


def _merge(xa, o_f, o_b, r, na, gates, modseg, g_gla, wpg, wpn, wo, n_ctx_tiles):
    b, t, d = xa.shape
    tok = lambda n: pl.BlockSpec((1, ROW_TILE, n), lambda i, j: (i, j, 0))
    full = lambda a: pl.BlockSpec(a.shape, lambda i, j: (0,) * a.ndim)
    return pl.pallas_call(
        _merge_kernel,
        grid=(b, t // ROW_TILE),
        in_specs=[tok(d), tok(GLA_VW), tok(GLA_VW), tok(GLA_VW), tok(NA_W), tok(2 * d),
                  pl.BlockSpec((1, 1, SUBLANE, d), lambda i, j: (i, (j >= n_ctx_tiles).astype(I32), 0, 0)),
                  full(g_gla), full(wpg), full(wpn), full(wo)],
        out_specs=tok(d),
        out_shape=jax.ShapeDtypeStruct((b, t, d), F32),
        compiler_params=_cp(("parallel", "arbitrary"), 40),
        name="merge",
    )(xa, o_f, o_b, r, na, gates, modseg, g_gla, wpg, wpn, wo)


def _topk_rows(s, ids, k):
    rows = s.shape[0]
    rid = lax.broadcasted_iota(I32, s.shape, 0).astype(F32)
    vals, out_ids = [], []
    for _ in range(k):
        m = jnp.max(s, axis=0, keepdims=True)
        pos = jnp.min(jnp.where(s == m, rid, float(rows)), axis=0, keepdims=True)
        hit = rid == pos
        vals.append(m)
        out_ids.append(pos if ids is None else jnp.max(jnp.where(hit, ids, -1.0), axis=0, keepdims=True))
        s = jnp.where(hit, -jnp.inf, s)
    return jnp.concatenate(vals, axis=0), jnp.concatenate(out_ids, axis=0)


def _staircase_candidates(v0, i0, v1, i1):
    k = PEER_TOPK
    sub = lax.broadcasted_iota(I32, (SUBLANE, v0.shape[1]), 0)
    vals = [v0[0:1] + v1]
    ids = [i0[0:1] * float(N_KEYS) + i1]
    for a in range(1, k // 2):
        nb = k // (a + 1)
        live = sub < nb
        vals.append(jnp.where(live, v0[a:a + 1] + v1[:SUBLANE], -jnp.inf))
        ids.append(i0[a:a + 1] * float(N_KEYS) + i1[:SUBLANE])
    vals.append(v0[k // 2:] + v1[0:1])
    ids.append(i0[k // 2:] * float(N_KEYS) + i1[0:1])
    return jnp.concatenate(vals, axis=0), jnp.concatenate(ids, axis=0)


def _peer1_kernel(x_ref, mod_ref, g_ref, wq_ref, sk_ref, idx_ref, pair_ref, gate_ref):
    m = mod_ref[0, 0]
    h = _norm_mod(x_ref[0], g_ref[...], m[3:4], m[4:5]).astype(BF16)
    half = PEER_DK // 2
    chosen = []
    for hd in range(PEER_HEADS):
        tops = []
        for p in range(2):
            col = (2 * hd + p) * half
            qp = jnp.dot(h, wq_ref[:, col:col + half], preferred_element_type=F32).astype(BF16)
            st = lax.dot_general(sk_ref[hd, p], qp, _NT, preferred_element_type=F32)
            tops.append(_topk_rows(st, None, PEER_TOPK))
        (v0, i0), (v1, i1) = tops
        cand, cidx = _staircase_candidates(v0, i0, v1, i1)
        best, experts = _topk_rows(cand, cidx, PEER_TOPK)
        e = jnp.exp(best - best[0:1])
        rs = slice(hd * PEER_TOPK, (hd + 1) * PEER_TOPK)
        gate_ref[0, rs, :] = e / jnp.sum(e, axis=0, keepdims=True)
        idx_ref[0, rs, :] = experts.astype(I32)
        chosen.append(experts)
    pair_ref[0] = lax.shift_right_logical(jnp.concatenate(chosen, axis=0).T.astype(I32), 1) * SUBLANE


def _peer_select(xa, modseg, g, wq, sk, lc):
    b, t, d = xa.shape
    ne = PEER_HEADS * PEER_TOPK
    tm = SELECT_TILE
    n_ctx_tiles = lc // tm
    return pl.pallas_call(
        _peer1_kernel,
        grid=(b, t // tm),
        in_specs=[pl.BlockSpec((1, tm, d), lambda i, j: (i, j, 0)),
                  pl.BlockSpec((1, 1, SUBLANE, d), lambda i, j: (i, (j >= n_ctx_tiles).astype(I32), 0, 0)),
                  pl.BlockSpec((1, d), lambda i, j: (0, 0)),
                  pl.BlockSpec(wq.shape, lambda i, j: (0, 0)),
                  pl.BlockSpec(sk.shape, lambda i, j: (0, 0, 0, 0))],
        out_specs=[pl.BlockSpec((1, ne, tm), lambda i, j: (i, 0, j)),
                   pl.BlockSpec((1, tm, ne), lambda i, j: (i, j, 0)),
                   pl.BlockSpec((1, ne, tm), lambda i, j: (i, 0, j))],
        out_shape=[jax.ShapeDtypeStruct((b, ne, t), I32), jax.ShapeDtypeStruct((b, t, ne), I32),
                   jax.ShapeDtypeStruct((b, ne, t), F32)],
        compiler_params=_cp(("parallel", "arbitrary"), 40),
        name="peer_select",
    )(xa, modseg, g, wq, sk)


PAIR_ROWS = 2 * SUBLANE


def _gather_tiles(tbl_ref, pair_ref, t, ne):
    base = pl.multiple_of(t * ne, ne)
    tiles = []
    for j in range(ne):
        row = pl.multiple_of(pair_ref[base + j], SUBLANE)
        tiles.append(pltpu.bitcast(tbl_ref[pl.ds(row, SUBLANE), :], BF16))
    return jnp.concatenate(tiles, axis=0)


TOKENS_PER_TRIP = 16


def _for_token_groups(tb, body):
    def trip(i, carry):
        for u in range(TOKENS_PER_TRIP):
            body(TOKENS_PER_TRIP * i + u)
        return carry

    lax.fori_loop(0, tb // TOKENS_PER_TRIP, trip, 0)


def _peer2_kernel(pair_ref, idx_ref, x_ref, mod_ref, g_ref, gate_ref, tbl_ref, cc_ref, acc_ref):
    tb, nrow = acc_ref.shape
    ne = nrow // 2
    kk = ne * PAIR_ROWS
    grp = 2 * PAIR_ROWS
    d = x_ref.shape[1] * x_ref.shape[2]
    a = g_ref[0] * (1.0 + mod_ref[4])
    shift = mod_ref[3]
    srow = lax.broadcasted_iota(I32, (grp, PAIR_ROWS * PAIR_ROWS), 0)
    scol = lax.broadcasted_iota(I32, (grp, PAIR_ROWS * PAIR_ROWS), 1)
    sel = (((scol // PAIR_ROWS) == (srow % PAIR_ROWS)) & ((scol % 2) == (srow // PAIR_ROWS))).astype(BF16)

    def token(t):
        x = x_ref[t]
        ms = jnp.sum(x * x, keepdims=True) * (1.0 / d)
        h = x * lax.rsqrt(ms + EPS) * a + shift
        hu = lax.bitcast_convert_type(h.astype(BF16).astype(F32), U32)
        hp = pltpu.bitcast(hu | lax.shift_right_logical(hu, jnp.uint32(16)), BF16)
        base = pl.multiple_of(t * ne, ne)
        parts = []
        for gi in range(ne // PAIR_ROWS):
            prods = []
            for j in range(gi * PAIR_ROWS, (gi + 1) * PAIR_ROWS):
                row = pl.multiple_of(pair_ref[base + j], SUBLANE)
                prods.append(pltpu.bitcast(tbl_ref[pl.ds(row, SUBLANE), :], BF16) * hp)
            parts.append(jnp.dot(sel, jnp.concatenate(prods, axis=0), preferred_element_type=F32))
        r = jnp.concatenate(parts, axis=0)
        acc_ref[pl.ds(t, 1), :] = jnp.sum(r.T, axis=0, keepdims=True)

    _for_token_groups(tb, token)

    at = acc_ref[...].T
    a0 = jnp.concatenate([at[gi * grp:gi * grp + PAIR_ROWS] for gi in range(ne // PAIR_ROWS)], axis=0)
    a1 = jnp.concatenate([at[gi * grp + PAIR_ROWS:(gi + 1) * grp] for gi in range(ne // PAIR_ROWS)], axis=0)
    par = idx_ref[0] & 1
    act = jnp.where(par == 0, a0, a1)
    coef = jax.nn.gelu(act, approximate=True) * gate_ref[0]
    ke = lax.broadcasted_iota(I32, (ne, kk), 1)
    re = lax.broadcasted_iota(I32, (ne, kk), 0)
    expand = ((ke // PAIR_ROWS) == re).astype(BF16)
    cexp = jnp.dot(coef.T.astype(BF16), expand, preferred_element_type=F32)
    pexp = jnp.dot(par.astype(F32).T.astype(BF16), expand, preferred_element_type=F32)
    lanepar = (lax.broadcasted_iota(I32, (tb, kk), 1) % 2).astype(F32)
    cc_ref[...] = jnp.where(pexp == lanepar, cexp, 0.0)


def _peer3_kernel(pair_ref, cc_ref, x_ref, mod_ref, tbl_ref, o_ref):
    tb, kk = cc_ref.shape
    ne = kk // PAIR_ROWS
    ga = mod_ref[5]
    lane = lax.broadcasted_iota(I32, (PAIR_ROWS, kk), 1)
    sub = lax.broadcasted_iota(I32, (PAIR_ROWS, kk), 0)
    diag = ((lane % PAIR_ROWS) // 2) == sub

    def token(t):
        row = cc_ref[pl.ds(t, 1), :]
        cm = jnp.where(diag, jnp.broadcast_to(row, (PAIR_ROWS, kk)), 0.0).astype(BF16)
        tiles = _gather_tiles(tbl_ref, pair_ref, t, ne)
        o = jnp.dot(cm, tiles, preferred_element_type=F32)
        o_ref[t] = x_ref[t] + ga * o[:SUBLANE]

    _for_token_groups(tb, token)


def _peer_blocks(b, t, lc):
    nj = t // PEER_TOKENS
    n_ctx = lc // PEER_TOKENS
    tokb = pl.BlockSpec((PEER_TOKENS, SUBLANE, LANE), lambda n: (n, 0, 0))
    modb = pl.BlockSpec((SUBLANE, SUBLANE, LANE), lambda n: ((n // nj) * 2 + ((n % nj) >= n_ctx).astype(I32), 0, 0))
    return nj, tokb, modb


def _peer_act(x3, pair2, idx, gate, modv, g3, tbl, b, t, lc):
    ne = PEER_HEADS * PEER_TOPK
    kk = ne * PAIR_ROWS
    nj, tokb, modb = _peer_blocks(b, t, lc)
    expb = pl.BlockSpec((1, ne, PEER_TOKENS), lambda n: (n // nj, 0, n % nj))
    return pl.pallas_call(
        _peer2_kernel,
        grid=(b * nj,),
        in_specs=[pl.BlockSpec((PEER_TOKENS * ne,), lambda n: (n,), memory_space=pltpu.SMEM),
                  expb, tokb, modb,
                  pl.BlockSpec((1, SUBLANE, LANE), lambda n: (0, 0, 0)),
                  expb,
                  pl.BlockSpec(tbl.shape, lambda n: (0, 0), pipeline_mode=pl.Buffered(1))],
        out_specs=pl.BlockSpec((PEER_TOKENS, kk), lambda n: (n, 0)),
        out_shape=jax.ShapeDtypeStruct((b * t, kk), F32),
        scratch_shapes=[pltpu.VMEM((PEER_TOKENS, 2 * ne), F32)],
        compiler_params=_cp(("arbitrary",), 56),
        name="peer_act",
    )(pair2, idx, x3, modv, g3, gate, tbl)


def _peer_out(x3, pair2, cc, modv, tbl, b, t, lc):
    ne = PEER_HEADS * PEER_TOPK
    kk = ne * PAIR_ROWS
    nj, tokb, modb = _peer_blocks(b, t, lc)
    return pl.pallas_call(
        _peer3_kernel,
        grid=(b * nj,),
        in_specs=[pl.BlockSpec((PEER_TOKENS * ne,), lambda n: (n,), memory_space=pltpu.SMEM),
                  pl.BlockSpec((PEER_TOKENS, kk), lambda n: (n, 0)),
                  tokb, modb,
                  pl.BlockSpec(tbl.shape, lambda n: (0, 0), pipeline_mode=pl.Buffered(1))],
        out_specs=tokb,
        out_shape=jax.ShapeDtypeStruct(x3.shape, F32),
        compiler_params=_cp(("arbitrary",), 56),
        name="peer_out",
    )(pair2, cc, x3, modv, tbl)


def _pack_experts(w):
    e, d = w.shape
    assert d == VREG_ELEMS
    bits = lax.bitcast_convert_type(w.astype(BF16), jnp.uint16).astype(U32).reshape(e // 2, 2, SUBLANE, LANE)
    return (bits[:, 0] | (bits[:, 1] << 16)).reshape(e // 2 * SUBLANE, LANE)


def _peer(xa, modseg, modv, g2, w_query, sub_keys, expert_u, expert_v, lc):
    b, t, d = xa.shape
    idx, pair, gate = _peer_select(xa, modseg, g2.reshape(1, d), w_query.astype(BF16), sub_keys.astype(BF16), lc)
    pair2 = pair.reshape(b * t * idx.shape[1])
    x3 = xa.reshape(b * t, SUBLANE, LANE)
    cc = _peer_act(x3, pair2, idx, gate, modv, g2.reshape(1, SUBLANE, LANE), _pack_experts(expert_u), b, t, lc)
    x3 = _peer_out(x3, pair2, cc, modv, _pack_experts(expert_v), b, t, lc)
    return x3.reshape(b, t, d)


def _final_kernel(x_ref, g_ref, o_ref):
    x = x_ref[0]
    ms = jnp.mean(x * x, axis=-1, keepdims=True)
    o_ref[0] = x * lax.rsqrt(ms + EPS) * g_ref[...]


def _final_norm(xa, g, lc):
    b, t, d = xa.shape
    off = lc // ROW_TILE
    return pl.pallas_call(
        _final_kernel,
        grid=(b, (t - lc) // ROW_TILE),
        in_specs=[pl.BlockSpec((1, ROW_TILE, d), lambda i, j: (i, j + off, 0)),
                  pl.BlockSpec((1, d), lambda i, j: (0, 0))],
        out_specs=pl.BlockSpec((1, ROW_TILE, d), lambda i, j: (i, j, 0)),
        out_shape=jax.ShapeDtypeStruct((b, t - lc, d), F32),
        compiler_params=_cp(("parallel", "arbitrary"), 32),
        name="final_norm",
    )(xa, g)


def _rope_tables(l, lc):
    t = jnp.arange(l, dtype=I32)
    quarter = GLA_DK // 4
    freqs = ROPE_BASE ** (-jnp.arange(quarter, dtype=F32) / quarter)
    ang_r = (t // GRID_W).astype(F32)[:, None] * freqs
    ang_c = (t % GRID_W).astype(F32)[:, None] * freqs
    cos = jnp.concatenate([jnp.cos(ang_r)] * 2 + [jnp.cos(ang_c)] * 2, axis=-1)
    sin = jnp.concatenate([-jnp.sin(ang_r), jnp.sin(ang_r), -jnp.sin(ang_c), jnp.sin(ang_c)], axis=-1)
    cos = jnp.tile(cos, (1, GLA_HEADS))
    sin = jnp.tile(sin, (1, GLA_HEADS))
    qs = GLA_DK ** -0.5
    cs = jnp.concatenate([cos * qs, cos], axis=-1)
    sn = jnp.concatenate([sin * qs, sin], axis=-1)
    cs_ctx = jnp.concatenate([jnp.full((lc, GLA_KW), qs, F32), jnp.ones((lc, GLA_KW), F32)], axis=-1)
    return (jnp.concatenate([cs_ctx, cs], axis=0),
            jnp.concatenate([jnp.zeros((lc, 2 * GLA_KW), F32), sn], axis=0))


def _permute_w_in(w, d):
    sizes = (GLA_KW, GLA_KW, GLA_VW, 2 * GATE_RANK, GLA_VW, NA_W, NA_W, NA_W, 2 * d)
    offs = [0]
    for s in sizes:
        offs.append(offs[-1] + s)
    q, k, v, z, r, nq, nk, nv, gates = [w[:, offs[i]:offs[i + 1]] for i in range(len(sizes))]
    zpad = jnp.pad(z, ((0, 0), (0, LANE - 2 * GATE_RANK)))
    return jnp.concatenate([q, k, v, r, nq, nk, nv, gates, zpad], axis=1).astype(BF16)


def _gate_weights(w_a, b_a, first_row):
    wa = jnp.zeros((LANE, GLA_KW), F32).at[first_row:first_row + GATE_RANK].set(w_a)
    return wa, b_a.reshape(1, GLA_KW)


def kernel(x, c, ctx, c_ctx, w_mod, b_mod, g_norm1, w_in, w_alpha_f, b_alpha_f, w_alpha_b, b_alpha_b, g_gla, rpb,
           w_proj_gla, w_proj_na, w_out, g_norm2, w_query, sub_keys, expert_u, expert_v, g_final):
    b, l, d = x.shape
    lc = ctx.shape[1]
    depth = w_mod.shape[0]
    t = lc + l
    assert d == VREG_ELEMS and lc % ROW_TILE == 0 and l % ROW_TILE == 0 and l % GRID_W == 0
    n_ctx_tiles = lc // ROW_TILE

    rows = -(-(b + 1) // SUBLANE) * SUBLANE
    cin = jnp.zeros((rows, d), F32).at[:b].set(c).at[b].set(c_ctx)
    mod_all = _modulation(cin, w_mod, b_mod)

    cs, sn = _rope_tables(l, lc)
    xa = jnp.concatenate([ctx, x], axis=1)

    for i in range(depth):
        mod = mod_all[i].reshape(rows, N_MOD, d)
        modseg = jnp.stack([jnp.broadcast_to(mod[b], (b, N_MOD, d)), mod[:b]], axis=1)
        modseg = jnp.pad(modseg, ((0, 0), (0, 0), (0, SUBLANE - N_MOD), (0, 0)))
        modv = modseg.reshape(b * 2 * SUBLANE, SUBLANE, LANE)

        qk, v, r, nq, nk, nv, gates, z = _project(xa, modseg, g_norm1[i].reshape(1, d),
                                                  _permute_w_in(w_in[i], d), n_ctx_tiles)
        waf, baf = _gate_weights(w_alpha_f[i], b_alpha_f[i], 0)
        wab, bab = _gate_weights(w_alpha_b[i], b_alpha_b[i], GATE_RANK)
        o_f = _gla(qk, v, z, cs, sn, waf, baf, lc // GLA_BLOCK, reverse=False)
        o_b = _gla(qk, v, z, cs, sn, wab, bab, lc // GLA_BLOCK, reverse=True)
        na = _na(nq, nk, nv, _na_bias_tables(rpb[i]), lc)
        xa = _merge(xa, o_f, o_b, r, na, gates, modseg, g_gla[i].reshape(1, GLA_DV),
                    w_proj_gla[i].astype(BF16), w_proj_na[i].astype(BF16), w_out[i].astype(BF16), n_ctx_tiles)

        xa = _peer(xa, modseg, modv, g_norm2[i], w_query[i], sub_keys[i], expert_u[i], expert_v[i], lc)

    return _final_norm(xa, g_final.reshape(1, d), lc)
```

```python
import functools

import jax
import jax.numpy as jnp
from jax import lax
from jax.experimental import pallas as pl
from jax.experimental.pallas import tpu as pltpu

F32 = jnp.float32
BF16 = jnp.bfloat16
I32 = jnp.int32
U32 = jnp.uint32
HIGHEST = lax.Precision.HIGHEST

GRID_W = 64
EPS = 1e-6
N_MOD = 6
GLA_HEADS = 4
GLA_DK = 64
GLA_DV = 128
GLA_KW = GLA_HEADS * GLA_DK
GLA_VW = GLA_HEADS * GLA_DV
GATE_RANK = 16
GATE_TAU = 16.0
CHUNK = 64
ROPE_BASE = 10000.0
NA_HEADS = 8
NA_DH = 64
NA_W = NA_HEADS * NA_DH
NA_KH = 8
NA_KW = 16
PEER_HEADS = 8
N_KEYS = 128
PEER_DK = 256
PEER_TOPK = 16

LANE = 128
SUBLANE = 8
VREG_ELEMS = LANE * SUBLANE

ROW_TILE = 256
GLA_BLOCK = 256
SELECT_TILE = 256
PEER_TOKENS = 128
MASK_NEG = -1e30

_NT = (((1,), (1,)), ((), ()))


def _cp(sem, vmem_mb):
    return pltpu.CompilerParams(dimension_semantics=sem, vmem_limit_bytes=vmem_mb * 1024 * 1024)


def _norm_mod(x, g, shift, scale):
    ms = jnp.mean(x * x, axis=-1, keepdims=True)
    return (x * lax.rsqrt(ms + EPS) * g) * (1.0 + scale) + shift


def _mod_kernel(c_ref, w_ref, b_ref, o_ref):
    a = c_ref[...]
    a = a * jax.nn.sigmoid(a)
    o_ref[0] = jnp.dot(a, w_ref[0], preferred_element_type=F32, precision=HIGHEST) + b_ref[0]


def _modulation(cin, w_mod, b_mod):
    depth, d, n = w_mod.shape
    rows = cin.shape[0]
    tn = n // 4
    return pl.pallas_call(
        _mod_kernel,
        grid=(depth, n // tn),
        in_specs=[pl.BlockSpec((rows, d), lambda l, j: (0, 0)),
                  pl.BlockSpec((1, d, tn), lambda l, j: (l, 0, j)),
                  pl.BlockSpec((1, 1, tn), lambda l, j: (l, 0, j))],
        out_specs=pl.BlockSpec((1, rows, tn), lambda l, j: (l, 0, j)),
        out_shape=jax.ShapeDtypeStruct((depth, rows, n), F32),
        compiler_params=_cp(("arbitrary", "arbitrary"), 40),
        name="modulation",
    )(cin, w_mod, b_mod.reshape(depth, 1, n))


_PROJ_OUTS = (("qk", 2 * GLA_KW, F32), ("v", GLA_VW, F32), ("r", GLA_VW, F32),
              ("nq", NA_W, BF16), ("nk", NA_W, BF16), ("nv", NA_W, BF16),
              ("gates", None, F32), ("z", LANE, F32))


def _proj_kernel(x_ref, mod_ref, g_ref, w_ref, *out_refs):
    m = mod_ref[0, 0]
    h = _norm_mod(x_ref[0], g_ref[...], m[0:1], m[1:2]).astype(BF16)
    col = 0
    for ref in out_refs:
        n = ref.shape[-1]
        ref[0] = jnp.dot(h, w_ref[:, col:col + n], preferred_element_type=F32).astype(ref.dtype)
        col += n


def _project(xa, modseg, g, w_perm, n_ctx_tiles):
    b, t, d = xa.shape
    outs = [(name, (2 * d if n is None else n), dt) for name, n, dt in _PROJ_OUTS]
    assert sum(n for _, n, _ in outs) == w_perm.shape[1]
    return pl.pallas_call(
        _proj_kernel,
        grid=(b, t // ROW_TILE),
        in_specs=[pl.BlockSpec((1, ROW_TILE, d), lambda i, j: (i, j, 0)),
                  pl.BlockSpec((1, 1, SUBLANE, d), lambda i, j: (i, (j >= n_ctx_tiles).astype(I32), 0, 0)),
                  pl.BlockSpec((1, d), lambda i, j: (0, 0)),
                  pl.BlockSpec(w_perm.shape, lambda i, j: (0, 0))],
        out_specs=[pl.BlockSpec((1, ROW_TILE, n), lambda i, j: (i, j, 0)) for _, n, _ in outs],
        out_shape=[jax.ShapeDtypeStruct((b, t, n), dt) for _, n, dt in outs],
        compiler_params=_cp(("parallel", "arbitrary"), 48),
        name="in_proj",
    )(xa, modseg, g, w_perm)


def _gla_kernel(qk_ref, v_ref, z_ref, cs_ref, sn_ref, wa_ref, ba_ref, o_ref, st_ref, *, reverse):
    @pl.when(pl.program_id(1) == 0)
    def _init():
        st_ref[...] = jnp.zeros_like(st_ref)

    kw2 = 2 * GLA_KW
    blk = qk_ref.shape[1]
    qk = qk_ref[0]
    lane = lax.broadcasted_iota(I32, (1, kw2), 1)
    first = (lane % 32) < 16
    partner = jnp.where(first, pltpu.roll(qk, kw2 - 16, 1), pltpu.roll(qk, 16, 1))
    qk = qk * cs_ref[...] + partner * sn_ref[...]
    zz = jnp.dot(z_ref[0], wa_ref[...], preferred_element_type=F32, precision=HIGHEST) + ba_ref[...]
    la = (jnp.minimum(zz, 0.0) - jnp.log1p(jnp.exp(-jnp.abs(zz)))) * (1.0 / GATE_TAU)

    ri = lax.broadcasted_iota(I32, (CHUNK, CHUNK), 0)
    ci = lax.broadcasted_iota(I32, (CHUNK, CHUNK), 1)
    keep = (ci >= ri) if reverse else (ci <= ri)
    tri = keep.astype(F32)
    khead = lax.broadcasted_iota(I32, (1, GLA_KW), 1) // GLA_DK
    vrow = lax.broadcasted_iota(I32, (GLA_VW, GLA_KW), 0) // GLA_DV
    kcol = lax.broadcasted_iota(I32, (GLA_VW, GLA_KW), 1) // GLA_DK
    same_head = vrow == kcol

    order = range(blk // CHUNK)
    if reverse:
        order = reversed(order)
    for c in order:
        lo = c * CHUNK
        q = qk[lo:lo + CHUNK, :GLA_KW]
        k = qk[lo:lo + CHUNK, GLA_KW:]
        v = v_ref[0, lo:lo + CHUNK, :].astype(BF16)
        bcum = jnp.dot(tri, la[lo:lo + CHUNK], preferred_element_type=F32, precision=HIGHEST)
        bend = bcum[0:1] if reverse else bcum[CHUNK - 1:CHUNK]
        qd = q * jnp.exp(bcum)
        kd = (k * jnp.exp(-bcum)).astype(BF16)
        kd2 = (k * jnp.exp(bend - bcum)).astype(BF16)
        st = st_ref[...]
        o_inter = lax.dot_general(qd.astype(BF16), st.astype(BF16), _NT, preferred_element_type=F32)
        for h in range(GLA_HEADS):
            qm = jnp.where(khead == h, qd, 0.0).astype(BF16)
            s = lax.dot_general(qm, kd, _NT, preferred_element_type=F32)
            s = jnp.where(keep, s, 0.0).astype(BF16)
            vs = slice(h * GLA_DV, (h + 1) * GLA_DV)
            o_ref[0, lo:lo + CHUNK, vs] = (jnp.dot(s, v[:, vs], preferred_element_type=F32)
                                           + o_inter[:, vs])
        upd = jnp.dot(v.T, kd2, preferred_element_type=F32)
        st_ref[...] = st * jnp.exp(bend) + jnp.where(same_head, upd, 0.0)


def _gla(qk, v, z, cs, sn, wa, ba, n_ctx_blocks, reverse):
    b, t, _ = qk.shape
    nblk = t // GLA_BLOCK

    if reverse:
        def blk(j):
            return jnp.where(j < n_ctx_blocks, n_ctx_blocks - 1 - j, nblk - 1 - (j - n_ctx_blocks))
    else:
        def blk(j):
            return j

    tok = lambda n: pl.BlockSpec((1, GLA_BLOCK, n), lambda i, j: (i, blk(j), 0))
    tab = lambda n: pl.BlockSpec((GLA_BLOCK, n), lambda i, j: (blk(j), 0))
    return pl.pallas_call(
        functools.partial(_gla_kernel, reverse=reverse),
        grid=(b, nblk),
        in_specs=[tok(2 * GLA_KW), tok(GLA_VW), tok(LANE), tab(2 * GLA_KW), tab(2 * GLA_KW),
                  pl.BlockSpec((LANE, GLA_KW), lambda i, j: (0, 0)),
                  pl.BlockSpec((1, GLA_KW), lambda i, j: (0, 0))],
        out_specs=tok(GLA_VW),
        out_shape=jax.ShapeDtypeStruct((b, t, GLA_VW), F32),
        scratch_shapes=[pltpu.VMEM((GLA_VW, GLA_KW), F32)],
        compiler_params=_cp(("parallel", "arbitrary"), 40),
        name="gla_bwd" if reverse else "gla_fwd",
    )(qk, v, z, cs, sn, wa, ba)


def _na_kernel(*refs):
    q_ref = refs[0]
    k_refs = refs[1:1 + NA_KH]
    v_refs = refs[1 + NA_KH:1 + 2 * NA_KH]
    kc_ref, vc_ref, tb_ref, o_ref, s_ref, p_ref = refs[1 + 2 * NA_KH:]
    scale = NA_DH ** -0.5
    nw = NA_KH * GRID_W
    half = lax.broadcasted_iota(I32, (1, LANE), 1) // NA_DH
    for p in range(NA_HEADS // 2):
        ls = slice(p * LANE, (p + 1) * LANE)
        qp = q_ref[0, :, ls]
        kw = jnp.concatenate([r[0, :, ls] for r in k_refs], axis=0)
        kc = kc_ref[0, :, ls]
        for s in range(2):
            qm = jnp.where(half == s, qp, jnp.zeros_like(qp))
            s_ref[2 * p + s, :, :nw] = (lax.dot_general(qm, kw, _NT, preferred_element_type=F32) * scale
                                        + tb_ref[0, 2 * p + s])
            s_ref[2 * p + s, :, nw:] = lax.dot_general(qm, kc, _NT, preferred_element_type=F32) * scale
    for h in range(NA_HEADS):
        sc = s_ref[h]
        e = jnp.exp(sc - jnp.max(sc, axis=-1, keepdims=True))
        p_ref[h] = (e * (1.0 / jnp.sum(e, axis=-1, keepdims=True))).astype(BF16)
    for p in range(NA_HEADS // 2):
        ls = slice(p * LANE, (p + 1) * LANE)
        vw = jnp.concatenate([r[0, :, ls] for r in v_refs], axis=0)
        vc = vc_ref[0, :, ls]
        outs = [jnp.dot(p_ref[2 * p + s, :, :nw], vw, preferred_element_type=F32)
                + jnp.dot(p_ref[2 * p + s, :, nw:], vc, preferred_element_type=F32) for s in range(2)]
        o_ref[0, :, ls] = jnp.where(half == 0, outs[0], outs[1]).astype(o_ref.dtype)


def _na_bias_tables(rpb):
    w = jnp.arange(GRID_W, dtype=I32)
    c0 = jnp.clip(w - NA_KW // 2, 0, GRID_W - NA_KW)
    cp = w[None, :]
    valid = (cp >= c0[:, None]) & (cp < c0[:, None] + NA_KW)
    off = jnp.clip(cp - w[:, None] + (NA_KW - 1), 0, 2 * NA_KW - 2)
    t = jnp.where(valid[None, None], rpb[:, :, off], MASK_NEG)
    tabs = [jnp.concatenate([t[:, s + a] for a in range(NA_KH)], axis=-1) for s in range(NA_KH)]
    tabs.append(jnp.full_like(tabs[0], MASK_NEG))
    return jnp.stack(tabs, axis=0).astype(F32)


def _na(nq, nk, nv, tables, lc):
    b, t, _ = nq.shape
    n_ctx = lc // GRID_W
    rows = (t - lc) // GRID_W
    assert rows >= NA_KH

    def row0(j):
        r = jnp.maximum(j - n_ctx, 0)
        return r, jnp.clip(r - NA_KH // 2, 0, rows - NA_KH)

    def win(a):
        return pl.BlockSpec((1, GRID_W, NA_W), lambda i, j: (i, n_ctx + row0(j)[1] + a, 0))

    def tab_idx(i, j):
        r, r0 = row0(j)
        return (jnp.where(j < n_ctx, NA_KH, r0 - r + NA_KH - 1), 0, 0, 0)

    return pl.pallas_call(
        _na_kernel,
        grid=(b, t // GRID_W),
        in_specs=([pl.BlockSpec((1, GRID_W, NA_W), lambda i, j: (i, j, 0))]
                  + [win(a) for a in range(NA_KH)] + [win(a) for a in range(NA_KH)]
                  + [pl.BlockSpec((1, lc, NA_W), lambda i, j: (i, 0, 0))] * 2
                  + [pl.BlockSpec((1, NA_HEADS, GRID_W, NA_KH * GRID_W), tab_idx)]),
        out_specs=pl.BlockSpec((1, GRID_W, NA_W), lambda i, j: (i, j, 0)),
        out_shape=jax.ShapeDtypeStruct((b, t, NA_W), BF16),
        scratch_shapes=[pltpu.VMEM((NA_HEADS, GRID_W, NA_KH * GRID_W + lc), F32),
                        pltpu.VMEM((NA_HEADS, GRID_W, NA_KH * GRID_W + lc), BF16)],
        compiler_params=_cp(("parallel", "arbitrary"), 40),
        name="natten",
    )(nq, *([nk] * NA_KH), *([nv] * NA_KH), nk, nv, tables)


def _merge_kernel(x_ref, of_ref, ob_ref, r_ref, na_ref, gt_ref, mod_ref, gg_ref, wpg_ref, wpn_ref, wo_ref,
                  o_ref):
    d = x_ref.shape[-1]
    o = of_ref[0] + ob_ref[0]
    r = r_ref[0]
    parts = []
    for h in range(GLA_HEADS):
        oh = o[:, h * GLA_DV:(h + 1) * GLA_DV]
        ms = jnp.mean(oh * oh, axis=-1, keepdims=True)
        parts.append(oh * lax.rsqrt(ms + EPS) * gg_ref[...])
    on = jnp.concatenate(parts, axis=-1) * (r * jax.nn.sigmoid(r))
    ya = jnp.dot(on.astype(BF16), wpg_ref[...], preferred_element_type=F32)
    yb = jnp.dot(na_ref[0], wpn_ref[...], preferred_element_type=F32)
    gt = gt_ref[0]
    y = jax.nn.sigmoid(gt[:, :d]) * ya + jax.nn.sigmoid(gt[:, d:]) * yb
    y = jnp.dot(y.astype(BF16), wo_ref[...], preferred_element_type=F32)
    o_ref[0] = x_ref[0] + mod_ref[0, 0][2:3] * y


def _merge(xa, o_f, o_b, r, na, gates, modseg, g_gla, wpg, wpn, wo, n_ctx_tiles):
    b, t, d = xa.shape
    tok = lambda n: pl.BlockSpec((1, ROW_TILE, n), lambda i, j: (i, j, 0))
    full = lambda a: pl.BlockSpec(a.shape, lambda i, j: (0,) * a.ndim)
    return pl.pallas_call(
        _merge_kernel,
        grid=(b, t // ROW_TILE),
        in_specs=[tok(d), tok(GLA_VW), tok(GLA_VW), tok(GLA_VW), tok(NA_W), tok(2 * d),
                  pl.BlockSpec((1, 1, SUBLANE, d), lambda i, j: (i, (j >= n_ctx_tiles).astype(I32), 0, 0)),
                  full(g_gla), full(wpg), full(wpn), full(wo)],
        out_specs=tok(d),
        out_shape=jax.ShapeDtypeStruct((b, t, d), F32),
        compiler_params=_cp(("parallel", "arbitrary"), 40),
        name="merge",
    )(xa, o_f, o_b, r, na, gates, modseg, g_gla, wpg, wpn, wo)


def _topk_rows(s, ids, k):
    rows = s.shape[0]
    rid = lax.broadcasted_iota(I32, s.shape, 0).astype(F32)
    vals, out_ids = [], []
    for _ in range(k):
        m = jnp.max(s, axis=0, keepdims=True)
        pos = jnp.min(jnp.where(s == m, rid, float(rows)), axis=0, keepdims=True)
        hit = rid == pos
        vals.append(m)
        out_ids.append(pos if ids is None else jnp.max(jnp.where(hit, ids, -1.0), axis=0, keepdims=True))
        s = jnp.where(hit, -jnp.inf, s)
    return jnp.concatenate(vals, axis=0), jnp.concatenate(out_ids, axis=0)


def _staircase_candidates(v0, i0, v1, i1):
    k = PEER_TOPK
    sub = lax.broadcasted_iota(I32, (SUBLANE, v0.shape[1]), 0)
    vals = [v0[0:1] + v1]
    ids = [i0[0:1] * float(N_KEYS) + i1]
    for a in range(1, k // 2):
        nb = k // (a + 1)
        live = sub < nb
        vals.append(jnp.where(live, v0[a:a + 1] + v1[:SUBLANE], -jnp.inf))
        ids.append(i0[a:a + 1] * float(N_KEYS) + i1[:SUBLANE])
    vals.append(v0[k // 2:] + v1[0:1])
    ids.append(i0[k // 2:] * float(N_KEYS) + i1[0:1])
    return jnp.concatenate(vals, axis=0), jnp.concatenate(ids, axis=0)


def _peer1_kernel(x_ref, mod_ref, g_ref, wq_ref, sk_ref, idx_ref, pair_ref, gate_ref):
    m = mod_ref[0, 0]
    h = _norm_mod(x_ref[0], g_ref[...], m[3:4], m[4:5]).astype(BF16)
    half = PEER_DK // 2
    chosen = []
    for hd in range(PEER_HEADS):
        tops = []
        for p in range(2):
            col = (2 * hd + p) * half
            qp = jnp.dot(h, wq_ref[:, col:col + half], preferred_element_type=F32).astype(BF16)
            st = lax.dot_general(sk_ref[hd, p], qp, _NT, preferred_element_type=F32)
            tops.append(_topk_rows(st, None, PEER_TOPK))
        (v0, i0), (v1, i1) = tops
        cand, cidx = _staircase_candidates(v0, i0, v1, i1)
        best, experts = _topk_rows(cand, cidx, PEER_TOPK)
        e = jnp.exp(best - best[0:1])
        rs = slice(hd * PEER_TOPK, (hd + 1) * PEER_TOPK)
        gate_ref[0, rs, :] = e / jnp.sum(e, axis=0, keepdims=True)
        idx_ref[0, rs, :] = experts.astype(I32)
        chosen.append(experts)
    pair_ref[0] = lax.shift_right_logical(jnp.concatenate(chosen, axis=0).T.astype(I32), 1) * SUBLANE


def _peer_select(xa, modseg, g, wq, sk, lc):
    b, t, d = xa.shape
    ne = PEER_HEADS * PEER_TOPK
    tm = SELECT_TILE
    n_ctx_tiles = lc // tm
    return pl.pallas_call(
        _peer1_kernel,
        grid=(b, t // tm),
        in_specs=[pl.BlockSpec((1, tm, d), lambda i, j: (i, j, 0)),
                  pl.BlockSpec((1, 1, SUBLANE, d), lambda i, j: (i, (j >= n_ctx_tiles).astype(I32), 0, 0)),
                  pl.BlockSpec((1, d), lambda i, j: (0, 0)),
                  pl.BlockSpec(wq.shape, lambda i, j: (0, 0)),
                  pl.BlockSpec(sk.shape, lambda i, j: (0, 0, 0, 0))],
        out_specs=[pl.BlockSpec((1, ne, tm), lambda i, j: (i, 0, j)),
                   pl.BlockSpec((1, tm, ne), lambda i, j: (i, j, 0)),
                   pl.BlockSpec((1, ne, tm), lambda i, j: (i, 0, j))],
        out_shape=[jax.ShapeDtypeStruct((b, ne, t), I32), jax.ShapeDtypeStruct((b, t, ne), I32),
                   jax.ShapeDtypeStruct((b, ne, t), F32)],
        compiler_params=_cp(("parallel", "arbitrary"), 40),
        name="peer_select",
    )(xa, modseg, g, wq, sk)


PAIR_ROWS = 2 * SUBLANE


def _gather_tiles(tbl_ref, pair_ref, t, ne):
    base = pl.multiple_of(t * ne, ne)
    tiles = []
    for j in range(ne):
        row = pl.multiple_of(pair_ref[base + j], SUBLANE)
        tiles.append(pltpu.bitcast(tbl_ref[pl.ds(row, SUBLANE), :], BF16))
    return jnp.concatenate(tiles, axis=0)


TOKENS_PER_TRIP = 16


def _for_token_groups(tb, body):
    def trip(i, carry):
        for u in range(TOKENS_PER_TRIP):
            body(TOKENS_PER_TRIP * i + u)
        return carry

    lax.fori_loop(0, tb // TOKENS_PER_TRIP, trip, 0)


PRODUCT_GROUPS = 4


def _peer2_kernel(pair_ref, idx_ref, x_ref, mod_ref, g_ref, gate_ref, tbl_ref, cc_ref, accp_ref, acct_ref):
    tb = accp_ref.shape[0]
    ne = idx_ref.shape[1]
    kk = ne * PAIR_ROWS
    grp = 2 * PAIR_ROWS
    n_p = PRODUCT_GROUPS * PAIR_ROWS
    n_t = ne - n_p
    kt = n_t * PAIR_ROWS
    d = x_ref.shape[1] * x_ref.shape[2]
    a = g_ref[0] * (1.0 + mod_ref[4])
    shift = mod_ref[3]
    srow = lax.broadcasted_iota(I32, (grp, PAIR_ROWS * PAIR_ROWS), 0)
    scol = lax.broadcasted_iota(I32, (grp, PAIR_ROWS * PAIR_ROWS), 1)
    sel = (((scol // PAIR_ROWS) == (srow % PAIR_ROWS)) & ((scol % 2) == (srow // PAIR_ROWS))).astype(BF16)
    dlane = lax.broadcasted_iota(I32, (SUBLANE, kt), 1)
    dsub = lax.broadcasted_iota(I32, (SUBLANE, kt), 0)
    diag = ((dlane % PAIR_ROWS) // 2) == dsub

    def token(t):
        x = x_ref[t]
        ms = jnp.sum(x * x, keepdims=True) * (1.0 / d)
        h = x * lax.rsqrt(ms + EPS) * a + shift
        hu = lax.bitcast_convert_type(h.astype(BF16).astype(F32), U32)
        hp = pltpu.bitcast(hu | lax.shift_right_logical(hu, jnp.uint32(16)), BF16)
        hb = jnp.concatenate([h, jnp.zeros_like(h)], axis=0).astype(BF16)
        base = pl.multiple_of(t * ne, ne)

        def tile(j):
            row = pl.multiple_of(pair_ref[base + j], SUBLANE)
            return pltpu.bitcast(tbl_ref[pl.ds(row, SUBLANE), :], BF16)

        parts = []
        for gi in range(PRODUCT_GROUPS):
            prods = [tile(j) * hp for j in range(gi * PAIR_ROWS, (gi + 1) * PAIR_ROWS)]
            parts.append(jnp.dot(sel, jnp.concatenate(prods, axis=0), preferred_element_type=F32))
        r = jnp.concatenate(parts, axis=0)
        accp_ref[pl.ds(t, 1), :] = jnp.sum(r.T, axis=0, keepdims=True)
        tiles = jnp.concatenate([tile(j) for j in range(n_p, ne)], axis=0)
        out = lax.dot_general(hb, tiles, _NT, preferred_element_type=F32)
        acct_ref[pl.ds(t, 1), :] = jnp.sum(jnp.where(diag, out[:SUBLANE], 0.0), axis=0, keepdims=True)

    _for_token_groups(tb, token)

    atp = accp_ref[...].T
    kq = lax.broadcasted_iota(I32, (2 * n_t, kt), 1)
    rq = lax.broadcasted_iota(I32, (2 * n_t, kt), 0)
    selt = (((kq // PAIR_ROWS) == (rq % n_t)) & ((kq % 2) == (rq // n_t))).astype(F32)
    att = lax.dot_general(selt, acct_ref[...], _NT, preferred_element_type=F32, precision=HIGHEST)
    a0 = jnp.concatenate([atp[gi * grp:gi * grp + PAIR_ROWS] for gi in range(PRODUCT_GROUPS)] + [att[:n_t]], axis=0)
    a1 = jnp.concatenate([atp[gi * grp + PAIR_ROWS:(gi + 1) * grp] for gi in range(PRODUCT_GROUPS)] + [att[n_t:]],
                         axis=0)
    par = idx_ref[0] & 1
    act = jnp.where(par == 0, a0, a1)
    coef = jax.nn.gelu(act, approximate=True) * gate_ref[0]
    ke = lax.broadcasted_iota(I32, (ne, kk), 1)
    re = lax.broadcasted_iota(I32, (ne, kk), 0)
    expand = ((ke // PAIR_ROWS) == re).astype(BF16)
    cexp = jnp.dot(coef.T.astype(BF16), expand, preferred_element_type=F32)
    pexp = jnp.dot(par.astype(F32).T.astype(BF16), expand, preferred_element_type=F32)
    lanepar = (lax.broadcasted_iota(I32, (tb, kk), 1) % 2).astype(F32)
    cc_ref[...] = jnp.where(pexp == lanepar, cexp, 0.0)


def _peer3_kernel(pair_ref, cc_ref, x_ref, mod_ref, tbl_ref, o_ref):
    tb, kk = cc_ref.shape
    ne = kk // PAIR_ROWS
    ga = mod_ref[5]
    lane = lax.broadcasted_iota(I32, (PAIR_ROWS, kk), 1)
    sub = lax.broadcasted_iota(I32, (PAIR_ROWS, kk), 0)
    diag = ((lane % PAIR_ROWS) // 2) == sub

    def token(t):
        row = cc_ref[pl.ds(t, 1), :]
        cm = jnp.where(diag, jnp.broadcast_to(row, (PAIR_ROWS, kk)), 0.0).astype(BF16)
        tiles = _gather_tiles(tbl_ref, pair_ref, t, ne)
        o = jnp.dot(cm, tiles, preferred_element_type=F32)
        o_ref[t] = x_ref[t] + ga * o[:SUBLANE]

    _for_token_groups(tb, token)


def _peer_blocks(b, t, lc):
    nj = t // PEER_TOKENS
    n_ctx = lc // PEER_TOKENS
    tokb = pl.BlockSpec((PEER_TOKENS, SUBLANE, LANE), lambda n: (n, 0, 0))
    modb = pl.BlockSpec((SUBLANE, SUBLANE, LANE), lambda n: ((n // nj) * 2 + ((n % nj) >= n_ctx).astype(I32), 0, 0))
    return nj, tokb, modb


def _peer_act(x3, pair2, idx, gate, modv, g3, tbl, b, t, lc):
    ne = PEER_HEADS * PEER_TOPK
    kk = ne * PAIR_ROWS
    nj, tokb, modb = _peer_blocks(b, t, lc)
    expb = pl.BlockSpec((1, ne, PEER_TOKENS), lambda n: (n // nj, 0, n % nj))
    return pl.pallas_call(
        _peer2_kernel,
        grid=(b * nj,),
        in_specs=[pl.BlockSpec((PEER_TOKENS * ne,), lambda n: (n,), memory_space=pltpu.SMEM),
                  expb, tokb, modb,
                  pl.BlockSpec((1, SUBLANE, LANE), lambda n: (0, 0, 0)),
                  expb,
                  pl.BlockSpec(tbl.shape, lambda n: (0, 0), pipeline_mode=pl.Buffered(1))],
        out_specs=pl.BlockSpec((PEER_TOKENS, kk), lambda n: (n, 0)),
        out_shape=jax.ShapeDtypeStruct((b * t, kk), F32),
        scratch_shapes=[pltpu.VMEM((PEER_TOKENS, 2 * PRODUCT_GROUPS * PAIR_ROWS), F32),
                        pltpu.VMEM((PEER_TOKENS, (ne - PRODUCT_GROUPS * PAIR_ROWS) * PAIR_ROWS), F32)],
        compiler_params=_cp(("arbitrary",), 56),
        name="peer_act",
    )(pair2, idx, x3, modv, g3, gate, tbl)


def _peer_out(x3, pair2, cc, modv, tbl, b, t, lc):
    ne = PEER_HEADS * PEER_TOPK
    kk = ne * PAIR_ROWS
    nj, tokb, modb = _peer_blocks(b, t, lc)
    return pl.pallas_call(
        _peer3_kernel,
        grid=(b * nj,),
        in_specs=[pl.BlockSpec((PEER_TOKENS * ne,), lambda n: (n,), memory_space=pltpu.SMEM),
                  pl.BlockSpec((PEER_TOKENS, kk), lambda n: (n, 0)),
                  tokb, modb,
                  pl.BlockSpec(tbl.shape, lambda n: (0, 0), pipeline_mode=pl.Buffered(1))],
        out_specs=tokb,
        out_shape=jax.ShapeDtypeStruct(x3.shape, F32),
        compiler_params=_cp(("arbitrary",), 56),
        name="peer_out",
    )(pair2, cc, x3, modv, tbl)


def _pack_experts(w):
    e, d = w.shape
    assert d == VREG_ELEMS
    bits = lax.bitcast_convert_type(w.astype(BF16), jnp.uint16).astype(U32).reshape(e // 2, 2, SUBLANE, LANE)
    return (bits[:, 0] | (bits[:, 1] << 16)).reshape(e // 2 * SUBLANE, LANE)


def _peer(xa, modseg, modv, g2, w_query, sub_keys, expert_u, expert_v, lc):
    b, t, d = xa.shape
    idx, pair, gate = _peer_select(xa, modseg, g2.reshape(1, d), w_query.astype(BF16), sub_keys.astype(BF16), lc)
    pair2 = pair.reshape(b * t * idx.shape[1])
    x3 = xa.reshape(b * t, SUBLANE, LANE)
    cc = _peer_act(x3, pair2, idx, gate, modv, g2.reshape(1, SUBLANE, LANE), _pack_experts(expert_u), b, t, lc)
    x3 = _peer_out(x3, pair2, cc, modv, _pack_experts(expert_v), b, t, lc)
    return x3.reshape(b, t, d)


def _final_kernel(x_ref, g_ref, o_ref):
    x = x_ref[0]
    ms = jnp.mean(x * x, axis=-1, keepdims=True)
    o_ref[0] = x * lax.rsqrt(ms + EPS) * g_ref[...]


def _final_norm(xa, g, lc):
    b, t, d = xa.shape
    off = lc // ROW_TILE
    return pl.pallas_call(
        _final_kernel,
        grid=(b, (t - lc) // ROW_TILE),
        in_specs=[pl.BlockSpec((1, ROW_TILE, d), lambda i, j: (i, j + off, 0)),
                  pl.BlockSpec((1, d), lambda i, j: (0, 0))],
        out_specs=pl.BlockSpec((1, ROW_TILE, d), lambda i, j: (i, j, 0)),
        out_shape=jax.ShapeDtypeStruct((b, t - lc, d), F32),
        compiler_params=_cp(("parallel", "arbitrary"), 32),
        name="final_norm",
    )(xa, g)


def _rope_tables(l, lc):
    t = jnp.arange(l, dtype=I32)
    quarter = GLA_DK // 4
    freqs = ROPE_BASE ** (-jnp.arange(quarter, dtype=F32) / quarter)
    ang_r = (t // GRID_W).astype(F32)[:, None] * freqs
    ang_c = (t % GRID_W).astype(F32)[:, None] * freqs
    cos = jnp.concatenate([jnp.cos(ang_r)] * 2 + [jnp.cos(ang_c)] * 2, axis=-1)
    sin = jnp.concatenate([-jnp.sin(ang_r), jnp.sin(ang_r), -jnp.sin(ang_c), jnp.sin(ang_c)], axis=-1)
    cos = jnp.tile(cos, (1, GLA_HEADS))
    sin = jnp.tile(sin, (1, GLA_HEADS))
    qs = GLA_DK ** -0.5
    cs = jnp.concatenate([cos * qs, cos], axis=-1)
    sn = jnp.concatenate([sin * qs, sin], axis=-1)
    cs_ctx = jnp.concatenate([jnp.full((lc, GLA_KW), qs, F32), jnp.ones((lc, GLA_KW), F32)], axis=-1)
    return (jnp.concatenate([cs_ctx, cs], axis=0),
            jnp.concatenate([jnp.zeros((lc, 2 * GLA_KW), F32), sn], axis=0))


def _permute_w_in(w, d):
    sizes = (GLA_KW, GLA_KW, GLA_VW, 2 * GATE_RANK, GLA_VW, NA_W, NA_W, NA_W, 2 * d)
    offs = [0]
    for s in sizes:
        offs.append(offs[-1] + s)
    q, k, v, z, r, nq, nk, nv, gates = [w[:, offs[i]:offs[i + 1]] for i in range(len(sizes))]
    zpad = jnp.pad(z, ((0, 0), (0, LANE - 2 * GATE_RANK)))
    return jnp.concatenate([q, k, v, r, nq, nk, nv, gates, zpad], axis=1).astype(BF16)


def _gate_weights(w_a, b_a, first_row):
    wa = jnp.zeros((LANE, GLA_KW), F32).at[first_row:first_row + GATE_RANK].set(w_a)
    return wa, b_a.reshape(1, GLA_KW)


def kernel(x, c, ctx, c_ctx, w_mod, b_mod, g_norm1, w_in, w_alpha_f, b_alpha_f, w_alpha_b, b_alpha_b, g_gla, rpb,
           w_proj_gla, w_proj_na, w_out, g_norm2, w_query, sub_keys, expert_u, expert_v, g_final):
    b, l, d = x.shape
    lc = ctx.shape[1]
    depth = w_mod.shape[0]
    t = lc + l
    assert d == VREG_ELEMS and lc % ROW_TILE == 0 and l % ROW_TILE == 0 and l % GRID_W == 0
    n_ctx_tiles = lc // ROW_TILE

    rows = -(-(b + 1) // SUBLANE) * SUBLANE
    cin = jnp.zeros((rows, d), F32).at[:b].set(c).at[b].set(c_ctx)
    mod_all = _modulation(cin, w_mod, b_mod)

    cs, sn = _rope_tables(l, lc)
    xa = jnp.concatenate([ctx, x], axis=1)

    for i in range(depth):
        mod = mod_all[i].reshape(rows, N_MOD, d)
        modseg = jnp.stack([jnp.broadcast_to(mod[b], (b, N_MOD, d)), mod[:b]], axis=1)
        modseg = jnp.pad(modseg, ((0, 0), (0, 0), (0, SUBLANE - N_MOD), (0, 0)))
        modv = modseg.reshape(b * 2 * SUBLANE, SUBLANE, LANE)

        qk, v, r, nq, nk, nv, gates, z = _project(xa, modseg, g_norm1[i].reshape(1, d),
                                                  _permute_w_in(w_in[i], d), n_ctx_tiles)
        waf, baf = _gate_weights(w_alpha_f[i], b_alpha_f[i], 0)
        wab, bab = _gate_weights(w_alpha_b[i], b_alpha_b[i], GATE_RANK)
        o_f = _gla(qk, v, z, cs, sn, waf, baf, lc // GLA_BLOCK, reverse=False)
        o_b = _gla(qk, v, z, cs, sn, wab, bab, lc // GLA_BLOCK, reverse=True)
        na = _na(nq, nk, nv, _na_bias_tables(rpb[i]), lc)
        xa = _merge(xa, o_f, o_b, r, na, gates, modseg, g_gla[i].reshape(1, GLA_DV),
                    w_proj_gla[i].astype(BF16), w_proj_na[i].astype(BF16), w_out[i].astype(BF16), n_ctx_tiles)

        xa = _peer(xa, modseg, modv, g_norm2[i], w_query[i], sub_keys[i], expert_u[i], expert_v[i], lc)

    return _final_norm(xa, g_final.reshape(1, d), lc)
```

```python
import functools

import jax
import jax.numpy as jnp
from jax import lax
from jax.experimental import pallas as pl
from jax.experimental.pallas import tpu as pltpu

F32 = jnp.float32
BF16 = jnp.bfloat16
I32 = jnp.int32
U32 = jnp.uint32
HIGHEST = lax.Precision.HIGHEST

GRID_W = 64
EPS = 1e-6
N_MOD = 6
GLA_HEADS = 4
GLA_DK = 64
GLA_DV = 128
GLA_KW = GLA_HEADS * GLA_DK
GLA_VW = GLA_HEADS * GLA_DV
GATE_RANK = 16
GATE_TAU = 16.0
CHUNK = 64
ROPE_BASE = 10000.0
NA_HEADS = 8
NA_DH = 64
NA_W = NA_HEADS * NA_DH
NA_KH = 8
NA_KW = 16
PEER_HEADS = 8
N_KEYS = 128
PEER_DK = 256
PEER_TOPK = 16

LANE = 128
SUBLANE = 8
VREG_ELEMS = LANE * SUBLANE

ROW_TILE = 256
GLA_BLOCK = 256
SELECT_TILE = 256
PEER_TOKENS = 128
MASK_NEG = -1e30

_NT = (((1,), (1,)), ((), ()))


def _cp(sem, vmem_mb):
    return pltpu.CompilerParams(dimension_semantics=sem, vmem_limit_bytes=vmem_mb * 1024 * 1024)


def _norm_mod(x, g, shift, scale):
    ms = jnp.mean(x * x, axis=-1, keepdims=True)
    return (x * lax.rsqrt(ms + EPS) * g) * (1.0 + scale) + shift


def _mod_kernel(c_ref, w_ref, b_ref, o_ref):
    a = c_ref[...]
    a = a * jax.nn.sigmoid(a)
    o_ref[0] = jnp.dot(a, w_ref[0], preferred_element_type=F32, precision=HIGHEST) + b_ref[0]


def _modulation(cin, w_mod, b_mod):
    depth, d, n = w_mod.shape
    rows = cin.shape[0]
    tn = n // 4
    return pl.pallas_call(
        _mod_kernel,
        grid=(depth, n // tn),
        in_specs=[pl.BlockSpec((rows, d), lambda l, j: (0, 0)),
                  pl.BlockSpec((1, d, tn), lambda l, j: (l, 0, j)),
                  pl.BlockSpec((1, 1, tn), lambda l, j: (l, 0, j))],
        out_specs=pl.BlockSpec((1, rows, tn), lambda l, j: (l, 0, j)),
        out_shape=jax.ShapeDtypeStruct((depth, rows, n), F32),
        compiler_params=_cp(("arbitrary", "arbitrary"), 40),
        name="modulation",
    )(cin, w_mod, b_mod.reshape(depth, 1, n))


_PROJ_OUTS = (("qk", 2 * GLA_KW, F32), ("v", GLA_VW, F32), ("r", GLA_VW, F32),
              ("nq", NA_W, BF16), ("nk", NA_W, BF16), ("nv", NA_W, BF16),
              ("gates", None, F32), ("z", LANE, F32))


def _proj_kernel(x_ref, mod_ref, g_ref, w_ref, *out_refs):
    m = mod_ref[0, 0]
    h = _norm_mod(x_ref[0], g_ref[...], m[0:1], m[1:2]).astype(BF16)
    col = 0
    for ref in out_refs:
        n = ref.shape[-1]
        ref[0] = jnp.dot(h, w_ref[:, col:col + n], preferred_element_type=F32).astype(ref.dtype)
        col += n


def _project(xa, modseg, g, w_perm, n_ctx_tiles):
    b, t, d = xa.shape
    outs = [(name, (2 * d if n is None else n), dt) for name, n, dt in _PROJ_OUTS]
    assert sum(n for _, n, _ in outs) == w_perm.shape[1]
    return pl.pallas_call(
        _proj_kernel,
        grid=(b, t // ROW_TILE),
        in_specs=[pl.BlockSpec((1, ROW_TILE, d), lambda i, j: (i, j, 0)),
                  pl.BlockSpec((1, 1, SUBLANE, d), lambda i, j: (i, (j >= n_ctx_tiles).astype(I32), 0, 0)),
                  pl.BlockSpec((1, d), lambda i, j: (0, 0)),
                  pl.BlockSpec(w_perm.shape, lambda i, j: (0, 0))],
        out_specs=[pl.BlockSpec((1, ROW_TILE, n), lambda i, j: (i, j, 0)) for _, n, _ in outs],
        out_shape=[jax.ShapeDtypeStruct((b, t, n), dt) for _, n, dt in outs],
        compiler_params=_cp(("parallel", "arbitrary"), 48),
        name="in_proj",
    )(xa, modseg, g, w_perm)


def _gla_kernel(qk_ref, v_ref, z_ref, cs_ref, sn_ref, wa_ref, ba_ref, o_ref, st_ref, *, reverse):
    @pl.when(pl.program_id(1) == 0)
    def _init():
        st_ref[...] = jnp.zeros_like(st_ref)

    kw2 = 2 * GLA_KW
    blk = qk_ref.shape[1]
    qk = qk_ref[0]
    lane = lax.broadcasted_iota(I32, (1, kw2), 1)
    first = (lane % 32) < 16
    partner = jnp.where(first, pltpu.roll(qk, kw2 - 16, 1), pltpu.roll(qk, 16, 1))
    qk = qk * cs_ref[...] + partner * sn_ref[...]
    zz = jnp.dot(z_ref[0], wa_ref[...], preferred_element_type=F32) + ba_ref[...]
    la = (jnp.minimum(zz, 0.0) - jnp.log1p(jnp.exp(-jnp.abs(zz)))) * (1.0 / GATE_TAU)
    la_hi = la.astype(BF16)
    rem = la - la_hi.astype(F32)
    la_mid = rem.astype(BF16)
    la_lo = (rem - la_mid.astype(F32)).astype(BF16)

    ri = lax.broadcasted_iota(I32, (CHUNK, CHUNK), 0)
    ci = lax.broadcasted_iota(I32, (CHUNK, CHUNK), 1)
    keep = (ci >= ri) if reverse else (ci <= ri)
    tri3 = jnp.concatenate([keep.astype(BF16)] * 3, axis=1)
    khead = lax.broadcasted_iota(I32, (1, GLA_KW), 1) // GLA_DK
    vrow = lax.broadcasted_iota(I32, (GLA_VW, GLA_KW), 0) // GLA_DV
    kcol = lax.broadcasted_iota(I32, (GLA_VW, GLA_KW), 1) // GLA_DK
    same_head = vrow == kcol

    order = range(blk // CHUNK)
    if reverse:
        order = reversed(order)
    for c in order:
        lo = c * CHUNK
        q = qk[lo:lo + CHUNK, :GLA_KW]
        k = qk[lo:lo + CHUNK, GLA_KW:]
        v = v_ref[0, lo:lo + CHUNK, :].astype(BF16)
        pieces = jnp.concatenate([p[lo:lo + CHUNK] for p in (la_hi, la_mid, la_lo)], axis=0)
        bcum = jnp.dot(tri3, pieces, preferred_element_type=F32)
        bend = bcum[0:1] if reverse else bcum[CHUNK - 1:CHUNK]
        qd = q * jnp.exp(bcum)
        kd = (k * jnp.exp(-bcum)).astype(BF16)
        kd2 = (k * jnp.exp(bend - bcum)).astype(BF16)
        st = st_ref[...]
        o_inter = lax.dot_general(qd.astype(BF16), st.astype(BF16), _NT, preferred_element_type=F32)
        for h in range(GLA_HEADS):
            qm = jnp.where(khead == h, qd, 0.0).astype(BF16)
            s = lax.dot_general(qm, kd, _NT, preferred_element_type=F32)
            s = jnp.where(keep, s, 0.0).astype(BF16)
            vs = slice(h * GLA_DV, (h + 1) * GLA_DV)
            o_ref[0, lo:lo + CHUNK, vs] = (jnp.dot(s, v[:, vs], preferred_element_type=F32)
                                           + o_inter[:, vs])
        upd = jnp.dot(v.T, kd2, preferred_element_type=F32)
        st_ref[...] = st * jnp.exp(bend) + jnp.where(same_head, upd, 0.0)


def _gla(qk, v, z, cs, sn, wa, ba, n_ctx_blocks, reverse):
    b, t, _ = qk.shape
    nblk = t // GLA_BLOCK

    if reverse:
        def blk(j):
            return jnp.where(j < n_ctx_blocks, n_ctx_blocks - 1 - j, nblk - 1 - (j - n_ctx_blocks))
    else:
        def blk(j):
            return j

    tok = lambda n: pl.BlockSpec((1, GLA_BLOCK, n), lambda i, j: (i, blk(j), 0))
    tab = lambda n: pl.BlockSpec((GLA_BLOCK, n), lambda i, j: (blk(j), 0))
    return pl.pallas_call(
        functools.partial(_gla_kernel, reverse=reverse),
        grid=(b, nblk),
        in_specs=[tok(2 * GLA_KW), tok(GLA_VW), tok(LANE), tab(2 * GLA_KW), tab(2 * GLA_KW),
                  pl.BlockSpec((LANE, GLA_KW), lambda i, j: (0, 0)),
                  pl.BlockSpec((1, GLA_KW), lambda i, j: (0, 0))],
        out_specs=tok(GLA_VW),
        out_shape=jax.ShapeDtypeStruct((b, t, GLA_VW), F32),
        scratch_shapes=[pltpu.VMEM((GLA_VW, GLA_KW), F32)],
        compiler_params=_cp(("parallel", "arbitrary"), 40),
        name="gla_bwd" if reverse else "gla_fwd",
    )(qk, v, z, cs, sn, wa, ba)


def _na_kernel(*refs):
    q_ref = refs[0]
    k_refs = refs[1:1 + NA_KH]
    v_refs = refs[1 + NA_KH:1 + 2 * NA_KH]
    kc_ref, vc_ref, tb_ref, o_ref, s_ref, p_ref = refs[1 + 2 * NA_KH:]
    scale = NA_DH ** -0.5
    nw = NA_KH * GRID_W
    half = lax.broadcasted_iota(I32, (1, LANE), 1) // NA_DH
    for p in range(NA_HEADS // 2):
        ls = slice(p * LANE, (p + 1) * LANE)
        qp = q_ref[0, :, ls]
        kw = jnp.concatenate([r[0, :, ls] for r in k_refs], axis=0)
        kc = kc_ref[0, :, ls]
        for s in range(2):
            qm = jnp.where(half == s, qp, jnp.zeros_like(qp))
            s_ref[2 * p + s, :, :nw] = (lax.dot_general(qm, kw, _NT, preferred_element_type=F32) * scale
                                        + tb_ref[0, 2 * p + s])
            s_ref[2 * p + s, :, nw:] = lax.dot_general(qm, kc, _NT, preferred_element_type=F32) * scale
    for h in range(NA_HEADS):
        sc = s_ref[h]
        e = jnp.exp(sc - jnp.max(sc, axis=-1, keepdims=True))
        p_ref[h] = (e * (1.0 / jnp.sum(e, axis=-1, keepdims=True))).astype(BF16)
    for p in range(NA_HEADS // 2):
        ls = slice(p * LANE, (p + 1) * LANE)
        vw = jnp.concatenate([r[0, :, ls] for r in v_refs], axis=0)
        vc = vc_ref[0, :, ls]
        outs = [jnp.dot(p_ref[2 * p + s, :, :nw], vw, preferred_element_type=F32)
                + jnp.dot(p_ref[2 * p + s, :, nw:], vc, preferred_element_type=F32) for s in range(2)]
        o_ref[0, :, ls] = jnp.where(half == 0, outs[0], outs[1]).astype(o_ref.dtype)


def _na_bias_tables(rpb):
    w = jnp.arange(GRID_W, dtype=I32)
    c0 = jnp.clip(w - NA_KW // 2, 0, GRID_W - NA_KW)
    cp = w[None, :]
    valid = (cp >= c0[:, None]) & (cp < c0[:, None] + NA_KW)
    off = jnp.clip(cp - w[:, None] + (NA_KW - 1), 0, 2 * NA_KW - 2)
    t = jnp.where(valid[None, None], rpb[:, :, off], MASK_NEG)
    tabs = [jnp.concatenate([t[:, s + a] for a in range(NA_KH)], axis=-1) for s in range(NA_KH)]
    tabs.append(jnp.full_like(tabs[0], MASK_NEG))
    return jnp.stack(tabs, axis=0).astype(F32)


def _na(nq, nk, nv, tables, lc):
    b, t, _ = nq.shape
    n_ctx = lc // GRID_W
    rows = (t - lc) // GRID_W
    assert rows >= NA_KH

    def row0(j):
        r = jnp.maximum(j - n_ctx, 0)
        return r, jnp.clip(r - NA_KH // 2, 0, rows - NA_KH)

    def win(a):
        return pl.BlockSpec((1, GRID_W, NA_W), lambda i, j: (i, n_ctx + row0(j)[1] + a, 0))

    def tab_idx(i, j):
        r, r0 = row0(j)
        return (jnp.where(j < n_ctx, NA_KH, r0 - r + NA_KH - 1), 0, 0, 0)

    return pl.pallas_call(
        _na_kernel,
        grid=(b, t // GRID_W),
        in_specs=([pl.BlockSpec((1, GRID_W, NA_W), lambda i, j: (i, j, 0))]
                  + [win(a) for a in range(NA_KH)] + [win(a) for a in range(NA_KH)]
                  + [pl.BlockSpec((1, lc, NA_W), lambda i, j: (i, 0, 0))] * 2
                  + [pl.BlockSpec((1, NA_HEADS, GRID_W, NA_KH * GRID_W), tab_idx)]),
        out_specs=pl.BlockSpec((1, GRID_W, NA_W), lambda i, j: (i, j, 0)),
        out_shape=jax.ShapeDtypeStruct((b, t, NA_W), BF16),
        scratch_shapes=[pltpu.VMEM((NA_HEADS, GRID_W, NA_KH * GRID_W + lc), F32),
                        pltpu.VMEM((NA_HEADS, GRID_W, NA_KH * GRID_W + lc), BF16)],
        compiler_params=_cp(("parallel", "arbitrary"), 40),
        name="natten",
    )(nq, *([nk] * NA_KH), *([nv] * NA_KH), nk, nv, tables)


def _merge_kernel(x_ref, of_ref, ob_ref, r_ref, na_ref, gt_ref, mod_ref, gg_ref, wpg_ref, wpn_ref, wo_ref,
                  o_ref):
    d = x_ref.shape[-1]
    o = of_ref[0] + ob_ref[0]
    r = r_ref[0]
    parts = []
    for h in range(GLA_HEADS):
        oh = o[:, h * GLA_DV:(h + 1) * GLA_DV]
        ms = jnp.mean(oh * oh, axis=-1, keepdims=True)
        parts.append(oh * lax.rsqrt(ms + EPS) * gg_ref[...])
    on = jnp.concatenate(parts, axis=-1) * (r * jax.nn.sigmoid(r))
    ya = jnp.dot(on.astype(BF16), wpg_ref[...], preferred_element_type=F32)
    yb = jnp.dot(na_ref[0], wpn_ref[...], preferred_element_type=F32)
    gt = gt_ref[0]
    y = jax.nn.sigmoid(gt[:, :d]) * ya + jax.nn.sigmoid(gt[:, d:]) * yb
    y = jnp.dot(y.astype(BF16), wo_ref[...], preferred_element_type=F32)
    o_ref[0] = x_ref[0] + mod_ref[0, 0][2:3] * y


def _merge(xa, o_f, o_b, r, na, gates, modseg, g_gla, wpg, wpn, wo, n_ctx_tiles):
    b, t, d = xa.shape
    tok = lambda n: pl.BlockSpec((1, ROW_TILE, n), lambda i, j: (i, j, 0))
    full = lambda a: pl.BlockSpec(a.shape, lambda i, j: (0,) * a.ndim)
    return pl.pallas_call(
        _merge_kernel,
        grid=(b, t // ROW_TILE),
        in_specs=[tok(d), tok(GLA_VW), tok(GLA_VW), tok(GLA_VW), tok(NA_W), tok(2 * d),
                  pl.BlockSpec((1, 1, SUBLANE, d), lambda i, j: (i, (j >= n_ctx_tiles).astype(I32), 0, 0)),
                  full(g_gla), full(wpg), full(wpn), full(wo)],
        out_specs=tok(d),
        out_shape=jax.ShapeDtypeStruct((b, t, d), F32),
        compiler_params=_cp(("parallel", "arbitrary"), 40),
        name="merge",
    )(xa, o_f, o_b, r, na, gates, modseg, g_gla, wpg, wpn, wo)


def _topk_rows(s, ids, k):
    rows = s.shape[0]
    rid = lax.broadcasted_iota(I32, s.shape, 0).astype(F32)
    vals, out_ids = [], []
    for _ in range(k):
        m = jnp.max(s, axis=0, keepdims=True)
        pos = jnp.min(jnp.where(s == m, rid, float(rows)), axis=0, keepdims=True)
        hit = rid == pos
        vals.append(m)
        out_ids.append(pos if ids is None else jnp.max(jnp.where(hit, ids, -1.0), axis=0, keepdims=True))
        s = jnp.where(hit, -jnp.inf, s)
    return jnp.concatenate(vals, axis=0), jnp.concatenate(out_ids, axis=0)


def _staircase_candidates(v0, i0, v1, i1):
    k = PEER_TOPK
    sub = lax.broadcasted_iota(I32, (SUBLANE, v0.shape[1]), 0)
    vals = [v0[0:1] + v1]
    ids = [i0[0:1] * float(N_KEYS) + i1]
    for a in range(1, k // 2):
        nb = k // (a + 1)
        live = sub < nb
        vals.append(jnp.where(live, v0[a:a + 1] + v1[:SUBLANE], -jnp.inf))
        ids.append(i0[a:a + 1] * float(N_KEYS) + i1[:SUBLANE])
    vals.append(v0[k // 2:] + v1[0:1])
    ids.append(i0[k // 2:] * float(N_KEYS) + i1[0:1])
    return jnp.concatenate(vals, axis=0), jnp.concatenate(ids, axis=0)


def _peer1_kernel(x_ref, mod_ref, g_ref, wq_ref, sk_ref, idx_ref, pair_ref, gate_ref):
    m = mod_ref[0, 0]
    h = _norm_mod(x_ref[0], g_ref[...], m[3:4], m[4:5]).astype(BF16)
    half = PEER_DK // 2
    chosen = []
    for hd in range(PEER_HEADS):
        tops = []
        for p in range(2):
            col = (2 * hd + p) * half
            qp = jnp.dot(h, wq_ref[:, col:col + half], preferred_element_type=F32).astype(BF16)
            st = lax.dot_general(sk_ref[hd, p], qp, _NT, preferred_element_type=F32)
            tops.append(_topk_rows(st, None, PEER_TOPK))
        (v0, i0), (v1, i1) = tops
        cand, cidx = _staircase_candidates(v0, i0, v1, i1)
        best, experts = _topk_rows(cand, cidx, PEER_TOPK)
        e = jnp.exp(best - best[0:1])
        rs = slice(hd * PEER_TOPK, (hd + 1) * PEER_TOPK)
        gate_ref[0, rs, :] = e / jnp.sum(e, axis=0, keepdims=True)
        idx_ref[0, rs, :] = experts.astype(I32)
        chosen.append(experts)
    pair_ref[0] = lax.shift_right_logical(jnp.concatenate(chosen, axis=0).T.astype(I32), 1) * SUBLANE


def _peer_select(xa, modseg, g, wq, sk, lc):
    b, t, d = xa.shape
    ne = PEER_HEADS * PEER_TOPK
    tm = SELECT_TILE
    n_ctx_tiles = lc // tm
    return pl.pallas_call(
        _peer1_kernel,
        grid=(b, t // tm),
        in_specs=[pl.BlockSpec((1, tm, d), lambda i, j: (i, j, 0)),
                  pl.BlockSpec((1, 1, SUBLANE, d), lambda i, j: (i, (j >= n_ctx_tiles).astype(I32), 0, 0)),
                  pl.BlockSpec((1, d), lambda i, j: (0, 0)),
                  pl.BlockSpec(wq.shape, lambda i, j: (0, 0)),
                  pl.BlockSpec(sk.shape, lambda i, j: (0, 0, 0, 0))],
        out_specs=[pl.BlockSpec((1, ne, tm), lambda i, j: (i, 0, j)),
                   pl.BlockSpec((1, tm, ne), lambda i, j: (i, j, 0)),
                   pl.BlockSpec((1, ne, tm), lambda i, j: (i, 0, j))],
        out_shape=[jax.ShapeDtypeStruct((b, ne, t), I32), jax.ShapeDtypeStruct((b, t, ne), I32),
                   jax.ShapeDtypeStruct((b, ne, t), F32)],
        compiler_params=_cp(("parallel", "arbitrary"), 40),
        name="peer_select",
    )(xa, modseg, g, wq, sk)


PAIR_ROWS = 2 * SUBLANE


def _gather_tiles(tbl_ref, pair_ref, t, ne):
    base = pl.multiple_of(t * ne, ne)
    tiles = []
    for j in range(ne):
        row = pl.multiple_of(pair_ref[base + j], SUBLANE)
        tiles.append(pltpu.bitcast(tbl_ref[pl.ds(row, SUBLANE), :], BF16))
    return jnp.concatenate(tiles, axis=0)


TOKENS_PER_TRIP = 16


def _for_token_groups(tb, body):
    def trip(i, carry):
        for u in range(TOKENS_PER_TRIP):
            body(TOKENS_PER_TRIP * i + u)
        return carry

    lax.fori_loop(0, tb // TOKENS_PER_TRIP, trip, 0)


PRODUCT_GROUPS = 4


def _peer2_kernel(pair_ref, idx_ref, x_ref, mod_ref, g_ref, gate_ref, tbl_ref, cc_ref, accp_ref, acct_ref):
    tb = accp_ref.shape[0]
    ne = idx_ref.shape[1]
    kk = ne * PAIR_ROWS
    grp = 2 * PAIR_ROWS
    n_p = PRODUCT_GROUPS * PAIR_ROWS
    n_t = ne - n_p
    kt = n_t * PAIR_ROWS
    d = x_ref.shape[1] * x_ref.shape[2]
    a = g_ref[0] * (1.0 + mod_ref[4])
    shift = mod_ref[3]
    srow = lax.broadcasted_iota(I32, (grp, PAIR_ROWS * PAIR_ROWS), 0)
    scol = lax.broadcasted_iota(I32, (grp, PAIR_ROWS * PAIR_ROWS), 1)
    sel = (((scol // PAIR_ROWS) == (srow % PAIR_ROWS)) & ((scol % 2) == (srow // PAIR_ROWS))).astype(BF16)
    dlane = lax.broadcasted_iota(I32, (SUBLANE, kt), 1)
    dsub = lax.broadcasted_iota(I32, (SUBLANE, kt), 0)
    diag = ((dlane % PAIR_ROWS) // 2) == dsub

    def token(t):
        x = x_ref[t]
        ms = jnp.sum(x * x, keepdims=True) * (1.0 / d)
        h = x * lax.rsqrt(ms + EPS) * a + shift
        hu = lax.bitcast_convert_type(h.astype(BF16).astype(F32), U32)
        hp = pltpu.bitcast(hu | lax.shift_right_logical(hu, jnp.uint32(16)), BF16)
        hb = jnp.concatenate([h, jnp.zeros_like(h)], axis=0).astype(BF16)
        base = pl.multiple_of(t * ne, ne)

        def tile(j):
            row = pl.multiple_of(pair_ref[base + j], SUBLANE)
            return pltpu.bitcast(tbl_ref[pl.ds(row, SUBLANE), :], BF16)

        parts = []
        for gi in range(PRODUCT_GROUPS):
            prods = [tile(j) * hp for j in range(gi * PAIR_ROWS, (gi + 1) * PAIR_ROWS)]
            parts.append(jnp.dot(sel, jnp.concatenate(prods, axis=0), preferred_element_type=F32))
        r = jnp.concatenate(parts, axis=0)
        accp_ref[pl.ds(t, 1), :] = jnp.sum(r.T, axis=0, keepdims=True)
        tiles = jnp.concatenate([tile(j) for j in range(n_p, ne)], axis=0)
        out = lax.dot_general(hb, tiles, _NT, preferred_element_type=F32)
        acct_ref[pl.ds(t, 1), :] = jnp.sum(jnp.where(diag, out[:SUBLANE], 0.0), axis=0, keepdims=True)

    _for_token_groups(tb, token)

    atp = accp_ref[...].T
    kq = lax.broadcasted_iota(I32, (2 * n_t, kt), 1)
    rq = lax.broadcasted_iota(I32, (2 * n_t, kt), 0)
    selt = (((kq // PAIR_ROWS) == (rq % n_t)) & ((kq % 2) == (rq // n_t))).astype(F32)
    att = lax.dot_general(selt, acct_ref[...], _NT, preferred_element_type=F32, precision=HIGHEST)
    a0 = jnp.concatenate([atp[gi * grp:gi * grp + PAIR_ROWS] for gi in range(PRODUCT_GROUPS)] + [att[:n_t]], axis=0)
    a1 = jnp.concatenate([atp[gi * grp + PAIR_ROWS:(gi + 1) * grp] for gi in range(PRODUCT_GROUPS)] + [att[n_t:]],
                         axis=0)
    par = idx_ref[0] & 1
    act = jnp.where(par == 0, a0, a1)
    coef = jax.nn.gelu(act, approximate=True) * gate_ref[0]
    ke = lax.broadcasted_iota(I32, (ne, kk), 1)
    re = lax.broadcasted_iota(I32, (ne, kk), 0)
    expand = ((ke // PAIR_ROWS) == re).astype(BF16)
    cexp = jnp.dot(coef.T.astype(BF16), expand, preferred_element_type=F32)
    pexp = jnp.dot(par.astype(F32).T.astype(BF16), expand, preferred_element_type=F32)
    lanepar = (lax.broadcasted_iota(I32, (tb, kk), 1) % 2).astype(F32)
    cc_ref[...] = jnp.where(pexp == lanepar, cexp, 0.0)


def _peer3_kernel(pair_ref, cc_ref, x_ref, mod_ref, tbl_ref, o_ref):
    tb, kk = cc_ref.shape
    ne = kk // PAIR_ROWS
    ga = mod_ref[5]
    lane = lax.broadcasted_iota(I32, (PAIR_ROWS, kk), 1)
    sub = lax.broadcasted_iota(I32, (PAIR_ROWS, kk), 0)
    diag = ((lane % PAIR_ROWS) // 2) == sub

    def token(t):
        row = cc_ref[pl.ds(t, 1), :]
        cm = jnp.where(diag, jnp.broadcast_to(row, (PAIR_ROWS, kk)), 0.0).astype(BF16)
        tiles = _gather_tiles(tbl_ref, pair_ref, t, ne)
        o = jnp.dot(cm, tiles, preferred_element_type=F32)
        o_ref[t] = x_ref[t] + ga * o[:SUBLANE]

    _for_token_groups(tb, token)


def _peer_blocks(b, t, lc):
    nj = t // PEER_TOKENS
    n_ctx = lc // PEER_TOKENS
    tokb = pl.BlockSpec((PEER_TOKENS, SUBLANE, LANE), lambda n: (n, 0, 0))
    modb = pl.BlockSpec((SUBLANE, SUBLANE, LANE), lambda n: ((n // nj) * 2 + ((n % nj) >= n_ctx).astype(I32), 0, 0))
    return nj, tokb, modb


def _peer_act(x3, pair2, idx, gate, modv, g3, tbl, b, t, lc):
    ne = PEER_HEADS * PEER_TOPK
    kk = ne * PAIR_ROWS
    nj, tokb, modb = _peer_blocks(b, t, lc)
    expb = pl.BlockSpec((1, ne, PEER_TOKENS), lambda n: (n // nj, 0, n % nj))
    return pl.pallas_call(
        _peer2_kernel,
        grid=(b * nj,),
        in_specs=[pl.BlockSpec((PEER_TOKENS * ne,), lambda n: (n,), memory_space=pltpu.SMEM),
                  expb, tokb, modb,
                  pl.BlockSpec((1, SUBLANE, LANE), lambda n: (0, 0, 0)),
                  expb,
                  pl.BlockSpec(tbl.shape, lambda n: (0, 0), pipeline_mode=pl.Buffered(1))],
        out_specs=pl.BlockSpec((PEER_TOKENS, kk), lambda n: (n, 0)),
        out_shape=jax.ShapeDtypeStruct((b * t, kk), F32),
        scratch_shapes=[pltpu.VMEM((PEER_TOKENS, 2 * PRODUCT_GROUPS * PAIR_ROWS), F32),
                        pltpu.VMEM((PEER_TOKENS, (ne - PRODUCT_GROUPS * PAIR_ROWS) * PAIR_ROWS), F32)],
        compiler_params=_cp(("arbitrary",), 56),
        name="peer_act",
    )(pair2, idx, x3, modv, g3, gate, tbl)


def _peer_out(x3, pair2, cc, modv, tbl, b, t, lc):
    ne = PEER_HEADS * PEER_TOPK
    kk = ne * PAIR_ROWS
    nj, tokb, modb = _peer_blocks(b, t, lc)
    return pl.pallas_call(
        _peer3_kernel,
        grid=(b * nj,),
        in_specs=[pl.BlockSpec((PEER_TOKENS * ne,), lambda n: (n,), memory_space=pltpu.SMEM),
                  pl.BlockSpec((PEER_TOKENS, kk), lambda n: (n, 0)),
                  tokb, modb,
                  pl.BlockSpec(tbl.shape, lambda n: (0, 0), pipeline_mode=pl.Buffered(1))],
        out_specs=tokb,
        out_shape=jax.ShapeDtypeStruct(x3.shape, F32),
        compiler_params=_cp(("arbitrary",), 56),
        name="peer_out",
    )(pair2, cc, x3, modv, tbl)


def _pack_experts(w):
    e, d = w.shape
    assert d == VREG_ELEMS
    bits = lax.bitcast_convert_type(w.astype(BF16), jnp.uint16).astype(U32).reshape(e // 2, 2, SUBLANE, LANE)
    return (bits[:, 0] | (bits[:, 1] << 16)).reshape(e // 2 * SUBLANE, LANE)


def _peer(xa, modseg, modv, g2, w_query, sub_keys, expert_u, expert_v, lc):
    b, t, d = xa.shape
    idx, pair, gate = _peer_select(xa, modseg, g2.reshape(1, d), w_query.astype(BF16), sub_keys.astype(BF16), lc)
    pair2 = pair.reshape(b * t * idx.shape[1])
    x3 = xa.reshape(b * t, SUBLANE, LANE)
    cc = _peer_act(x3, pair2, idx, gate, modv, g2.reshape(1, SUBLANE, LANE), _pack_experts(expert_u), b, t, lc)
    x3 = _peer_out(x3, pair2, cc, modv, _pack_experts(expert_v), b, t, lc)
    return x3.reshape(b, t, d)


def _final_kernel(x_ref, g_ref, o_ref):
    x = x_ref[0]
    ms = jnp.mean(x * x, axis=-1, keepdims=True)
    o_ref[0] = x * lax.rsqrt(ms + EPS) * g_ref[...]


def _final_norm(xa, g, lc):
    b, t, d = xa.shape
    off = lc // ROW_TILE
    return pl.pallas_call(
        _final_kernel,
        grid=(b, (t - lc) // ROW_TILE),
        in_specs=[pl.BlockSpec((1, ROW_TILE, d), lambda i, j: (i, j + off, 0)),
                  pl.BlockSpec((1, d), lambda i, j: (0, 0))],
        out_specs=pl.BlockSpec((1, ROW_TILE, d), lambda i, j: (i, j, 0)),
        out_shape=jax.ShapeDtypeStruct((b, t - lc, d), F32),
        compiler_params=_cp(("parallel", "arbitrary"), 32),
        name="final_norm",
    )(xa, g)


def _rope_tables(l, lc):
    t = jnp.arange(l, dtype=I32)
    quarter = GLA_DK // 4
    freqs = ROPE_BASE ** (-jnp.arange(quarter, dtype=F32) / quarter)
    ang_r = (t // GRID_W).astype(F32)[:, None] * freqs
    ang_c = (t % GRID_W).astype(F32)[:, None] * freqs
    cos = jnp.concatenate([jnp.cos(ang_r)] * 2 + [jnp.cos(ang_c)] * 2, axis=-1)
    sin = jnp.concatenate([-jnp.sin(ang_r), jnp.sin(ang_r), -jnp.sin(ang_c), jnp.sin(ang_c)], axis=-1)
    cos = jnp.tile(cos, (1, GLA_HEADS))
    sin = jnp.tile(sin, (1, GLA_HEADS))
    qs = GLA_DK ** -0.5
    cs = jnp.concatenate([cos * qs, cos], axis=-1)
    sn = jnp.concatenate([sin * qs, sin], axis=-1)
    cs_ctx = jnp.concatenate([jnp.full((lc, GLA_KW), qs, F32), jnp.ones((lc, GLA_KW), F32)], axis=-1)
    return (jnp.concatenate([cs_ctx, cs], axis=0),
            jnp.concatenate([jnp.zeros((lc, 2 * GLA_KW), F32), sn], axis=0))


def _permute_w_in(w, d):
    sizes = (GLA_KW, GLA_KW, GLA_VW, 2 * GATE_RANK, GLA_VW, NA_W, NA_W, NA_W, 2 * d)
    offs = [0]
    for s in sizes:
        offs.append(offs[-1] + s)
    q, k, v, z, r, nq, nk, nv, gates = [w[:, offs[i]:offs[i + 1]] for i in range(len(sizes))]
    zpad = jnp.pad(z, ((0, 0), (0, LANE - 2 * GATE_RANK)))
    return jnp.concatenate([q, k, v, r, nq, nk, nv, gates, zpad], axis=1).astype(BF16)


def _gate_weights(w_a, b_a, first_row):
    wa = jnp.zeros((LANE, GLA_KW), F32).at[first_row:first_row + GATE_RANK].set(w_a)
    return wa, b_a.reshape(1, GLA_KW)


def kernel(x, c, ctx, c_ctx, w_mod, b_mod, g_norm1, w_in, w_alpha_f, b_alpha_f, w_alpha_b, b_alpha_b, g_gla, rpb,
           w_proj_gla, w_proj_na, w_out, g_norm2, w_query, sub_keys, expert_u, expert_v, g_final):
    b, l, d = x.shape
    lc = ctx.shape[1]
    depth = w_mod.shape[0]
    t = lc + l
    assert d == VREG_ELEMS and lc % ROW_TILE == 0 and l % ROW_TILE == 0 and l % GRID_W == 0
    n_ctx_tiles = lc // ROW_TILE

    rows = -(-(b + 1) // SUBLANE) * SUBLANE
    cin = jnp.zeros((rows, d), F32).at[:b].set(c).at[b].set(c_ctx)
    mod_all = _modulation(cin, w_mod, b_mod)

    cs, sn = _rope_tables(l, lc)
    xa = jnp.concatenate([ctx, x], axis=1)

    for i in range(depth):
        mod = mod_all[i].reshape(rows, N_MOD, d)
        modseg = jnp.stack([jnp.broadcast_to(mod[b], (b, N_MOD, d)), mod[:b]], axis=1)
        modseg = jnp.pad(modseg, ((0, 0), (0, 0), (0, SUBLANE - N_MOD), (0, 0)))
        modv = modseg.reshape(b * 2 * SUBLANE, SUBLANE, LANE)

        qk, v, r, nq, nk, nv, gates, z = _project(xa, modseg, g_norm1[i].reshape(1, d),
                                                  _permute_w_in(w_in[i], d), n_ctx_tiles)
        waf, baf = _gate_weights(w_alpha_f[i], b_alpha_f[i], 0)
        wab, bab = _gate_weights(w_alpha_b[i], b_alpha_b[i], GATE_RANK)
        o_f = _gla(qk, v, z, cs, sn, waf, baf, lc // GLA_BLOCK, reverse=False)
        o_b = _gla(qk, v, z, cs, sn, wab, bab, lc // GLA_BLOCK, reverse=True)
        na = _na(nq, nk, nv, _na_bias_tables(rpb[i]), lc)
        xa = _merge(xa, o_f, o_b, r, na, gates, modseg, g_gla[i].reshape(1, GLA_DV),
                    w_proj_gla[i].astype(BF16), w_proj_na[i].astype(BF16), w_out[i].astype(BF16), n_ctx_tiles)

        xa = _peer(xa, modseg, modv, g_norm2[i], w_query[i], sub_keys[i], expert_u[i], expert_v[i], lc)

    return _final_norm(xa, g_final.reshape(1, d), lc)
```

```python
import functools

import jax
import jax.numpy as jnp
from jax import lax
from jax.experimental import pallas as pl
from jax.experimental.pallas import tpu as pltpu

F32 = jnp.float32
BF16 = jnp.bfloat16
I32 = jnp.int32
U32 = jnp.uint32
HIGHEST = lax.Precision.HIGHEST

GRID_W = 64
EPS = 1e-6
N_MOD = 6
GLA_HEADS = 4
GLA_DK = 64
GLA_DV = 128
GLA_KW = GLA_HEADS * GLA_DK
GLA_VW = GLA_HEADS * GLA_DV
GATE_RANK = 16
GATE_TAU = 16.0
CHUNK = 64
ROPE_BASE = 10000.0
NA_HEADS = 8
NA_DH = 64
NA_W = NA_HEADS * NA_DH
NA_KH = 8
NA_KW = 16
PEER_HEADS = 8
N_KEYS = 128
PEER_DK = 256
PEER_TOPK = 16

LANE = 128
SUBLANE = 8
VREG_ELEMS = LANE * SUBLANE

ROW_TILE = 256
GLA_BLOCK = 256
SELECT_TILE = 256
PEER_TOKENS = 128
MASK_NEG = -1e30

_NT = (((1,), (1,)), ((), ()))


def _cp(sem, vmem_mb):
    return pltpu.CompilerParams(dimension_semantics=sem, vmem_limit_bytes=vmem_mb * 1024 * 1024)


def _norm_mod(x, g, shift, scale):
    ms = jnp.mean(x * x, axis=-1, keepdims=True)
    return (x * lax.rsqrt(ms + EPS) * g) * (1.0 + scale) + shift


def _mod_kernel(c_ref, w_ref, b_ref, o_ref):
    a = c_ref[...]
    a = a * jax.nn.sigmoid(a)
    o_ref[0] = jnp.dot(a, w_ref[0], preferred_element_type=F32, precision=HIGHEST) + b_ref[0]


def _modulation(cin, w_mod, b_mod):
    depth, d, n = w_mod.shape
    rows = cin.shape[0]
    tn = n // 4
    return pl.pallas_call(
        _mod_kernel,
        grid=(depth, n // tn),
        in_specs=[pl.BlockSpec((rows, d), lambda l, j: (0, 0)),
                  pl.BlockSpec((1, d, tn), lambda l, j: (l, 0, j)),
                  pl.BlockSpec((1, 1, tn), lambda l, j: (l, 0, j))],
        out_specs=pl.BlockSpec((1, rows, tn), lambda l, j: (l, 0, j)),
        out_shape=jax.ShapeDtypeStruct((depth, rows, n), F32),
        compiler_params=_cp(("arbitrary", "arbitrary"), 40),
        name="modulation",
    )(cin, w_mod, b_mod.reshape(depth, 1, n))


_PROJ_OUTS = (("qk", 2 * GLA_KW, F32), ("v", GLA_VW, F32), ("r", GLA_VW, F32),
              ("nq", NA_W, BF16), ("nk", NA_W, BF16), ("nv", NA_W, BF16),
              ("gates", None, F32), ("z", LANE, F32))


def _proj_kernel(x_ref, mod_ref, g_ref, w_ref, *out_refs):
    m = mod_ref[0, 0]
    h = _norm_mod(x_ref[0], g_ref[...], m[0:1], m[1:2]).astype(BF16)
    col = 0
    for ref in out_refs:
        n = ref.shape[-1]
        ref[0] = jnp.dot(h, w_ref[:, col:col + n], preferred_element_type=F32).astype(ref.dtype)
        col += n


def _project(xa, modseg, g, w_perm, n_ctx_tiles):
    b, t, d = xa.shape
    outs = [(name, (2 * d if n is None else n), dt) for name, n, dt in _PROJ_OUTS]
    assert sum(n for _, n, _ in outs) == w_perm.shape[1]
    return pl.pallas_call(
        _proj_kernel,
        grid=(b, t // ROW_TILE),
        in_specs=[pl.BlockSpec((1, ROW_TILE, d), lambda i, j: (i, j, 0)),
                  pl.BlockSpec((1, 1, SUBLANE, d), lambda i, j: (i, (j >= n_ctx_tiles).astype(I32), 0, 0)),
                  pl.BlockSpec((1, d), lambda i, j: (0, 0)),
                  pl.BlockSpec(w_perm.shape, lambda i, j: (0, 0))],
        out_specs=[pl.BlockSpec((1, ROW_TILE, n), lambda i, j: (i, j, 0)) for _, n, _ in outs],
        out_shape=[jax.ShapeDtypeStruct((b, t, n), dt) for _, n, dt in outs],
        compiler_params=_cp(("parallel", "arbitrary"), 48),
        name="in_proj",
    )(xa, modseg, g, w_perm)


def _gla_kernel(qk_ref, v_ref, z_ref, cs_ref, sn_ref, wa_ref, ba_ref, o_ref, st_ref, *, reverse):
    @pl.when(pl.program_id(1) == 0)
    def _init():
        st_ref[...] = jnp.zeros_like(st_ref)

    kw2 = 2 * GLA_KW
    blk = qk_ref.shape[1]
    qk = qk_ref[0]
    lane = lax.broadcasted_iota(I32, (1, kw2), 1)
    first = (lane % 32) < 16
    partner = jnp.where(first, pltpu.roll(qk, kw2 - 16, 1), pltpu.roll(qk, 16, 1))
    qk = qk * cs_ref[...] + partner * sn_ref[...]
    zz = jnp.dot(z_ref[0], wa_ref[...], preferred_element_type=F32) + ba_ref[...]
    la = (jnp.minimum(zz, 0.0) - jnp.log1p(jnp.exp(-jnp.abs(zz)))) * (1.0 / GATE_TAU)
    la_hi = la.astype(BF16)
    rem = la - la_hi.astype(F32)
    la_mid = rem.astype(BF16)
    la_lo = (rem - la_mid.astype(F32)).astype(BF16)

    ri = lax.broadcasted_iota(I32, (CHUNK, CHUNK), 0)
    ci = lax.broadcasted_iota(I32, (CHUNK, CHUNK), 1)
    keep = (ci >= ri) if reverse else (ci <= ri)
    tri3 = jnp.concatenate([keep.astype(BF16)] * 3, axis=1)
    khead = lax.broadcasted_iota(I32, (1, GLA_KW), 1) // GLA_DK
    vrow = lax.broadcasted_iota(I32, (GLA_VW, GLA_KW), 0) // GLA_DV
    kcol = lax.broadcasted_iota(I32, (GLA_VW, GLA_KW), 1) // GLA_DK
    same_head = vrow == kcol

    order = range(blk // CHUNK)
    if reverse:
        order = reversed(order)
    for c in order:
        lo = c * CHUNK
        q = qk[lo:lo + CHUNK, :GLA_KW]
        k = qk[lo:lo + CHUNK, GLA_KW:]
        v = v_ref[0, lo:lo + CHUNK, :].astype(BF16)
        pieces = jnp.concatenate([p[lo:lo + CHUNK] for p in (la_hi, la_mid, la_lo)], axis=0)
        bcum = jnp.dot(tri3, pieces, preferred_element_type=F32)
        bend = bcum[0:1] if reverse else bcum[CHUNK - 1:CHUNK]
        qd = q * jnp.exp(bcum)
        kd = (k * jnp.exp(-bcum)).astype(BF16)
        kd2 = (k * jnp.exp(bend - bcum)).astype(BF16)
        st = st_ref[...]
        o_inter = lax.dot_general(qd.astype(BF16), st.astype(BF16), _NT, preferred_element_type=F32)
        for h in range(GLA_HEADS):
            qm = jnp.where(khead == h, qd, 0.0).astype(BF16)
            s = lax.dot_general(qm, kd, _NT, preferred_element_type=F32)
            s = jnp.where(keep, s, 0.0).astype(BF16)
            vs = slice(h * GLA_DV, (h + 1) * GLA_DV)
            o_ref[0, lo:lo + CHUNK, vs] = (jnp.dot(s, v[:, vs], preferred_element_type=F32)
                                           + o_inter[:, vs])
        upd = jnp.dot(v.T, kd2, preferred_element_type=F32)
        st_ref[...] = st * jnp.exp(bend) + jnp.where(same_head, upd, 0.0)


def _gla(qk, v, z, cs, sn, wa, ba, n_ctx_blocks, reverse):
    b, t, _ = qk.shape
    nblk = t // GLA_BLOCK

    if reverse:
        def blk(j):
            return jnp.where(j < n_ctx_blocks, n_ctx_blocks - 1 - j, nblk - 1 - (j - n_ctx_blocks))
    else:
        def blk(j):
            return j

    tok = lambda n: pl.BlockSpec((1, GLA_BLOCK, n), lambda i, j: (i, blk(j), 0))
    tab = lambda n: pl.BlockSpec((GLA_BLOCK, n), lambda i, j: (blk(j), 0))
    return pl.pallas_call(
        functools.partial(_gla_kernel, reverse=reverse),
        grid=(b, nblk),
        in_specs=[tok(2 * GLA_KW), tok(GLA_VW), tok(LANE), tab(2 * GLA_KW), tab(2 * GLA_KW),
                  pl.BlockSpec((LANE, GLA_KW), lambda i, j: (0, 0)),
                  pl.BlockSpec((1, GLA_KW), lambda i, j: (0, 0))],
        out_specs=tok(GLA_VW),
        out_shape=jax.ShapeDtypeStruct((b, t, GLA_VW), F32),
        scratch_shapes=[pltpu.VMEM((GLA_VW, GLA_KW), F32)],
        compiler_params=_cp(("parallel", "arbitrary"), 40),
        name="gla_bwd" if reverse else "gla_fwd",
    )(qk, v, z, cs, sn, wa, ba)


def _na_kernel(*refs):
    q_ref = refs[0]
    k_refs = refs[1:1 + NA_KH]
    v_refs = refs[1 + NA_KH:1 + 2 * NA_KH]
    kc_ref, vc_ref, tb_ref, o_ref, s_ref, p_ref = refs[1 + 2 * NA_KH:]
    scale = NA_DH ** -0.5
    nw = NA_KH * GRID_W
    half = lax.broadcasted_iota(I32, (1, LANE), 1) // NA_DH
    for p in range(NA_HEADS // 2):
        ls = slice(p * LANE, (p + 1) * LANE)
        qp = q_ref[0, :, ls]
        kw = jnp.concatenate([r[0, :, ls] for r in k_refs], axis=0)
        kc = kc_ref[0, :, ls]
        for s in range(2):
            qm = jnp.where(half == s, qp, jnp.zeros_like(qp))
            s_ref[2 * p + s, :, :nw] = (lax.dot_general(qm, kw, _NT, preferred_element_type=F32) * scale
                                        + tb_ref[0, 2 * p + s])
            s_ref[2 * p + s, :, nw:] = lax.dot_general(qm, kc, _NT, preferred_element_type=F32) * scale
    for h in range(NA_HEADS):
        sc = s_ref[h]
        e = jnp.exp(sc - jnp.max(sc, axis=-1, keepdims=True))
        p_ref[h] = (e * (1.0 / jnp.sum(e, axis=-1, keepdims=True))).astype(BF16)
    for p in range(NA_HEADS // 2):
        ls = slice(p * LANE, (p + 1) * LANE)
        vw = jnp.concatenate([r[0, :, ls] for r in v_refs], axis=0)
        vc = vc_ref[0, :, ls]
        outs = [jnp.dot(p_ref[2 * p + s, :, :nw], vw, preferred_element_type=F32)
                + jnp.dot(p_ref[2 * p + s, :, nw:], vc, preferred_element_type=F32) for s in range(2)]
        o_ref[0, :, ls] = jnp.where(half == 0, outs[0], outs[1]).astype(o_ref.dtype)


def _na_bias_tables(rpb):
    w = jnp.arange(GRID_W, dtype=I32)
    c0 = jnp.clip(w - NA_KW // 2, 0, GRID_W - NA_KW)
    cp = w[None, :]
    valid = (cp >= c0[:, None]) & (cp < c0[:, None] + NA_KW)
    off = jnp.clip(cp - w[:, None] + (NA_KW - 1), 0, 2 * NA_KW - 2)
    t = jnp.where(valid[None, None], rpb[:, :, off], MASK_NEG)
    tabs = [jnp.concatenate([t[:, s + a] for a in range(NA_KH)], axis=-1) for s in range(NA_KH)]
    tabs.append(jnp.full_like(tabs[0], MASK_NEG))
    return jnp.stack(tabs, axis=0).astype(F32)


def _na(nq, nk, nv, tables, lc):
    b, t, _ = nq.shape
    n_ctx = lc // GRID_W
    rows = (t - lc) // GRID_W
    assert rows >= NA_KH

    def row0(j):
        r = jnp.maximum(j - n_ctx, 0)
        return r, jnp.clip(r - NA_KH // 2, 0, rows - NA_KH)

    def win(a):
        return pl.BlockSpec((1, GRID_W, NA_W), lambda i, j: (i, n_ctx + row0(j)[1] + a, 0))

    def tab_idx(i, j):
        r, r0 = row0(j)
        return (jnp.where(j < n_ctx, NA_KH, r0 - r + NA_KH - 1), 0, 0, 0)

    return pl.pallas_call(
        _na_kernel,
        grid=(b, t // GRID_W),
        in_specs=([pl.BlockSpec((1, GRID_W, NA_W), lambda i, j: (i, j, 0))]
                  + [win(a) for a in range(NA_KH)] + [win(a) for a in range(NA_KH)]
                  + [pl.BlockSpec((1, lc, NA_W), lambda i, j: (i, 0, 0))] * 2
                  + [pl.BlockSpec((1, NA_HEADS, GRID_W, NA_KH * GRID_W), tab_idx)]),
        out_specs=pl.BlockSpec((1, GRID_W, NA_W), lambda i, j: (i, j, 0)),
        out_shape=jax.ShapeDtypeStruct((b, t, NA_W), BF16),
        scratch_shapes=[pltpu.VMEM((NA_HEADS, GRID_W, NA_KH * GRID_W + lc), F32),
                        pltpu.VMEM((NA_HEADS, GRID_W, NA_KH * GRID_W + lc), BF16)],
        compiler_params=_cp(("parallel", "arbitrary"), 40),
        name="natten",
    )(nq, *([nk] * NA_KH), *([nv] * NA_KH), nk, nv, tables)


def _merge_kernel(x_ref, of_ref, ob_ref, r_ref, na_ref, gt_ref, mod_ref, gg_ref, wpg_ref, wpn_ref, wo_ref,
                  o_ref):
    d = x_ref.shape[-1]
    o = of_ref[0] + ob_ref[0]
    r = r_ref[0]
    parts = []
    for h in range(GLA_HEADS):
        oh = o[:, h * GLA_DV:(h + 1) * GLA_DV]
        ms = jnp.mean(oh * oh, axis=-1, keepdims=True)
        parts.append(oh * lax.rsqrt(ms + EPS) * gg_ref[...])
    on = jnp.concatenate(parts, axis=-1) * (r * jax.nn.sigmoid(r))
    ya = jnp.dot(on.astype(BF16), wpg_ref[...], preferred_element_type=F32)
    yb = jnp.dot(na_ref[0], wpn_ref[...], preferred_element_type=F32)
    gt = gt_ref[0]
    y = jax.nn.sigmoid(gt[:, :d]) * ya + jax.nn.sigmoid(gt[:, d:]) * yb
    y = jnp.dot(y.astype(BF16), wo_ref[...], preferred_element_type=F32)
    o_ref[0] = x_ref[0] + mod_ref[0, 0][2:3] * y


def _merge(xa, o_f, o_b, r, na, gates, modseg, g_gla, wpg, wpn, wo, n_ctx_tiles):
    b, t, d = xa.shape
    tok = lambda n: pl.BlockSpec((1, ROW_TILE, n), lambda i, j: (i, j, 0))
    full = lambda a: pl.BlockSpec(a.shape, lambda i, j: (0,) * a.ndim)
    return pl.pallas_call(
        _merge_kernel,
        grid=(b, t // ROW_TILE),
        in_specs=[tok(d), tok(GLA_VW), tok(GLA_VW), tok(GLA_VW), tok(NA_W), tok(2 * d),
                  pl.BlockSpec((1, 1, SUBLANE, d), lambda i, j: (i, (j >= n_ctx_tiles).astype(I32), 0, 0)),
                  full(g_gla), full(wpg), full(wpn), full(wo)],
        out_specs=tok(d),
        out_shape=jax.ShapeDtypeStruct((b, t, d), F32),
        compiler_params=_cp(("parallel", "arbitrary"), 40),
        name="merge",
    )(xa, o_f, o_b, r, na, gates, modseg, g_gla, wpg, wpn, wo)


def _topk_rows(s, ids, k):
    rows = s.shape[0]
    rid = lax.broadcasted_iota(I32, s.shape, 0).astype(F32)
    vals, out_ids = [], []
    for _ in range(k):
        m = jnp.max(s, axis=0, keepdims=True)
        pos = jnp.min(jnp.where(s == m, rid, float(rows)), axis=0, keepdims=True)
        hit = rid == pos
        vals.append(m)
        out_ids.append(pos if ids is None else jnp.max(jnp.where(hit, ids, -1.0), axis=0, keepdims=True))
        s = jnp.where(hit, -jnp.inf, s)
    return jnp.concatenate(vals, axis=0), jnp.concatenate(out_ids, axis=0)


def _staircase_candidates(v0, i0, v1, i1):
    k = PEER_TOPK
    sub = lax.broadcasted_iota(I32, (SUBLANE, v0.shape[1]), 0)
    vals = [v0[0:1] + v1]
    ids = [i0[0:1] * float(N_KEYS) + i1]
    for a in range(1, k // 2):
        nb = k // (a + 1)
        live = sub < nb
        vals.append(jnp.where(live, v0[a:a + 1] + v1[:SUBLANE], -jnp.inf))
        ids.append(i0[a:a + 1] * float(N_KEYS) + i1[:SUBLANE])
    vals.append(v0[k // 2:] + v1[0:1])
    ids.append(i0[k // 2:] * float(N_KEYS) + i1[0:1])
    return jnp.concatenate(vals, axis=0), jnp.concatenate(ids, axis=0)


def _peer1_kernel(x_ref, mod_ref, g_ref, wq_ref, sk_ref, idx_ref, pair_ref, gate_ref):
    m = mod_ref[0, 0]
    h = _norm_mod(x_ref[0], g_ref[...], m[3:4], m[4:5]).astype(BF16)
    half = PEER_DK // 2
    chosen = []
    for hd in range(PEER_HEADS):
        tops = []
        for p in range(2):
            col = (2 * hd + p) * half
            qp = jnp.dot(h, wq_ref[:, col:col + half], preferred_element_type=F32).astype(BF16)
            st = lax.dot_general(sk_ref[hd, p], qp, _NT, preferred_element_type=F32)
            tops.append(_topk_rows(st, None, PEER_TOPK))
        (v0, i0), (v1, i1) = tops
        cand, cidx = _staircase_candidates(v0, i0, v1, i1)
        best, experts = _topk_rows(cand, cidx, PEER_TOPK)
        e = jnp.exp(best - best[0:1])
        rs = slice(hd * PEER_TOPK, (hd + 1) * PEER_TOPK)
        gate_ref[0, rs, :] = e / jnp.sum(e, axis=0, keepdims=True)
        idx_ref[0, rs, :] = experts.astype(I32)
        chosen.append(experts)
    pair_ref[0] = lax.shift_right_logical(jnp.concatenate(chosen, axis=0).T.astype(I32), 1) * SUBLANE


def _peer_select(xa, modseg, g, wq, sk, lc):
    b, t, d = xa.shape
    ne = PEER_HEADS * PEER_TOPK
    tm = SELECT_TILE
    n_ctx_tiles = lc // tm
    return pl.pallas_call(
        _peer1_kernel,
        grid=(b, t // tm),
        in_specs=[pl.BlockSpec((1, tm, d), lambda i, j: (i, j, 0)),
                  pl.BlockSpec((1, 1, SUBLANE, d), lambda i, j: (i, (j >= n_ctx_tiles).astype(I32), 0, 0)),
                  pl.BlockSpec((1, d), lambda i, j: (0, 0)),
                  pl.BlockSpec(wq.shape, lambda i, j: (0, 0)),
                  pl.BlockSpec(sk.shape, lambda i, j: (0, 0, 0, 0))],
        out_specs=[pl.BlockSpec((1, ne, tm), lambda i, j: (i, 0, j)),
                   pl.BlockSpec((1, tm, ne), lambda i, j: (i, j, 0)),
                   pl.BlockSpec((1, ne, tm), lambda i, j: (i, 0, j))],
        out_shape=[jax.ShapeDtypeStruct((b, ne, t), I32), jax.ShapeDtypeStruct((b, t, ne), I32),
                   jax.ShapeDtypeStruct((b, ne, t), F32)],
        compiler_params=_cp(("parallel", "arbitrary"), 40),
        name="peer_select",
    )(xa, modseg, g, wq, sk)


PAIR_ROWS = 2 * SUBLANE


def _gather_tiles(tbl_ref, pair_ref, t, ne):
    base = pl.multiple_of(t * ne, ne)
    tiles = []
    for j in range(ne):
        row = pl.multiple_of(pair_ref[base + j], SUBLANE)
        tiles.append(pltpu.bitcast(tbl_ref[pl.ds(row, SUBLANE), :], BF16))
    return jnp.concatenate(tiles, axis=0)


TOKENS_PER_TRIP = 16


def _for_token_groups(tb, body):
    def trip(i, carry):
        for u in range(TOKENS_PER_TRIP):
            body(TOKENS_PER_TRIP * i + u)
        return carry

    lax.fori_loop(0, tb // TOKENS_PER_TRIP, trip, 0)


PRODUCT_GROUPS = 4


def _peer2_kernel(pair_ref, idx_ref, x_ref, mod_ref, g_ref, gate_ref, tbl_ref, cc_ref, accp_ref, acct_ref):
    tb = accp_ref.shape[0]
    ne = idx_ref.shape[1]
    kk = ne * PAIR_ROWS
    grp = 2 * PAIR_ROWS
    n_p = PRODUCT_GROUPS * PAIR_ROWS
    n_t = ne - n_p
    kt = n_t * PAIR_ROWS
    d = x_ref.shape[1]
    a = g_ref[0] * (1.0 + mod_ref[4])
    shift = mod_ref[3]
    srow = lax.broadcasted_iota(I32, (grp, PAIR_ROWS * PAIR_ROWS), 0)
    scol = lax.broadcasted_iota(I32, (grp, PAIR_ROWS * PAIR_ROWS), 1)
    sel = (((scol // PAIR_ROWS) == (srow % PAIR_ROWS)) & ((scol % 2) == (srow // PAIR_ROWS))).astype(BF16)
    dlane = lax.broadcasted_iota(I32, (SUBLANE, kt), 1)
    dsub = lax.broadcasted_iota(I32, (SUBLANE, kt), 0)
    diag = ((dlane % PAIR_ROWS) // 2) == dsub

    def token(t):
        x = x_ref[pl.ds(t, 1), :].reshape(SUBLANE, LANE)
        ms = jnp.sum(x * x, keepdims=True) * (1.0 / d)
        h = x * lax.rsqrt(ms + EPS) * a + shift
        hu = lax.bitcast_convert_type(h.astype(BF16).astype(F32), U32)
        hp = pltpu.bitcast(hu | lax.shift_right_logical(hu, jnp.uint32(16)), BF16)
        hb = jnp.concatenate([h, jnp.zeros_like(h)], axis=0).astype(BF16)
        base = pl.multiple_of(t * ne, ne)

        def tile(j):
            row = pl.multiple_of(pair_ref[base + j], SUBLANE)
            return pltpu.bitcast(tbl_ref[pl.ds(row, SUBLANE), :], BF16)

        parts = []
        for gi in range(PRODUCT_GROUPS):
            prods = [tile(j) * hp for j in range(gi * PAIR_ROWS, (gi + 1) * PAIR_ROWS)]
            parts.append(jnp.dot(sel, jnp.concatenate(prods, axis=0), preferred_element_type=F32))
        r = jnp.concatenate(parts, axis=0)
        accp_ref[pl.ds(t, 1), :] = jnp.sum(r.T, axis=0, keepdims=True)
        tiles = jnp.concatenate([tile(j) for j in range(n_p, ne)], axis=0)
        out = lax.dot_general(hb, tiles, _NT, preferred_element_type=F32)
        acct_ref[pl.ds(t, 1), :] = jnp.sum(jnp.where(diag, out[:SUBLANE], 0.0), axis=0, keepdims=True)

    _for_token_groups(tb, token)

    atp = accp_ref[...].T
    kq = lax.broadcasted_iota(I32, (2 * n_t, kt), 1)
    rq = lax.broadcasted_iota(I32, (2 * n_t, kt), 0)
    selt = (((kq // PAIR_ROWS) == (rq % n_t)) & ((kq % 2) == (rq // n_t))).astype(F32)
    att = lax.dot_general(selt, acct_ref[...], _NT, preferred_element_type=F32, precision=HIGHEST)
    a0 = jnp.concatenate([atp[gi * grp:gi * grp + PAIR_ROWS] for gi in range(PRODUCT_GROUPS)] + [att[:n_t]], axis=0)
    a1 = jnp.concatenate([atp[gi * grp + PAIR_ROWS:(gi + 1) * grp] for gi in range(PRODUCT_GROUPS)] + [att[n_t:]],
                         axis=0)
    par = idx_ref[0] & 1
    act = jnp.where(par == 0, a0, a1)
    coef = jax.nn.gelu(act, approximate=True) * gate_ref[0]
    ke = lax.broadcasted_iota(I32, (ne, kk), 1)
    re = lax.broadcasted_iota(I32, (ne, kk), 0)
    expand = ((ke // PAIR_ROWS) == re).astype(BF16)
    cexp = jnp.dot(coef.T.astype(BF16), expand, preferred_element_type=F32)
    pexp = jnp.dot(par.astype(F32).T.astype(BF16), expand, preferred_element_type=F32)
    lanepar = (lax.broadcasted_iota(I32, (tb, kk), 1) % 2).astype(F32)
    cc_ref[...] = jnp.where(pexp == lanepar, cexp, 0.0)


def _peer3_kernel(pair_ref, cc_ref, x_ref, mod_ref, tbl_ref, o_ref):
    tb, kk = cc_ref.shape
    ne = kk // PAIR_ROWS
    ga = mod_ref[5]
    lane = lax.broadcasted_iota(I32, (PAIR_ROWS, kk), 1)
    sub = lax.broadcasted_iota(I32, (PAIR_ROWS, kk), 0)
    diag = ((lane % PAIR_ROWS) // 2) == sub

    def token(t):
        row = cc_ref[pl.ds(t, 1), :]
        cm = jnp.where(diag, jnp.broadcast_to(row, (PAIR_ROWS, kk)), 0.0).astype(BF16)
        tiles = _gather_tiles(tbl_ref, pair_ref, t, ne)
        o = jnp.dot(cm, tiles, preferred_element_type=F32)
        xrow = x_ref[pl.ds(t, 1), :].reshape(SUBLANE, LANE)
        o_ref[pl.ds(t, 1), :] = (xrow + ga * o[:SUBLANE]).reshape(1, SUBLANE * LANE)

    _for_token_groups(tb, token)


def _peer_blocks(b, t, lc):
    nj = t // PEER_TOKENS
    n_ctx = lc // PEER_TOKENS
    tokb = pl.BlockSpec((PEER_TOKENS, SUBLANE * LANE), lambda n: (n, 0))
    modb = pl.BlockSpec((SUBLANE, SUBLANE, LANE), lambda n: ((n // nj) * 2 + ((n % nj) >= n_ctx).astype(I32), 0, 0))
    return nj, tokb, modb


def _peer_act(x2, pair2, idx, gate, modv, g3, tbl, b, t, lc):
    ne = PEER_HEADS * PEER_TOPK
    kk = ne * PAIR_ROWS
    nj, tokb, modb = _peer_blocks(b, t, lc)
    expb = pl.BlockSpec((1, ne, PEER_TOKENS), lambda n: (n // nj, 0, n % nj))
    return pl.pallas_call(
        _peer2_kernel,
        grid=(b * nj,),
        in_specs=[pl.BlockSpec((PEER_TOKENS * ne,), lambda n: (n,), memory_space=pltpu.SMEM),
                  expb, tokb, modb,
                  pl.BlockSpec((1, SUBLANE, LANE), lambda n: (0, 0, 0)),
                  expb,
                  pl.BlockSpec(tbl.shape, lambda n: (0, 0), pipeline_mode=pl.Buffered(1))],
        out_specs=pl.BlockSpec((PEER_TOKENS, kk), lambda n: (n, 0)),
        out_shape=jax.ShapeDtypeStruct((b * t, kk), F32),
        scratch_shapes=[pltpu.VMEM((PEER_TOKENS, 2 * PRODUCT_GROUPS * PAIR_ROWS), F32),
                        pltpu.VMEM((PEER_TOKENS, (ne - PRODUCT_GROUPS * PAIR_ROWS) * PAIR_ROWS), F32)],
        compiler_params=_cp(("arbitrary",), 56),
        name="peer_act",
    )(pair2, idx, x2, modv, g3, gate, tbl)


def _peer_out(x2, pair2, cc, modv, tbl, b, t, lc):
    ne = PEER_HEADS * PEER_TOPK
    kk = ne * PAIR_ROWS
    nj, tokb, modb = _peer_blocks(b, t, lc)
    return pl.pallas_call(
        _peer3_kernel,
        grid=(b * nj,),
        in_specs=[pl.BlockSpec((PEER_TOKENS * ne,), lambda n: (n,), memory_space=pltpu.SMEM),
                  pl.BlockSpec((PEER_TOKENS, kk), lambda n: (n, 0)),
                  tokb, modb,
                  pl.BlockSpec(tbl.shape, lambda n: (0, 0), pipeline_mode=pl.Buffered(1))],
        out_specs=tokb,
        out_shape=jax.ShapeDtypeStruct(x2.shape, F32),
        compiler_params=_cp(("arbitrary",), 56),
        name="peer_out",
    )(pair2, cc, x2, modv, tbl)


def _pack_experts(w):
    e, d = w.shape
    assert d == VREG_ELEMS
    bits = lax.bitcast_convert_type(w.astype(BF16), jnp.uint16).astype(U32).reshape(e // 2, 2, SUBLANE, LANE)
    return (bits[:, 0] | (bits[:, 1] << 16)).reshape(e // 2 * SUBLANE, LANE)


def _peer(xa, modseg, modv, g2, w_query, sub_keys, expert_u, expert_v, lc):
    b, t, d = xa.shape
    idx, pair, gate = _peer_select(xa, modseg, g2.reshape(1, d), w_query.astype(BF16), sub_keys.astype(BF16), lc)
    pair2 = pair.reshape(b * t * idx.shape[1])
    x2 = xa.reshape(b * t, d)
    cc = _peer_act(x2, pair2, idx, gate, modv, g2.reshape(1, SUBLANE, LANE), _pack_experts(expert_u), b, t, lc)
    x2 = _peer_out(x2, pair2, cc, modv, _pack_experts(expert_v), b, t, lc)
    return x2.reshape(b, t, d)


def _final_kernel(x_ref, g_ref, o_ref):
    x = x_ref[0]
    ms = jnp.mean(x * x, axis=-1, keepdims=True)
    o_ref[0] = x * lax.rsqrt(ms + EPS) * g_ref[...]


def _final_norm(xa, g, lc):
    b, t, d = xa.shape
    off = lc // ROW_TILE
    return pl.pallas_call(
        _final_kernel,
        grid=(b, (t - lc) // ROW_TILE),
        in_specs=[pl.BlockSpec((1, ROW_TILE, d), lambda i, j: (i, j + off, 0)),
                  pl.BlockSpec((1, d), lambda i, j: (0, 0))],
        out_specs=pl.BlockSpec((1, ROW_TILE, d), lambda i, j: (i, j, 0)),
        out_shape=jax.ShapeDtypeStruct((b, t - lc, d), F32),
        compiler_params=_cp(("parallel", "arbitrary"), 32),
        name="final_norm",
    )(xa, g)


def _rope_tables(l, lc):
    t = jnp.arange(l, dtype=I32)
    quarter = GLA_DK // 4
    freqs = ROPE_BASE ** (-jnp.arange(quarter, dtype=F32) / quarter)
    ang_r = (t // GRID_W).astype(F32)[:, None] * freqs
    ang_c = (t % GRID_W).astype(F32)[:, None] * freqs
    cos = jnp.concatenate([jnp.cos(ang_r)] * 2 + [jnp.cos(ang_c)] * 2, axis=-1)
    sin = jnp.concatenate([-jnp.sin(ang_r), jnp.sin(ang_r), -jnp.sin(ang_c), jnp.sin(ang_c)], axis=-1)
    cos = jnp.tile(cos, (1, GLA_HEADS))
    sin = jnp.tile(sin, (1, GLA_HEADS))
    qs = GLA_DK ** -0.5
    cs = jnp.concatenate([cos * qs, cos], axis=-1)
    sn = jnp.concatenate([sin * qs, sin], axis=-1)
    cs_ctx = jnp.concatenate([jnp.full((lc, GLA_KW), qs, F32), jnp.ones((lc, GLA_KW), F32)], axis=-1)
    return (jnp.concatenate([cs_ctx, cs], axis=0),
            jnp.concatenate([jnp.zeros((lc, 2 * GLA_KW), F32), sn], axis=0))


def _permute_w_in(w, d):
    sizes = (GLA_KW, GLA_KW, GLA_VW, 2 * GATE_RANK, GLA_VW, NA_W, NA_W, NA_W, 2 * d)
    offs = [0]
    for s in sizes:
        offs.append(offs[-1] + s)
    q, k, v, z, r, nq, nk, nv, gates = [w[:, offs[i]:offs[i + 1]] for i in range(len(sizes))]
    zpad = jnp.pad(z, ((0, 0), (0, LANE - 2 * GATE_RANK)))
    return jnp.concatenate([q, k, v, r, nq, nk, nv, gates, zpad], axis=1).astype(BF16)


def _gate_weights(w_a, b_a, first_row):
    wa = jnp.zeros((LANE, GLA_KW), F32).at[first_row:first_row + GATE_RANK].set(w_a)
    return wa, b_a.reshape(1, GLA_KW)


def kernel(x, c, ctx, c_ctx, w_mod, b_mod, g_norm1, w_in, w_alpha_f, b_alpha_f, w_alpha_b, b_alpha_b, g_gla, rpb,
           w_proj_gla, w_proj_na, w_out, g_norm2, w_query, sub_keys, expert_u, expert_v, g_final):
    b, l, d = x.shape
    lc = ctx.shape[1]
    depth = w_mod.shape[0]
    t = lc + l
    assert d == VREG_ELEMS and lc % ROW_TILE == 0 and l % ROW_TILE == 0 and l % GRID_W == 0
    n_ctx_tiles = lc // ROW_TILE

    rows = -(-(b + 1) // SUBLANE) * SUBLANE
    cin = jnp.zeros((rows, d), F32).at[:b].set(c).at[b].set(c_ctx)
    mod_all = _modulation(cin, w_mod, b_mod)

    cs, sn = _rope_tables(l, lc)
    xa = jnp.concatenate([ctx, x], axis=1)

    for i in range(depth):
        mod = mod_all[i].reshape(rows, N_MOD, d)
        modseg = jnp.stack([jnp.broadcast_to(mod[b], (b, N_MOD, d)), mod[:b]], axis=1)
        modseg = jnp.pad(modseg, ((0, 0), (0, 0), (0, SUBLANE - N_MOD), (0, 0)))
        modv = modseg.reshape(b * 2 * SUBLANE, SUBLANE, LANE)

        qk, v, r, nq, nk, nv, gates, z = _project(xa, modseg, g_norm1[i].reshape(1, d),
                                                  _permute_w_in(w_in[i], d), n_ctx_tiles)
        waf, baf = _gate_weights(w_alpha_f[i], b_alpha_f[i], 0)
        wab, bab = _gate_weights(w_alpha_b[i], b_alpha_b[i], GATE_RANK)
        o_f = _gla(qk, v, z, cs, sn, waf, baf, lc // GLA_BLOCK, reverse=False)
        o_b = _gla(qk, v, z, cs, sn, wab, bab, lc // GLA_BLOCK, reverse=True)
        na = _na(nq, nk, nv, _na_bias_tables(rpb[i]), lc)
        xa = _merge(xa, o_f, o_b, r, na, gates, modseg, g_gla[i].reshape(1, GLA_DV),
                    w_proj_gla[i].astype(BF16), w_proj_na[i].astype(BF16), w_out[i].astype(BF16), n_ctx_tiles)

        xa = _peer(xa, modseg, modv, g_norm2[i], w_query[i], sub_keys[i], expert_u[i], expert_v[i], lc)

    return _final_norm(xa, g_final.reshape(1, d), lc)
```

```python
import functools

import jax
import jax.numpy as jnp
from jax import lax
from jax.experimental import pallas as pl
from jax.experimental.pallas import tpu as pltpu

F32 = jnp.float32
BF16 = jnp.bfloat16
I32 = jnp.int32
U32 = jnp.uint32
HIGHEST = lax.Precision.HIGHEST

GRID_W = 64
EPS = 1e-6
N_MOD = 6
GLA_HEADS = 4
GLA_DK = 64
GLA_DV = 128
GLA_KW = GLA_HEADS * GLA_DK
GLA_VW = GLA_HEADS * GLA_DV
GATE_RANK = 16
GATE_TAU = 16.0
CHUNK = 64
ROPE_BASE = 10000.0
NA_HEADS = 8
NA_DH = 64
NA_W = NA_HEADS * NA_DH
NA_KH = 8
NA_KW = 16
PEER_HEADS = 8
N_KEYS = 128
PEER_DK = 256
PEER_TOPK = 16

LANE = 128
SUBLANE = 8
VREG_ELEMS = LANE * SUBLANE

ROW_TILE = 256
GLA_BLOCK = 256
SELECT_TILE = 256
PEER_TOKENS = 128
MASK_NEG = -1e30

_NT = (((1,), (1,)), ((), ()))


def _cp(sem, vmem_mb):
    return pltpu.CompilerParams(dimension_semantics=sem, vmem_limit_bytes=vmem_mb * 1024 * 1024)


def _norm_mod(x, g, shift, scale):
    ms = jnp.mean(x * x, axis=-1, keepdims=True)
    return (x * lax.rsqrt(ms + EPS) * g) * (1.0 + scale) + shift


def _mod_kernel(c_ref, w_ref, b_ref, o_ref):
    a = c_ref[...]
    a = a * jax.nn.sigmoid(a)
    o_ref[0] = jnp.dot(a, w_ref[0], preferred_element_type=F32, precision=HIGHEST) + b_ref[0]


def _modulation(cin, w_mod, b_mod):
    depth, d, n = w_mod.shape
    rows = cin.shape[0]
    tn = n // 4
    return pl.pallas_call(
        _mod_kernel,
        grid=(depth, n // tn),
        in_specs=[pl.BlockSpec((rows, d), lambda l, j: (0, 0)),
                  pl.BlockSpec((1, d, tn), lambda l, j: (l, 0, j)),
                  pl.BlockSpec((1, 1, tn), lambda l, j: (l, 0, j))],
        out_specs=pl.BlockSpec((1, rows, tn), lambda l, j: (l, 0, j)),
        out_shape=jax.ShapeDtypeStruct((depth, rows, n), F32),
        compiler_params=_cp(("arbitrary", "arbitrary"), 40),
        name="modulation",
    )(cin, w_mod, b_mod.reshape(depth, 1, n))


_PROJ_OUTS = (("qk", 2 * GLA_KW, F32), ("v", GLA_VW, F32), ("r", GLA_VW, F32),
              ("nq", NA_W, BF16), ("nk", NA_W, BF16), ("nv", NA_W, BF16),
              ("gates", None, F32), ("z", LANE, F32))


def _proj_kernel(x_ref, mod_ref, g_ref, w_ref, *out_refs):
    m = mod_ref[0, 0]
    h = _norm_mod(x_ref[0], g_ref[...], m[0:1], m[1:2]).astype(BF16)
    col = 0
    for ref in out_refs:
        n = ref.shape[-1]
        ref[0] = jnp.dot(h, w_ref[:, col:col + n], preferred_element_type=F32).astype(ref.dtype)
        col += n


def _project(xa, modseg, g, w_perm, n_ctx_tiles):
    b, t, d = xa.shape
    outs = [(name, (2 * d if n is None else n), dt) for name, n, dt in _PROJ_OUTS]
    assert sum(n for _, n, _ in outs) == w_perm.shape[1]
    return pl.pallas_call(
        _proj_kernel,
        grid=(b, t // ROW_TILE),
        in_specs=[pl.BlockSpec((1, ROW_TILE, d), lambda i, j: (i, j, 0)),
                  pl.BlockSpec((1, 1, SUBLANE, d), lambda i, j: (i, (j >= n_ctx_tiles).astype(I32), 0, 0)),
                  pl.BlockSpec((1, d), lambda i, j: (0, 0)),
                  pl.BlockSpec(w_perm.shape, lambda i, j: (0, 0))],
        out_specs=[pl.BlockSpec((1, ROW_TILE, n), lambda i, j: (i, j, 0)) for _, n, _ in outs],
        out_shape=[jax.ShapeDtypeStruct((b, t, n), dt) for _, n, dt in outs],
        compiler_params=_cp(("parallel", "arbitrary"), 48),
        name="in_proj",
    )(xa, modseg, g, w_perm)


def _gla_kernel(qk_ref, v_ref, z_ref, cs_ref, sn_ref, wa_ref, ba_ref, o_ref, st_ref, *, reverse):
    @pl.when(pl.program_id(1) == 0)
    def _init():
        st_ref[...] = jnp.zeros_like(st_ref)

    kw2 = 2 * GLA_KW
    blk = qk_ref.shape[1]
    qk = qk_ref[0]
    lane = lax.broadcasted_iota(I32, (1, kw2), 1)
    first = (lane % 32) < 16
    partner = jnp.where(first, pltpu.roll(qk, kw2 - 16, 1), pltpu.roll(qk, 16, 1))
    qk = qk * cs_ref[...] + partner * sn_ref[...]
    zz = jnp.dot(z_ref[0], wa_ref[...], preferred_element_type=F32) + ba_ref[...]
    la = (jnp.minimum(zz, 0.0) - jnp.log1p(jnp.exp(-jnp.abs(zz)))) * (1.0 / GATE_TAU)
    la_hi = la.astype(BF16)
    rem = la - la_hi.astype(F32)
    la_mid = rem.astype(BF16)
    la_lo = (rem - la_mid.astype(F32)).astype(BF16)

    ri = lax.broadcasted_iota(I32, (CHUNK, CHUNK), 0)
    ci = lax.broadcasted_iota(I32, (CHUNK, CHUNK), 1)
    keep = (ci >= ri) if reverse else (ci <= ri)
    tri3 = jnp.concatenate([keep.astype(BF16)] * 3, axis=1)
    khead = lax.broadcasted_iota(I32, (1, GLA_KW), 1) // GLA_DK
    vrow = lax.broadcasted_iota(I32, (GLA_VW, GLA_KW), 0) // GLA_DV
    kcol = lax.broadcasted_iota(I32, (GLA_VW, GLA_KW), 1) // GLA_DK
    same_head = vrow == kcol

    order = range(blk // CHUNK)
    if reverse:
        order = reversed(order)
    for c in order:
        lo = c * CHUNK
        q = qk[lo:lo + CHUNK, :GLA_KW]
        k = qk[lo:lo + CHUNK, GLA_KW:]
        v = v_ref[0, lo:lo + CHUNK, :].astype(BF16)
        pieces = jnp.concatenate([p[lo:lo + CHUNK] for p in (la_hi, la_mid, la_lo)], axis=0)
        bcum = jnp.dot(tri3, pieces, preferred_element_type=F32)
        bend = bcum[0:1] if reverse else bcum[CHUNK - 1:CHUNK]
        qd = q * jnp.exp(bcum)
        kd = (k * jnp.exp(-bcum)).astype(BF16)
        kd2 = (k * jnp.exp(bend - bcum)).astype(BF16)
        st = st_ref[...]
        o_inter = lax.dot_general(qd.astype(BF16), st.astype(BF16), _NT, preferred_element_type=F32)
        for h in range(GLA_HEADS):
            qm = jnp.where(khead == h, qd, 0.0).astype(BF16)
            s = lax.dot_general(qm, kd, _NT, preferred_element_type=F32)
            s = jnp.where(keep, s, 0.0).astype(BF16)
            vs = slice(h * GLA_DV, (h + 1) * GLA_DV)
            o_ref[0, lo:lo + CHUNK, vs] = (jnp.dot(s, v[:, vs], preferred_element_type=F32)
                                           + o_inter[:, vs])
        upd = jnp.dot(v.T, kd2, preferred_element_type=F32)
        st_ref[...] = st * jnp.exp(bend) + jnp.where(same_head, upd, 0.0)


def _gla(qk, v, z, cs, sn, wa, ba, n_ctx_blocks, reverse):
    b, t, _ = qk.shape
    nblk = t // GLA_BLOCK

    if reverse:
        def blk(j):
            return jnp.where(j < n_ctx_blocks, n_ctx_blocks - 1 - j, nblk - 1 - (j - n_ctx_blocks))
    else:
        def blk(j):
            return j

    tok = lambda n: pl.BlockSpec((1, GLA_BLOCK, n), lambda i, j: (i, blk(j), 0))
    tab = lambda n: pl.BlockSpec((GLA_BLOCK, n), lambda i, j: (blk(j), 0))
    return pl.pallas_call(
        functools.partial(_gla_kernel, reverse=reverse),
        grid=(b, nblk),
        in_specs=[tok(2 * GLA_KW), tok(GLA_VW), tok(LANE), tab(2 * GLA_KW), tab(2 * GLA_KW),
                  pl.BlockSpec((LANE, GLA_KW), lambda i, j: (0, 0)),
                  pl.BlockSpec((1, GLA_KW), lambda i, j: (0, 0))],
        out_specs=tok(GLA_VW),
        out_shape=jax.ShapeDtypeStruct((b, t, GLA_VW), F32),
        scratch_shapes=[pltpu.VMEM((GLA_VW, GLA_KW), F32)],
        compiler_params=_cp(("parallel", "arbitrary"), 40),
        name="gla_bwd" if reverse else "gla_fwd",
    )(qk, v, z, cs, sn, wa, ba)


def _na_kernel(*refs):
    q_ref = refs[0]
    k_refs = refs[1:1 + NA_KH]
    v_refs = refs[1 + NA_KH:1 + 2 * NA_KH]
    kc_ref, vc_ref, tb_ref, o_ref, s_ref, p_ref = refs[1 + 2 * NA_KH:]
    scale = NA_DH ** -0.5
    nw = NA_KH * GRID_W
    half = lax.broadcasted_iota(I32, (1, LANE), 1) // NA_DH
    for p in range(NA_HEADS // 2):
        ls = slice(p * LANE, (p + 1) * LANE)
        qp = q_ref[0, :, ls]
        kw = jnp.concatenate([r[0, :, ls] for r in k_refs], axis=0)
        kc = kc_ref[0, :, ls]
        for s in range(2):
            qm = jnp.where(half == s, qp, jnp.zeros_like(qp))
            s_ref[2 * p + s, :, :nw] = (lax.dot_general(qm, kw, _NT, preferred_element_type=F32) * scale
                                        + tb_ref[0, 2 * p + s])
            s_ref[2 * p + s, :, nw:] = lax.dot_general(qm, kc, _NT, preferred_element_type=F32) * scale
    for h in range(NA_HEADS):
        sc = s_ref[h]
        e = jnp.exp(sc - jnp.max(sc, axis=-1, keepdims=True))
        p_ref[h] = (e * (1.0 / jnp.sum(e, axis=-1, keepdims=True))).astype(BF16)
    for p in range(NA_HEADS // 2):
        ls = slice(p * LANE, (p + 1) * LANE)
        vw = jnp.concatenate([r[0, :, ls] for r in v_refs], axis=0)
        vc = vc_ref[0, :, ls]
        outs = [jnp.dot(p_ref[2 * p + s, :, :nw], vw, preferred_element_type=F32)
                + jnp.dot(p_ref[2 * p + s, :, nw:], vc, preferred_element_type=F32) for s in range(2)]
        o_ref[0, :, ls] = jnp.where(half == 0, outs[0], outs[1]).astype(o_ref.dtype)


def _na_bias_tables(rpb):
    w = jnp.arange(GRID_W, dtype=I32)
    c0 = jnp.clip(w - NA_KW // 2, 0, GRID_W - NA_KW)
    cp = w[None, :]
    valid = (cp >= c0[:, None]) & (cp < c0[:, None] + NA_KW)
    off = jnp.clip(cp - w[:, None] + (NA_KW - 1), 0, 2 * NA_KW - 2)
    t = jnp.where(valid[None, None], rpb[:, :, off], MASK_NEG)
    tabs = [jnp.concatenate([t[:, s + a] for a in range(NA_KH)], axis=-1) for s in range(NA_KH)]
    tabs.append(jnp.full_like(tabs[0], MASK_NEG))
    return jnp.stack(tabs, axis=0).astype(F32)


def _na(nq, nk, nv, tables, lc):
    b, t, _ = nq.shape
    n_ctx = lc // GRID_W
    rows = (t - lc) // GRID_W
    assert rows >= NA_KH

    def row0(j):
        r = jnp.maximum(j - n_ctx, 0)
        return r, jnp.clip(r - NA_KH // 2, 0, rows - NA_KH)

    def win(a):
        return pl.BlockSpec((1, GRID_W, NA_W), lambda i, j: (i, n_ctx + row0(j)[1] + a, 0))

    def tab_idx(i, j):
        r, r0 = row0(j)
        return (jnp.where(j < n_ctx, NA_KH, r0 - r + NA_KH - 1), 0, 0, 0)

    return pl.pallas_call(
        _na_kernel,
        grid=(b, t // GRID_W),
        in_specs=([pl.BlockSpec((1, GRID_W, NA_W), lambda i, j: (i, j, 0))]
                  + [win(a) for a in range(NA_KH)] + [win(a) for a in range(NA_KH)]
                  + [pl.BlockSpec((1, lc, NA_W), lambda i, j: (i, 0, 0))] * 2
                  + [pl.BlockSpec((1, NA_HEADS, GRID_W, NA_KH * GRID_W), tab_idx)]),
        out_specs=pl.BlockSpec((1, GRID_W, NA_W), lambda i, j: (i, j, 0)),
        out_shape=jax.ShapeDtypeStruct((b, t, NA_W), BF16),
        scratch_shapes=[pltpu.VMEM((NA_HEADS, GRID_W, NA_KH * GRID_W + lc), F32),
                        pltpu.VMEM((NA_HEADS, GRID_W, NA_KH * GRID_W + lc), BF16)],
        compiler_params=_cp(("parallel", "arbitrary"), 40),
        name="natten",
    )(nq, *([nk] * NA_KH), *([nv] * NA_KH), nk, nv, tables)


def _merge_kernel(x_ref, of_ref, ob_ref, r_ref, na_ref, gt_ref, mod_ref, gg_ref, wpg_ref, wpn_ref, wo_ref,
                  o_ref):
    d = x_ref.shape[-1]
    o = of_ref[0] + ob_ref[0]
    r = r_ref[0]
    parts = []
    for h in range(GLA_HEADS):
        oh = o[:, h * GLA_DV:(h + 1) * GLA_DV]
        ms = jnp.mean(oh * oh, axis=-1, keepdims=True)
        parts.append(oh * lax.rsqrt(ms + EPS) * gg_ref[...])
    on = jnp.concatenate(parts, axis=-1) * (r * jax.nn.sigmoid(r))
    ya = jnp.dot(on.astype(BF16), wpg_ref[...], preferred_element_type=F32)
    yb = jnp.dot(na_ref[0], wpn_ref[...], preferred_element_type=F32)
    gt = gt_ref[0]
    y = jax.nn.sigmoid(gt[:, :d]) * ya + jax.nn.sigmoid(gt[:, d:]) * yb
    y = jnp.dot(y.astype(BF16), wo_ref[...], preferred_element_type=F32)
    o_ref[0] = x_ref[0] + mod_ref[0, 0][2:3] * y


def _merge(xa, o_f, o_b, r, na, gates, modseg, g_gla, wpg, wpn, wo, n_ctx_tiles):
    b, t, d = xa.shape
    tok = lambda n: pl.BlockSpec((1, ROW_TILE, n), lambda i, j: (i, j, 0))
    full = lambda a: pl.BlockSpec(a.shape, lambda i, j: (0,) * a.ndim)
    return pl.pallas_call(
        _merge_kernel,
        grid=(b, t // ROW_TILE),
        in_specs=[tok(d), tok(GLA_VW), tok(GLA_VW), tok(GLA_VW), tok(NA_W), tok(2 * d),
                  pl.BlockSpec((1, 1, SUBLANE, d), lambda i, j: (i, (j >= n_ctx_tiles).astype(I32), 0, 0)),
                  full(g_gla), full(wpg), full(wpn), full(wo)],
        out_specs=tok(d),
        out_shape=jax.ShapeDtypeStruct((b, t, d), F32),
        compiler_params=_cp(("parallel", "arbitrary"), 40),
        name="merge",
    )(xa, o_f, o_b, r, na, gates, modseg, g_gla, wpg, wpn, wo)


def _topk_rows(s, ids, k):
    rows = s.shape[0]
    rid = lax.broadcasted_iota(I32, s.shape, 0).astype(F32)
    vals, out_ids = [], []
    for _ in range(k):
        m = jnp.max(s, axis=0, keepdims=True)
        pos = jnp.min(jnp.where(s == m, rid, float(rows)), axis=0, keepdims=True)
        hit = rid == pos
        vals.append(m)
        out_ids.append(pos if ids is None else jnp.max(jnp.where(hit, ids, -1.0), axis=0, keepdims=True))
        s = jnp.where(hit, -jnp.inf, s)
    return jnp.concatenate(vals, axis=0), jnp.concatenate(out_ids, axis=0)


def _staircase_candidates(v0, i0, v1, i1):
    k = PEER_TOPK
    sub = lax.broadcasted_iota(I32, (SUBLANE, v0.shape[1]), 0)
    vals = [v0[0:1] + v1]
    ids = [i0[0:1] * float(N_KEYS) + i1]
    for a in range(1, k // 2):
        nb = k // (a + 1)
        live = sub < nb
        vals.append(jnp.where(live, v0[a:a + 1] + v1[:SUBLANE], -jnp.inf))
        ids.append(i0[a:a + 1] * float(N_KEYS) + i1[:SUBLANE])
    vals.append(v0[k // 2:] + v1[0:1])
    ids.append(i0[k // 2:] * float(N_KEYS) + i1[0:1])
    return jnp.concatenate(vals, axis=0), jnp.concatenate(ids, axis=0)


def _peer1_kernel(x_ref, mod_ref, g_ref, wq_ref, sk_ref, idx_ref, pair_ref, gate_ref):
    m = mod_ref[0, 0]
    h = _norm_mod(x_ref[0], g_ref[...], m[3:4], m[4:5]).astype(BF16)
    half = PEER_DK // 2
    chosen = []
    for hd in range(PEER_HEADS):
        tops = []
        for p in range(2):
            col = (2 * hd + p) * half
            qp = jnp.dot(h, wq_ref[:, col:col + half], preferred_element_type=F32).astype(BF16)
            st = lax.dot_general(sk_ref[hd, p], qp, _NT, preferred_element_type=F32)
            tops.append(_topk_rows(st, None, PEER_TOPK))
        (v0, i0), (v1, i1) = tops
        cand, cidx = _staircase_candidates(v0, i0, v1, i1)
        best, experts = _topk_rows(cand, cidx, PEER_TOPK)
        e = jnp.exp(best - best[0:1])
        rs = slice(hd * PEER_TOPK, (hd + 1) * PEER_TOPK)
        gate_ref[0, rs, :] = e / jnp.sum(e, axis=0, keepdims=True)
        idx_ref[0, rs, :] = experts.astype(I32)
        chosen.append(experts)
    pair_ref[0] = lax.shift_right_logical(jnp.concatenate(chosen, axis=0).T.astype(I32), 1) * SUBLANE


def _peer_select(xa, modseg, g, wq, sk, lc):
    b, t, d = xa.shape
    ne = PEER_HEADS * PEER_TOPK
    tm = SELECT_TILE
    n_ctx_tiles = lc // tm
    return pl.pallas_call(
        _peer1_kernel,
        grid=(b, t // tm),
        in_specs=[pl.BlockSpec((1, tm, d), lambda i, j: (i, j, 0)),
                  pl.BlockSpec((1, 1, SUBLANE, d), lambda i, j: (i, (j >= n_ctx_tiles).astype(I32), 0, 0)),
                  pl.BlockSpec((1, d), lambda i, j: (0, 0)),
                  pl.BlockSpec(wq.shape, lambda i, j: (0, 0)),
                  pl.BlockSpec(sk.shape, lambda i, j: (0, 0, 0, 0))],
        out_specs=[pl.BlockSpec((1, ne, tm), lambda i, j: (i, 0, j)),
                   pl.BlockSpec((1, tm, ne), lambda i, j: (i, j, 0)),
                   pl.BlockSpec((1, ne, tm), lambda i, j: (i, 0, j))],
        out_shape=[jax.ShapeDtypeStruct((b, ne, t), I32), jax.ShapeDtypeStruct((b, t, ne), I32),
                   jax.ShapeDtypeStruct((b, ne, t), F32)],
        compiler_params=_cp(("parallel", "arbitrary"), 40),
        name="peer_select",
    )(xa, modseg, g, wq, sk)


PAIR_ROWS = 2 * SUBLANE


def _gather_tiles(tbl_ref, pair_ref, t, ne):
    base = pl.multiple_of(t * ne, ne)
    tiles = []
    for j in range(ne):
        row = pl.multiple_of(pair_ref[base + j], SUBLANE)
        tiles.append(pltpu.bitcast(tbl_ref[pl.ds(row, SUBLANE), :], BF16))
    return jnp.concatenate(tiles, axis=0)


TOKENS_PER_TRIP = 16


def _for_token_groups(tb, body):
    def trip(i, carry):
        for u in range(TOKENS_PER_TRIP):
            body(TOKENS_PER_TRIP * i + u)
        return carry

    lax.fori_loop(0, tb // TOKENS_PER_TRIP, trip, 0)


PRODUCT_GROUPS = 3


def _peer2_kernel(pair_ref, idx_ref, x_ref, mod_ref, g_ref, gate_ref, tbl_ref, cc_ref, accp_ref, acct_ref):
    tb = accp_ref.shape[0]
    ne = idx_ref.shape[1]
    kk = ne * PAIR_ROWS
    grp = 2 * PAIR_ROWS
    n_p = PRODUCT_GROUPS * PAIR_ROWS
    n_t = ne - n_p
    kt = n_t * PAIR_ROWS
    d = x_ref.shape[1]
    a = g_ref[0] * (1.0 + mod_ref[4])
    shift = mod_ref[3]
    srow = lax.broadcasted_iota(I32, (grp, PAIR_ROWS * PAIR_ROWS), 0)
    scol = lax.broadcasted_iota(I32, (grp, PAIR_ROWS * PAIR_ROWS), 1)
    sel = (((scol // PAIR_ROWS) == (srow % PAIR_ROWS)) & ((scol % 2) == (srow // PAIR_ROWS))).astype(BF16)
    dlane = lax.broadcasted_iota(I32, (SUBLANE, kt), 1)
    dsub = lax.broadcasted_iota(I32, (SUBLANE, kt), 0)
    diag = ((dlane % PAIR_ROWS) // 2) == dsub

    def token(t):
        x = x_ref[pl.ds(t, 1), :].reshape(SUBLANE, LANE)
        ms = jnp.sum(x * x, keepdims=True) * (1.0 / d)
        h = x * lax.rsqrt(ms + EPS) * a + shift
        hu = lax.bitcast_convert_type(h.astype(BF16).astype(F32), U32)
        hp = pltpu.bitcast(hu | lax.shift_right_logical(hu, jnp.uint32(16)), BF16)
        hb = jnp.concatenate([h, jnp.zeros_like(h)], axis=0).astype(BF16)
        base = pl.multiple_of(t * ne, ne)

        def tile(j):
            row = pl.multiple_of(pair_ref[base + j], SUBLANE)
            return pltpu.bitcast(tbl_ref[pl.ds(row, SUBLANE), :], BF16)

        parts = []
        for gi in range(PRODUCT_GROUPS):
            prods = [tile(j) * hp for j in range(gi * PAIR_ROWS, (gi + 1) * PAIR_ROWS)]
            parts.append(jnp.dot(sel, jnp.concatenate(prods, axis=0), preferred_element_type=F32))
        r = jnp.concatenate(parts, axis=0)
        accp_ref[pl.ds(t, 1), :] = jnp.sum(r.T, axis=0, keepdims=True)
        tiles = jnp.concatenate([tile(j) for j in range(n_p, ne)], axis=0)
        out = lax.dot_general(hb, tiles, _NT, preferred_element_type=F32)
        acct_ref[pl.ds(t, 1), :] = jnp.sum(jnp.where(diag, out[:SUBLANE], 0.0), axis=0, keepdims=True)

    _for_token_groups(tb, token)

    atp = accp_ref[...].T
    kq = lax.broadcasted_iota(I32, (2 * n_t, kt), 1)
    rq = lax.broadcasted_iota(I32, (2 * n_t, kt), 0)
    selt = (((kq // PAIR_ROWS) == (rq % n_t)) & ((kq % 2) == (rq // n_t))).astype(F32)
    att = lax.dot_general(selt, acct_ref[...], _NT, preferred_element_type=F32, precision=HIGHEST)
    a0 = jnp.concatenate([atp[gi * grp:gi * grp + PAIR_ROWS] for gi in range(PRODUCT_GROUPS)] + [att[:n_t]], axis=0)
    a1 = jnp.concatenate([atp[gi * grp + PAIR_ROWS:(gi + 1) * grp] for gi in range(PRODUCT_GROUPS)] + [att[n_t:]],
                         axis=0)
    par = idx_ref[0] & 1
    act = jnp.where(par == 0, a0, a1)
    coef = jax.nn.gelu(act, approximate=True) * gate_ref[0]
    ke = lax.broadcasted_iota(I32, (ne, kk), 1)
    re = lax.broadcasted_iota(I32, (ne, kk), 0)
    expand = ((ke // PAIR_ROWS) == re).astype(BF16)
    cexp = jnp.dot(coef.T.astype(BF16), expand, preferred_element_type=F32)
    pexp = jnp.dot(par.astype(F32).T.astype(BF16), expand, preferred_element_type=F32)
    lanepar = (lax.broadcasted_iota(I32, (tb, kk), 1) % 2).astype(F32)
    cc_ref[...] = jnp.where(pexp == lanepar, cexp, 0.0)


def _peer3_kernel(pair_ref, cc_ref, x_ref, mod_ref, tbl_ref, o_ref):
    tb, kk = cc_ref.shape
    ne = kk // PAIR_ROWS
    ga = mod_ref[5]
    lane = lax.broadcasted_iota(I32, (PAIR_ROWS, kk), 1)
    sub = lax.broadcasted_iota(I32, (PAIR_ROWS, kk), 0)
    diag = ((lane % PAIR_ROWS) // 2) == sub

    def token(t):
        row = cc_ref[pl.ds(t, 1), :]
        cm = jnp.where(diag, jnp.broadcast_to(row, (PAIR_ROWS, kk)), 0.0).astype(BF16)
        tiles = _gather_tiles(tbl_ref, pair_ref, t, ne)
        o = jnp.dot(cm, tiles, preferred_element_type=F32)
        xrow = x_ref[pl.ds(t, 1), :].reshape(SUBLANE, LANE)
        o_ref[pl.ds(t, 1), :] = (xrow + ga * o[:SUBLANE]).reshape(1, SUBLANE * LANE)

    _for_token_groups(tb, token)


def _peer_blocks(b, t, lc):
    nj = t // PEER_TOKENS
    n_ctx = lc // PEER_TOKENS
    tokb = pl.BlockSpec((PEER_TOKENS, SUBLANE * LANE), lambda n: (n, 0))
    modb = pl.BlockSpec((SUBLANE, SUBLANE, LANE), lambda n: ((n // nj) * 2 + ((n % nj) >= n_ctx).astype(I32), 0, 0))
    return nj, tokb, modb


def _peer_act(x2, pair2, idx, gate, modv, g3, tbl, b, t, lc):
    ne = PEER_HEADS * PEER_TOPK
    kk = ne * PAIR_ROWS
    nj, tokb, modb = _peer_blocks(b, t, lc)
    expb = pl.BlockSpec((1, ne, PEER_TOKENS), lambda n: (n // nj, 0, n % nj))
    return pl.pallas_call(
        _peer2_kernel,
        grid=(b * nj,),
        in_specs=[pl.BlockSpec((PEER_TOKENS * ne,), lambda n: (n,), memory_space=pltpu.SMEM),
                  expb, tokb, modb,
                  pl.BlockSpec((1, SUBLANE, LANE), lambda n: (0, 0, 0)),
                  expb,
                  pl.BlockSpec(tbl.shape, lambda n: (0, 0), pipeline_mode=pl.Buffered(1))],
        out_specs=pl.BlockSpec((PEER_TOKENS, kk), lambda n: (n, 0)),
        out_shape=jax.ShapeDtypeStruct((b * t, kk), F32),
        scratch_shapes=[pltpu.VMEM((PEER_TOKENS, 2 * PRODUCT_GROUPS * PAIR_ROWS), F32),
                        pltpu.VMEM((PEER_TOKENS, (ne - PRODUCT_GROUPS * PAIR_ROWS) * PAIR_ROWS), F32)],
        compiler_params=_cp(("arbitrary",), 56),
        name="peer_act",
    )(pair2, idx, x2, modv, g3, gate, tbl)


def _peer_out(x2, pair2, cc, modv, tbl, b, t, lc):
    ne = PEER_HEADS * PEER_TOPK
    kk = ne * PAIR_ROWS
    nj, tokb, modb = _peer_blocks(b, t, lc)
    return pl.pallas_call(
        _peer3_kernel,
        grid=(b * nj,),
        in_specs=[pl.BlockSpec((PEER_TOKENS * ne,), lambda n: (n,), memory_space=pltpu.SMEM),
                  pl.BlockSpec((PEER_TOKENS, kk), lambda n: (n, 0)),
                  tokb, modb,
                  pl.BlockSpec(tbl.shape, lambda n: (0, 0), pipeline_mode=pl.Buffered(1))],
        out_specs=tokb,
        out_shape=jax.ShapeDtypeStruct(x2.shape, F32),
        compiler_params=_cp(("arbitrary",), 56),
        name="peer_out",
    )(pair2, cc, x2, modv, tbl)


def _pack_experts(w):
    e, d = w.shape
    assert d == VREG_ELEMS
    bits = lax.bitcast_convert_type(w.astype(BF16), jnp.uint16).astype(U32).reshape(e // 2, 2, SUBLANE, LANE)
    return (bits[:, 0] | (bits[:, 1] << 16)).reshape(e // 2 * SUBLANE, LANE)


def _peer(xa, modseg, modv, g2, w_query, sub_keys, expert_u, expert_v, lc):
    b, t, d = xa.shape
    idx, pair, gate = _peer_select(xa, modseg, g2.reshape(1, d), w_query.astype(BF16), sub_keys.astype(BF16), lc)
    pair2 = pair.reshape(b * t * idx.shape[1])
    x2 = xa.reshape(b * t, d)
    cc = _peer_act(x2, pair2, idx, gate, modv, g2.reshape(1, SUBLANE, LANE), _pack_experts(expert_u), b, t, lc)
    x2 = _peer_out(x2, pair2, cc, modv, _pack_experts(expert_v), b, t, lc)
    return x2.reshape(b, t, d)


def _final_kernel(x_ref, g_ref, o_ref):
    x = x_ref[0]
    ms = jnp.mean(x * x, axis=-1, keepdims=True)
    o_ref[0] = x * lax.rsqrt(ms + EPS) * g_ref[...]


def _final_norm(xa, g, lc):
    b, t, d = xa.shape
    off = lc // ROW_TILE
    return pl.pallas_call(
        _final_kernel,
        grid=(b, (t - lc) // ROW_TILE),
        in_specs=[pl.BlockSpec((1, ROW_TILE, d), lambda i, j: (i, j + off, 0)),
                  pl.BlockSpec((1, d), lambda i, j: (0, 0))],
        out_specs=pl.BlockSpec((1, ROW_TILE, d), lambda i, j: (i, j, 0)),
        out_shape=jax.ShapeDtypeStruct((b, t - lc, d), F32),
        compiler_params=_cp(("parallel", "arbitrary"), 32),
        name="final_norm",
    )(xa, g)


def _rope_tables(l, lc):
    t = jnp.arange(l, dtype=I32)
    quarter = GLA_DK // 4
    freqs = ROPE_BASE ** (-jnp.arange(quarter, dtype=F32) / quarter)
    ang_r = (t // GRID_W).astype(F32)[:, None] * freqs
    ang_c = (t % GRID_W).astype(F32)[:, None] * freqs
    cos = jnp.concatenate([jnp.cos(ang_r)] * 2 + [jnp.cos(ang_c)] * 2, axis=-1)
    sin = jnp.concatenate([-jnp.sin(ang_r), jnp.sin(ang_r), -jnp.sin(ang_c), jnp.sin(ang_c)], axis=-1)
    cos = jnp.tile(cos, (1, GLA_HEADS))
    sin = jnp.tile(sin, (1, GLA_HEADS))
    qs = GLA_DK ** -0.5
    cs = jnp.concatenate([cos * qs, cos], axis=-1)
    sn = jnp.concatenate([sin * qs, sin], axis=-1)
    cs_ctx = jnp.concatenate([jnp.full((lc, GLA_KW), qs, F32), jnp.ones((lc, GLA_KW), F32)], axis=-1)
    return (jnp.concatenate([cs_ctx, cs], axis=0),
            jnp.concatenate([jnp.zeros((lc, 2 * GLA_KW), F32), sn], axis=0))


def _permute_w_in(w, d):
    sizes = (GLA_KW, GLA_KW, GLA_VW, 2 * GATE_RANK, GLA_VW, NA_W, NA_W, NA_W, 2 * d)
    offs = [0]
    for s in sizes:
        offs.append(offs[-1] + s)
    q, k, v, z, r, nq, nk, nv, gates = [w[:, offs[i]:offs[i + 1]] for i in range(len(sizes))]
    zpad = jnp.pad(z, ((0, 0), (0, LANE - 2 * GATE_RANK)))
    return jnp.concatenate([q, k, v, r, nq, nk, nv, gates, zpad], axis=1).astype(BF16)


def _gate_weights(w_a, b_a, first_row):
    wa = jnp.zeros((LANE, GLA_KW), F32).at[first_row:first_row + GATE_RANK].set(w_a)
    return wa, b_a.reshape(1, GLA_KW)


def kernel(x, c, ctx, c_ctx, w_mod, b_mod, g_norm1, w_in, w_alpha_f, b_alpha_f, w_alpha_b, b_alpha_b, g_gla, rpb,
           w_proj_gla, w_proj_na, w_out, g_norm2, w_query, sub_keys, expert_u, expert_v, g_final):
    b, l, d = x.shape
    lc = ctx.shape[1]
    depth = w_mod.shape[0]
    t = lc + l
    assert d == VREG_ELEMS and lc % ROW_TILE == 0 and l % ROW_TILE == 0 and l % GRID_W == 0
    n_ctx_tiles = lc // ROW_TILE

    rows = -(-(b + 1) // SUBLANE) * SUBLANE
    cin = jnp.zeros((rows, d), F32).at[:b].set(c).at[b].set(c_ctx)
    mod_all = _modulation(cin, w_mod, b_mod)

    cs, sn = _rope_tables(l, lc)
    xa = jnp.concatenate([ctx, x], axis=1)

    for i in range(depth):
        mod = mod_all[i].reshape(rows, N_MOD, d)
        modseg = jnp.stack([jnp.broadcast_to(mod[b], (b, N_MOD, d)), mod[:b]], axis=1)
        modseg = jnp.pad(modseg, ((0, 0), (0, 0), (0, SUBLANE - N_MOD), (0, 0)))
        modv = modseg.reshape(b * 2 * SUBLANE, SUBLANE, LANE)

        qk, v, r, nq, nk, nv, gates, z = _project(xa, modseg, g_norm1[i].reshape(1, d),
                                                  _permute_w_in(w_in[i], d), n_ctx_tiles)
        waf, baf = _gate_weights(w_alpha_f[i], b_alpha_f[i], 0)
        wab, bab = _gate_weights(w_alpha_b[i], b_alpha_b[i], GATE_RANK)
        o_f = _gla(qk, v, z, cs, sn, waf, baf, lc // GLA_BLOCK, reverse=False)
        o_b = _gla(qk, v, z, cs, sn, wab, bab, lc // GLA_BLOCK, reverse=True)
        na = _na(nq, nk, nv, _na_bias_tables(rpb[i]), lc)
        xa = _merge(xa, o_f, o_b, r, na, gates, modseg, g_gla[i].reshape(1, GLA_DV),
                    w_proj_gla[i].astype(BF16), w_proj_na[i].astype(BF16), w_out[i].astype(BF16), n_ctx_tiles)

        xa = _peer(xa, modseg, modv, g_norm2[i], w_query[i], sub_keys[i], expert_u[i], expert_v[i], lc)

    return _final_norm(xa, g_final.reshape(1, d), lc)
```

```python
import functools

import jax
import jax.numpy as jnp
from jax import lax
from jax.experimental import pallas as pl
from jax.experimental.pallas import tpu as pltpu

F32 = jnp.float32
BF16 = jnp.bfloat16
I32 = jnp.int32
U32 = jnp.uint32
HIGHEST = lax.Precision.HIGHEST

GRID_W = 64
EPS = 1e-6
N_MOD = 6
GLA_HEADS = 4
GLA_DK = 64
GLA_DV = 128
GLA_KW = GLA_HEADS * GLA_DK
GLA_VW = GLA_HEADS * GLA_DV
GATE_RANK = 16
GATE_TAU = 16.0
CHUNK = 64
ROPE_BASE = 10000.0
NA_HEADS = 8
NA_DH = 64
NA_W = NA_HEADS * NA_DH
NA_KH = 8
NA_KW = 16
PEER_HEADS = 8
N_KEYS = 128
PEER_DK = 256
PEER_TOPK = 16

LANE = 128
SUBLANE = 8
VREG_ELEMS = LANE * SUBLANE

ROW_TILE = 256
GLA_BLOCK = 256
SELECT_TILE = 256
PEER_TOKENS = 128
MASK_NEG = -1e30

_NT = (((1,), (1,)), ((), ()))


def _cp(sem, vmem_mb):
    return pltpu.CompilerParams(dimension_semantics=sem, vmem_limit_bytes=vmem_mb * 1024 * 1024)


def _norm_mod(x, g, shift, scale):
    ms = jnp.mean(x * x, axis=-1, keepdims=True)
    return (x * lax.rsqrt(ms + EPS) * g) * (1.0 + scale) + shift


def _mod_kernel(c_ref, w_ref, b_ref, o_ref):
    a = c_ref[...]
    a = a * jax.nn.sigmoid(a)
    o_ref[0] = jnp.dot(a, w_ref[0], preferred_element_type=F32, precision=HIGHEST) + b_ref[0]


def _modulation(cin, w_mod, b_mod):
    depth, d, n = w_mod.shape
    rows = cin.shape[0]
    tn = n // 4
    return pl.pallas_call(
        _mod_kernel,
        grid=(depth, n // tn),
        in_specs=[pl.BlockSpec((rows, d), lambda l, j: (0, 0)),
                  pl.BlockSpec((1, d, tn), lambda l, j: (l, 0, j)),
                  pl.BlockSpec((1, 1, tn), lambda l, j: (l, 0, j))],
        out_specs=pl.BlockSpec((1, rows, tn), lambda l, j: (l, 0, j)),
        out_shape=jax.ShapeDtypeStruct((depth, rows, n), F32),
        compiler_params=_cp(("arbitrary", "arbitrary"), 40),
        name="modulation",
    )(cin, w_mod, b_mod.reshape(depth, 1, n))


_PROJ_OUTS = (("qk", 2 * GLA_KW, F32), ("v", GLA_VW, F32), ("r", GLA_VW, F32),
              ("nq", NA_W, BF16), ("nk", NA_W, BF16), ("nv", NA_W, BF16),
              ("gates", None, F32), ("z", LANE, F32))


def _proj_kernel(x_ref, mod_ref, g_ref, w_ref, *out_refs):
    m = mod_ref[0, 0]
    h = _norm_mod(x_ref[0], g_ref[...], m[0:1], m[1:2]).astype(BF16)
    col = 0
    for ref in out_refs:
        n = ref.shape[-1]
        ref[0] = jnp.dot(h, w_ref[:, col:col + n], preferred_element_type=F32).astype(ref.dtype)
        col += n


def _project(xa, modseg, g, w_perm, n_ctx_tiles):
    b, t, d = xa.shape
    outs = [(name, (2 * d if n is None else n), dt) for name, n, dt in _PROJ_OUTS]
    assert sum(n for _, n, _ in outs) == w_perm.shape[1]
    return pl.pallas_call(
        _proj_kernel,
        grid=(b, t // ROW_TILE),
        in_specs=[pl.BlockSpec((1, ROW_TILE, d), lambda i, j: (i, j, 0)),
                  pl.BlockSpec((1, 1, SUBLANE, d), lambda i, j: (i, (j >= n_ctx_tiles).astype(I32), 0, 0)),
                  pl.BlockSpec((1, d), lambda i, j: (0, 0)),
                  pl.BlockSpec(w_perm.shape, lambda i, j: (0, 0))],
        out_specs=[pl.BlockSpec((1, ROW_TILE, n), lambda i, j: (i, j, 0)) for _, n, _ in outs],
        out_shape=[jax.ShapeDtypeStruct((b, t, n), dt) for _, n, dt in outs],
        compiler_params=_cp(("parallel", "arbitrary"), 48),
        name="in_proj",
    )(xa, modseg, g, w_perm)


def _gla_kernel(qk_ref, v_ref, z_ref, cs_ref, sn_ref, wa_ref, ba_ref, o_ref, st_ref, *, reverse):
    @pl.when(pl.program_id(1) == 0)
    def _init():
        st_ref[...] = jnp.zeros_like(st_ref)

    kw2 = 2 * GLA_KW
    blk = qk_ref.shape[1]
    qk = qk_ref[0]
    lane = lax.broadcasted_iota(I32, (1, kw2), 1)
    first = (lane % 32) < 16
    partner = jnp.where(first, pltpu.roll(qk, kw2 - 16, 1), pltpu.roll(qk, 16, 1))
    qk = qk * cs_ref[...] + partner * sn_ref[...]
    zz = jnp.dot(z_ref[0], wa_ref[...], preferred_element_type=F32) + ba_ref[...]
    la = (jnp.minimum(zz, 0.0) - jnp.log1p(jnp.exp(-jnp.abs(zz)))) * (1.0 / GATE_TAU)
    la_hi = la.astype(BF16)
    rem = la - la_hi.astype(F32)
    la_mid = rem.astype(BF16)
    la_lo = (rem - la_mid.astype(F32)).astype(BF16)

    ri = lax.broadcasted_iota(I32, (CHUNK, CHUNK), 0)
    ci = lax.broadcasted_iota(I32, (CHUNK, CHUNK), 1)
    keep = (ci >= ri) if reverse else (ci <= ri)
    tri3 = jnp.concatenate([keep.astype(BF16)] * 3, axis=1)
    khead = lax.broadcasted_iota(I32, (1, GLA_KW), 1) // GLA_DK
    vrow = lax.broadcasted_iota(I32, (GLA_VW, GLA_KW), 0) // GLA_DV
    kcol = lax.broadcasted_iota(I32, (GLA_VW, GLA_KW), 1) // GLA_DK
    same_head = vrow == kcol

    order = range(blk // CHUNK)
    if reverse:
        order = reversed(order)
    for c in order:
        lo = c * CHUNK
        q = qk[lo:lo + CHUNK, :GLA_KW]
        k = qk[lo:lo + CHUNK, GLA_KW:]
        v = v_ref[0, lo:lo + CHUNK, :].astype(BF16)
        pieces = jnp.concatenate([p[lo:lo + CHUNK] for p in (la_hi, la_mid, la_lo)], axis=0)
        bcum = jnp.dot(tri3, pieces, preferred_element_type=F32)
        bend = bcum[0:1] if reverse else bcum[CHUNK - 1:CHUNK]
        qd = q * jnp.exp(bcum)
        kd = (k * jnp.exp(-bcum)).astype(BF16)
        kd2 = (k * jnp.exp(bend - bcum)).astype(BF16)
        st = st_ref[...]
        o_inter = lax.dot_general(qd.astype(BF16), st.astype(BF16), _NT, preferred_element_type=F32)
        for h in range(GLA_HEADS):
            qm = jnp.where(khead == h, qd, 0.0).astype(BF16)
            s = lax.dot_general(qm, kd, _NT, preferred_element_type=F32)
            s = jnp.where(keep, s, 0.0).astype(BF16)
            vs = slice(h * GLA_DV, (h + 1) * GLA_DV)
            o_ref[0, lo:lo + CHUNK, vs] = (jnp.dot(s, v[:, vs], preferred_element_type=F32)
                                           + o_inter[:, vs])
        upd = jnp.dot(v.T, kd2, preferred_element_type=F32)
        st_ref[...] = st * jnp.exp(bend) + jnp.where(same_head, upd, 0.0)


def _gla(qk, v, z, cs, sn, wa, ba, n_ctx_blocks, reverse):
    b, t, _ = qk.shape
    nblk = t // GLA_BLOCK

    if reverse:
        def blk(j):
            return jnp.where(j < n_ctx_blocks, n_ctx_blocks - 1 - j, nblk - 1 - (j - n_ctx_blocks))
    else:
        def blk(j):
            return j

    tok = lambda n: pl.BlockSpec((1, GLA_BLOCK, n), lambda i, j: (i, blk(j), 0))
    tab = lambda n: pl.BlockSpec((GLA_BLOCK, n), lambda i, j: (blk(j), 0))
    return pl.pallas_call(
        functools.partial(_gla_kernel, reverse=reverse),
        grid=(b, nblk),
        in_specs=[tok(2 * GLA_KW), tok(GLA_VW), tok(LANE), tab(2 * GLA_KW), tab(2 * GLA_KW),
                  pl.BlockSpec((LANE, GLA_KW), lambda i, j: (0, 0)),
                  pl.BlockSpec((1, GLA_KW), lambda i, j: (0, 0))],
        out_specs=tok(GLA_VW),
        out_shape=jax.ShapeDtypeStruct((b, t, GLA_VW), F32),
        scratch_shapes=[pltpu.VMEM((GLA_VW, GLA_KW), F32)],
        compiler_params=_cp(("parallel", "arbitrary"), 40),
        name="gla_bwd" if reverse else "gla_fwd",
    )(qk, v, z, cs, sn, wa, ba)


def _na_kernel(*refs):
    q_ref = refs[0]
    k_refs = refs[1:1 + NA_KH]
    v_refs = refs[1 + NA_KH:1 + 2 * NA_KH]
    kc_ref, vc_ref, tb_ref, o_ref, s_ref, p_ref = refs[1 + 2 * NA_KH:]
    scale = NA_DH ** -0.5
    nw = NA_KH * GRID_W
    half = lax.broadcasted_iota(I32, (1, LANE), 1) // NA_DH
    for p in range(NA_HEADS // 2):
        ls = slice(p * LANE, (p + 1) * LANE)
        qp = q_ref[0, :, ls]
        kw = jnp.concatenate([r[0, :, ls] for r in k_refs], axis=0)
        kc = kc_ref[0, :, ls]
        for s in range(2):
            qm = jnp.where(half == s, qp, jnp.zeros_like(qp))
            s_ref[2 * p + s, :, :nw] = (lax.dot_general(qm, kw, _NT, preferred_element_type=F32) * scale
                                        + tb_ref[0, 2 * p + s])
            s_ref[2 * p + s, :, nw:] = lax.dot_general(qm, kc, _NT, preferred_element_type=F32) * scale
    for h in range(NA_HEADS):
        sc = s_ref[h]
        e = jnp.exp(sc - jnp.max(sc, axis=-1, keepdims=True))
        p_ref[h] = (e * (1.0 / jnp.sum(e, axis=-1, keepdims=True))).astype(BF16)
    for p in range(NA_HEADS // 2):
        ls = slice(p * LANE, (p + 1) * LANE)
        vw = jnp.concatenate([r[0, :, ls] for r in v_refs], axis=0)
        vc = vc_ref[0, :, ls]
        outs = [jnp.dot(p_ref[2 * p + s, :, :nw], vw, preferred_element_type=F32)
                + jnp.dot(p_ref[2 * p + s, :, nw:], vc, preferred_element_type=F32) for s in range(2)]
        o_ref[0, :, ls] = jnp.where(half == 0, outs[0], outs[1]).astype(o_ref.dtype)


def _na_bias_tables(rpb):
    w = jnp.arange(GRID_W, dtype=I32)
    c0 = jnp.clip(w - NA_KW // 2, 0, GRID_W - NA_KW)
    cp = w[None, :]
    valid = (cp >= c0[:, None]) & (cp < c0[:, None] + NA_KW)
    off = jnp.clip(cp - w[:, None] + (NA_KW - 1), 0, 2 * NA_KW - 2)
    t = jnp.where(valid[None, None], rpb[:, :, off], MASK_NEG)
    tabs = [jnp.concatenate([t[:, s + a] for a in range(NA_KH)], axis=-1) for s in range(NA_KH)]
    tabs.append(jnp.full_like(tabs[0], MASK_NEG))
    return jnp.stack(tabs, axis=0).astype(F32)


def _na(nq, nk, nv, tables, lc):
    b, t, _ = nq.shape
    n_ctx = lc // GRID_W
    rows = (t - lc) // GRID_W
    assert rows >= NA_KH

    def row0(j):
        r = jnp.maximum(j - n_ctx, 0)
        return r, jnp.clip(r - NA_KH // 2, 0, rows - NA_KH)

    def win(a):
        return pl.BlockSpec((1, GRID_W, NA_W), lambda i, j: (i, n_ctx + row0(j)[1] + a, 0))

    def tab_idx(i, j):
        r, r0 = row0(j)
        return (jnp.where(j < n_ctx, NA_KH, r0 - r + NA_KH - 1), 0, 0, 0)

    return pl.pallas_call(
        _na_kernel,
        grid=(b, t // GRID_W),
        in_specs=([pl.BlockSpec((1, GRID_W, NA_W), lambda i, j: (i, j, 0))]
                  + [win(a) for a in range(NA_KH)] + [win(a) for a in range(NA_KH)]
                  + [pl.BlockSpec((1, lc, NA_W), lambda i, j: (i, 0, 0))] * 2
                  + [pl.BlockSpec((1, NA_HEADS, GRID_W, NA_KH * GRID_W), tab_idx)]),
        out_specs=pl.BlockSpec((1, GRID_W, NA_W), lambda i, j: (i, j, 0)),
        out_shape=jax.ShapeDtypeStruct((b, t, NA_W), BF16),
        scratch_shapes=[pltpu.VMEM((NA_HEADS, GRID_W, NA_KH * GRID_W + lc), F32),
                        pltpu.VMEM((NA_HEADS, GRID_W, NA_KH * GRID_W + lc), BF16)],
        compiler_params=_cp(("parallel", "arbitrary"), 40),
        name="natten",
    )(nq, *([nk] * NA_KH), *([nv] * NA_KH), nk, nv, tables)


def _merge_kernel(x_ref, of_ref, ob_ref, r_ref, na_ref, gt_ref, mod_ref, gg_ref, wpg_ref, wpn_ref, wo_ref,
                  o_ref):
    d = x_ref.shape[-1]
    o = of_ref[0] + ob_ref[0]
    r = r_ref[0]
    parts = []
    for h in range(GLA_HEADS):
        oh = o[:, h * GLA_DV:(h + 1) * GLA_DV]
        ms = jnp.mean(oh * oh, axis=-1, keepdims=True)
        parts.append(oh * lax.rsqrt(ms + EPS) * gg_ref[...])
    on = jnp.concatenate(parts, axis=-1) * (r * jax.nn.sigmoid(r))
    ya = jnp.dot(on.astype(BF16), wpg_ref[...], preferred_element_type=F32)
    yb = jnp.dot(na_ref[0], wpn_ref[...], preferred_element_type=F32)
    gt = gt_ref[0]
    y = jax.nn.sigmoid(gt[:, :d]) * ya + jax.nn.sigmoid(gt[:, d:]) * yb
    y = jnp.dot(y.astype(BF16), wo_ref[...], preferred_element_type=F32)
    o_ref[0] = x_ref[0] + mod_ref[0, 0][2:3] * y


def _merge(xa, o_f, o_b, r, na, gates, modseg, g_gla, wpg, wpn, wo, n_ctx_tiles):
    b, t, d = xa.shape
    tok = lambda n: pl.BlockSpec((1, ROW_TILE, n), lambda i, j: (i, j, 0))
    full = lambda a: pl.BlockSpec(a.shape, lambda i, j: (0,) * a.ndim)
    return pl.pallas_call(
        _merge_kernel,
        grid=(b, t // ROW_TILE),
        in_specs=[tok(d), tok(GLA_VW), tok(GLA_VW), tok(GLA_VW), tok(NA_W), tok(2 * d),
                  pl.BlockSpec((1, 1, SUBLANE, d), lambda i, j: (i, (j >= n_ctx_tiles).astype(I32), 0, 0)),
                  full(g_gla), full(wpg), full(wpn), full(wo)],
        out_specs=tok(d),
        out_shape=jax.ShapeDtypeStruct((b, t, d), F32),
        compiler_params=_cp(("parallel", "arbitrary"), 40),
        name="merge",
    )(xa, o_f, o_b, r, na, gates, modseg, g_gla, wpg, wpn, wo)


def _topk_rows(s, ids, k):
    rows = s.shape[0]
    rid = lax.broadcasted_iota(I32, s.shape, 0).astype(F32)
    vals, out_ids = [], []
    for _ in range(k):
        m = jnp.max(s, axis=0, keepdims=True)
        pos = jnp.min(jnp.where(s == m, rid, float(rows)), axis=0, keepdims=True)
        hit = rid == pos
        vals.append(m)
        out_ids.append(pos if ids is None else jnp.max(jnp.where(hit, ids, -1.0), axis=0, keepdims=True))
        s = jnp.where(hit, -jnp.inf, s)
    return jnp.concatenate(vals, axis=0), jnp.concatenate(out_ids, axis=0)


def _staircase_candidates(v0, i0, v1, i1):
    k = PEER_TOPK
    sub = lax.broadcasted_iota(I32, (SUBLANE, v0.shape[1]), 0)
    vals = [v0[0:1] + v1]
    ids = [i0[0:1] * float(N_KEYS) + i1]
    for a in range(1, k // 2):
        nb = k // (a + 1)
        live = sub < nb
        vals.append(jnp.where(live, v0[a:a + 1] + v1[:SUBLANE], -jnp.inf))
        ids.append(i0[a:a + 1] * float(N_KEYS) + i1[:SUBLANE])
    vals.append(v0[k // 2:] + v1[0:1])
    ids.append(i0[k // 2:] * float(N_KEYS) + i1[0:1])
    return jnp.concatenate(vals, axis=0), jnp.concatenate(ids, axis=0)


def _peer1_kernel(x_ref, mod_ref, g_ref, wq_ref, sk_ref, idx_ref, pair_ref, gate_ref):
    m = mod_ref[0, 0]
    h = _norm_mod(x_ref[0], g_ref[...], m[3:4], m[4:5]).astype(BF16)
    half = PEER_DK // 2
    chosen = []
    for hd in range(PEER_HEADS):
        tops = []
        for p in range(2):
            col = (2 * hd + p) * half
            qp = jnp.dot(h, wq_ref[:, col:col + half], preferred_element_type=F32).astype(BF16)
            st = lax.dot_general(sk_ref[hd, p], qp, _NT, preferred_element_type=F32)
            tops.append(_topk_rows(st, None, PEER_TOPK))
        (v0, i0), (v1, i1) = tops
        cand, cidx = _staircase_candidates(v0, i0, v1, i1)
        best, experts = _topk_rows(cand, cidx, PEER_TOPK)
        e = jnp.exp(best - best[0:1])
        rs = slice(hd * PEER_TOPK, (hd + 1) * PEER_TOPK)
        gate_ref[0, rs, :] = e / jnp.sum(e, axis=0, keepdims=True)
        idx_ref[0, rs, :] = experts.astype(I32)
        chosen.append(experts)
    pair_ref[0] = lax.shift_right_logical(jnp.concatenate(chosen, axis=0).T.astype(I32), 1) * SUBLANE


def _peer_select(xa, modseg, g, wq, sk, lc):
    b, t, d = xa.shape
    ne = PEER_HEADS * PEER_TOPK
    tm = SELECT_TILE
    n_ctx_tiles = lc // tm
    return pl.pallas_call(
        _peer1_kernel,
        grid=(b, t // tm),
        in_specs=[pl.BlockSpec((1, tm, d), lambda i, j: (i, j, 0)),
                  pl.BlockSpec((1, 1, SUBLANE, d), lambda i, j: (i, (j >= n_ctx_tiles).astype(I32), 0, 0)),
                  pl.BlockSpec((1, d), lambda i, j: (0, 0)),
                  pl.BlockSpec(wq.shape, lambda i, j: (0, 0)),
                  pl.BlockSpec(sk.shape, lambda i, j: (0, 0, 0, 0))],
        out_specs=[pl.BlockSpec((1, ne, tm), lambda i, j: (i, 0, j)),
                   pl.BlockSpec((1, tm, ne), lambda i, j: (i, j, 0)),
                   pl.BlockSpec((1, ne, tm), lambda i, j: (i, 0, j))],
        out_shape=[jax.ShapeDtypeStruct((b, ne, t), I32), jax.ShapeDtypeStruct((b, t, ne), I32),
                   jax.ShapeDtypeStruct((b, ne, t), F32)],
        compiler_params=_cp(("parallel", "arbitrary"), 40),
        name="peer_select",
    )(xa, modseg, g, wq, sk)


PAIR_ROWS = 2 * SUBLANE


def _gather_tiles(tbl_ref, pair_ref, t, ne):
    base = pl.multiple_of(t * ne, ne)
    tiles = []
    for j in range(ne):
        row = pl.multiple_of(pair_ref[base + j], SUBLANE)
        tiles.append(pltpu.bitcast(tbl_ref[pl.ds(row, SUBLANE), :], BF16))
    return jnp.concatenate(tiles, axis=0)


TOKENS_PER_TRIP = 64


def _for_token_groups(tb, body):
    def trip(i, carry):
        for u in range(TOKENS_PER_TRIP):
            body(TOKENS_PER_TRIP * i + u)
        return carry

    lax.fori_loop(0, tb // TOKENS_PER_TRIP, trip, 0)


PRODUCT_GROUPS = 3


def _peer2_kernel(pair_ref, idx_ref, x_ref, mod_ref, g_ref, gate_ref, tbl_ref, cc_ref, accp_ref, acct_ref):
    tb = accp_ref.shape[0]
    ne = idx_ref.shape[1]
    kk = ne * PAIR_ROWS
    grp = 2 * PAIR_ROWS
    n_p = PRODUCT_GROUPS * PAIR_ROWS
    n_t = ne - n_p
    kt = n_t * PAIR_ROWS
    d = x_ref.shape[1]
    a = g_ref[0] * (1.0 + mod_ref[4])
    shift = mod_ref[3]
    srow = lax.broadcasted_iota(I32, (grp, PAIR_ROWS * PAIR_ROWS), 0)
    scol = lax.broadcasted_iota(I32, (grp, PAIR_ROWS * PAIR_ROWS), 1)
    sel = (((scol // PAIR_ROWS) == (srow % PAIR_ROWS)) & ((scol % 2) == (srow // PAIR_ROWS))).astype(BF16)
    dlane = lax.broadcasted_iota(I32, (SUBLANE, kt), 1)
    dsub = lax.broadcasted_iota(I32, (SUBLANE, kt), 0)
    diag = ((dlane % PAIR_ROWS) // 2) == dsub

    def token(t):
        x = x_ref[pl.ds(t, 1), :].reshape(SUBLANE, LANE)
        ms = jnp.sum(x * x, keepdims=True) * (1.0 / d)
        h = x * lax.rsqrt(ms + EPS) * a + shift
        hu = lax.bitcast_convert_type(h.astype(BF16).astype(F32), U32)
        hp = pltpu.bitcast(hu | lax.shift_right_logical(hu, jnp.uint32(16)), BF16)
        hb = jnp.concatenate([h, jnp.zeros_like(h)], axis=0).astype(BF16)
        base = pl.multiple_of(t * ne, ne)

        def tile(j):
            row = pl.multiple_of(pair_ref[base + j], SUBLANE)
            return pltpu.bitcast(tbl_ref[pl.ds(row, SUBLANE), :], BF16)

        parts = []
        for gi in range(PRODUCT_GROUPS):
            prods = [tile(j) * hp for j in range(gi * PAIR_ROWS, (gi + 1) * PAIR_ROWS)]
            parts.append(jnp.dot(sel, jnp.concatenate(prods, axis=0), preferred_element_type=F32))
        r = jnp.concatenate(parts, axis=0)
        accp_ref[pl.ds(t, 1), :] = jnp.sum(r.T, axis=0, keepdims=True)
        tiles = jnp.concatenate([tile(j) for j in range(n_p, ne)], axis=0)
        out = lax.dot_general(hb, tiles, _NT, preferred_element_type=F32)
        acct_ref[pl.ds(t, 1), :] = jnp.sum(jnp.where(diag, out[:SUBLANE], 0.0), axis=0, keepdims=True)

    _for_token_groups(tb, token)

    atp = accp_ref[...].T
    kq = lax.broadcasted_iota(I32, (2 * n_t, kt), 1)
    rq = lax.broadcasted_iota(I32, (2 * n_t, kt), 0)
    selt = (((kq // PAIR_ROWS) == (rq % n_t)) & ((kq % 2) == (rq // n_t))).astype(F32)
    att = lax.dot_general(selt, acct_ref[...], _NT, preferred_element_type=F32, precision=HIGHEST)
    a0 = jnp.concatenate([atp[gi * grp:gi * grp + PAIR_ROWS] for gi in range(PRODUCT_GROUPS)] + [att[:n_t]], axis=0)
    a1 = jnp.concatenate([atp[gi * grp + PAIR_ROWS:(gi + 1) * grp] for gi in range(PRODUCT_GROUPS)] + [att[n_t:]],
                         axis=0)
    par = idx_ref[0] & 1
    act = jnp.where(par == 0, a0, a1)
    coef = jax.nn.gelu(act, approximate=True) * gate_ref[0]
    ke = lax.broadcasted_iota(I32, (ne, kk), 1)
    re = lax.broadcasted_iota(I32, (ne, kk), 0)
    expand = ((ke // PAIR_ROWS) == re).astype(BF16)
    cexp = jnp.dot(coef.T.astype(BF16), expand, preferred_element_type=F32)
    pexp = jnp.dot(par.astype(F32).T.astype(BF16), expand, preferred_element_type=F32)
    lanepar = (lax.broadcasted_iota(I32, (tb, kk), 1) % 2).astype(F32)
    cc_ref[...] = jnp.where(pexp == lanepar, cexp, 0.0)


def _peer3_kernel(pair_ref, cc_ref, x_ref, mod_ref, tbl_ref, o_ref):
    tb, kk = cc_ref.shape
    ne = kk // PAIR_ROWS
    ga = mod_ref[5]
    lane = lax.broadcasted_iota(I32, (PAIR_ROWS, kk), 1)
    sub = lax.broadcasted_iota(I32, (PAIR_ROWS, kk), 0)
    diag = ((lane % PAIR_ROWS) // 2) == sub

    def token(t):
        row = cc_ref[pl.ds(t, 1), :]
        cm = jnp.where(diag, jnp.broadcast_to(row, (PAIR_ROWS, kk)), 0.0).astype(BF16)
        tiles = _gather_tiles(tbl_ref, pair_ref, t, ne)
        o = jnp.dot(cm, tiles, preferred_element_type=F32)
        xrow = x_ref[pl.ds(t, 1), :].reshape(SUBLANE, LANE)
        o_ref[pl.ds(t, 1), :] = (xrow + ga * o[:SUBLANE]).reshape(1, SUBLANE * LANE)

    _for_token_groups(tb, token)


def _peer_blocks(b, t, lc):
    nj = t // PEER_TOKENS
    n_ctx = lc // PEER_TOKENS
    tokb = pl.BlockSpec((PEER_TOKENS, SUBLANE * LANE), lambda n: (n, 0))
    modb = pl.BlockSpec((SUBLANE, SUBLANE, LANE), lambda n: ((n // nj) * 2 + ((n % nj) >= n_ctx).astype(I32), 0, 0))
    return nj, tokb, modb


def _peer_act(x2, pair2, idx, gate, modv, g3, tbl, b, t, lc):
    ne = PEER_HEADS * PEER_TOPK
    kk = ne * PAIR_ROWS
    nj, tokb, modb = _peer_blocks(b, t, lc)
    expb = pl.BlockSpec((1, ne, PEER_TOKENS), lambda n: (n // nj, 0, n % nj))
    return pl.pallas_call(
        _peer2_kernel,
        grid=(b * nj,),
        in_specs=[pl.BlockSpec((PEER_TOKENS * ne,), lambda n: (n,), memory_space=pltpu.SMEM),
                  expb, tokb, modb,
                  pl.BlockSpec((1, SUBLANE, LANE), lambda n: (0, 0, 0)),
                  expb,
                  pl.BlockSpec(tbl.shape, lambda n: (0, 0), pipeline_mode=pl.Buffered(1))],
        out_specs=pl.BlockSpec((PEER_TOKENS, kk), lambda n: (n, 0)),
        out_shape=jax.ShapeDtypeStruct((b * t, kk), F32),
        scratch_shapes=[pltpu.VMEM((PEER_TOKENS, 2 * PRODUCT_GROUPS * PAIR_ROWS), F32),
                        pltpu.VMEM((PEER_TOKENS, (ne - PRODUCT_GROUPS * PAIR_ROWS) * PAIR_ROWS), F32)],
        compiler_params=_cp(("arbitrary",), 56),
        name="peer_act",
    )(pair2, idx, x2, modv, g3, gate, tbl)


def _peer_out(x2, pair2, cc, modv, tbl, b, t, lc):
    ne = PEER_HEADS * PEER_TOPK
    kk = ne * PAIR_ROWS
    nj, tokb, modb = _peer_blocks(b, t, lc)
    return pl.pallas_call(
        _peer3_kernel,
        grid=(b * nj,),
        in_specs=[pl.BlockSpec((PEER_TOKENS * ne,), lambda n: (n,), memory_space=pltpu.SMEM),
                  pl.BlockSpec((PEER_TOKENS, kk), lambda n: (n, 0)),
                  tokb, modb,
                  pl.BlockSpec(tbl.shape, lambda n: (0, 0), pipeline_mode=pl.Buffered(1))],
        out_specs=tokb,
        out_shape=jax.ShapeDtypeStruct(x2.shape, F32),
        compiler_params=_cp(("arbitrary",), 56),
        name="peer_out",
    )(pair2, cc, x2, modv, tbl)


def _pack_experts(w):
    e, d = w.shape
    assert d == VREG_ELEMS
    bits = lax.bitcast_convert_type(w.astype(BF16), jnp.uint16).astype(U32).reshape(e // 2, 2, SUBLANE, LANE)
    return (bits[:, 0] | (bits[:, 1] << 16)).reshape(e // 2 * SUBLANE, LANE)


def _peer(xa, modseg, modv, g2, w_query, sub_keys, expert_u, expert_v, lc):
    b, t, d = xa.shape
    idx, pair, gate = _peer_select(xa, modseg, g2.reshape(1, d), w_query.astype(BF16), sub_keys.astype(BF16), lc)
    pair2 = pair.reshape(b * t * idx.shape[1])
    x2 = xa.reshape(b * t, d)
    cc = _peer_act(x2, pair2, idx, gate, modv, g2.reshape(1, SUBLANE, LANE), _pack_experts(expert_u), b, t, lc)
    x2 = _peer_out(x2, pair2, cc, modv, _pack_experts(expert_v), b, t, lc)
    return x2.reshape(b, t, d)


def _final_kernel(x_ref, g_ref, o_ref):
    x = x_ref[0]
    ms = jnp.mean(x * x, axis=-1, keepdims=True)
    o_ref[0] = x * lax.rsqrt(ms + EPS) * g_ref[...]


def _final_norm(xa, g, lc):
    b, t, d = xa.shape
    off = lc // ROW_TILE
    return pl.pallas_call(
        _final_kernel,
        grid=(b, (t - lc) // ROW_TILE),
        in_specs=[pl.BlockSpec((1, ROW_TILE, d), lambda i, j: (i, j + off, 0)),
                  pl.BlockSpec((1, d), lambda i, j: (0, 0))],
        out_specs=pl.BlockSpec((1, ROW_TILE, d), lambda i, j: (i, j, 0)),
        out_shape=jax.ShapeDtypeStruct((b, t - lc, d), F32),
        compiler_params=_cp(("parallel", "arbitrary"), 32),
        name="final_norm",
    )(xa, g)


def _rope_tables(l, lc):
    t = jnp.arange(l, dtype=I32)
    quarter = GLA_DK // 4
    freqs = ROPE_BASE ** (-jnp.arange(quarter, dtype=F32) / quarter)
    ang_r = (t // GRID_W).astype(F32)[:, None] * freqs
    ang_c = (t % GRID_W).astype(F32)[:, None] * freqs
    cos = jnp.concatenate([jnp.cos(ang_r)] * 2 + [jnp.cos(ang_c)] * 2, axis=-1)
    sin = jnp.concatenate([-jnp.sin(ang_r), jnp.sin(ang_r), -jnp.sin(ang_c), jnp.sin(ang_c)], axis=-1)
    cos = jnp.tile(cos, (1, GLA_HEADS))
    sin = jnp.tile(sin, (1, GLA_HEADS))
    qs = GLA_DK ** -0.5
    cs = jnp.concatenate([cos * qs, cos], axis=-1)
    sn = jnp.concatenate([sin * qs, sin], axis=-1)
    cs_ctx = jnp.concatenate([jnp.full((lc, GLA_KW), qs, F32), jnp.ones((lc, GLA_KW), F32)], axis=-1)
    return (jnp.concatenate([cs_ctx, cs], axis=0),
            jnp.concatenate([jnp.zeros((lc, 2 * GLA_KW), F32), sn], axis=0))


def _permute_w_in(w, d):
    sizes = (GLA_KW, GLA_KW, GLA_VW, 2 * GATE_RANK, GLA_VW, NA_W, NA_W, NA_W, 2 * d)
    offs = [0]
    for s in sizes:
        offs.append(offs[-1] + s)
    q, k, v, z, r, nq, nk, nv, gates = [w[:, offs[i]:offs[i + 1]] for i in range(len(sizes))]
    zpad = jnp.pad(z, ((0, 0), (0, LANE - 2 * GATE_RANK)))
    return jnp.concatenate([q, k, v, r, nq, nk, nv, gates, zpad], axis=1).astype(BF16)


def _gate_weights(w_a, b_a, first_row):
    wa = jnp.zeros((LANE, GLA_KW), F32).at[first_row:first_row + GATE_RANK].set(w_a)
    return wa, b_a.reshape(1, GLA_KW)


def kernel(x, c, ctx, c_ctx, w_mod, b_mod, g_norm1, w_in, w_alpha_f, b_alpha_f, w_alpha_b, b_alpha_b, g_gla, rpb,
           w_proj_gla, w_proj_na, w_out, g_norm2, w_query, sub_keys, expert_u, expert_v, g_final):
    b, l, d = x.shape
    lc = ctx.shape[1]
    depth = w_mod.shape[0]
    t = lc + l
    assert d == VREG_ELEMS and lc % ROW_TILE == 0 and l % ROW_TILE == 0 and l % GRID_W == 0
    n_ctx_tiles = lc // ROW_TILE

    rows = -(-(b + 1) // SUBLANE) * SUBLANE
    cin = jnp.zeros((rows, d), F32).at[:b].set(c).at[b].set(c_ctx)
    mod_all = _modulation(cin, w_mod, b_mod)

    cs, sn = _rope_tables(l, lc)
    xa = jnp.concatenate([ctx, x], axis=1)

    for i in range(depth):
        mod = mod_all[i].reshape(rows, N_MOD, d)
        modseg = jnp.stack([jnp.broadcast_to(mod[b], (b, N_MOD, d)), mod[:b]], axis=1)
        modseg = jnp.pad(modseg, ((0, 0), (0, 0), (0, SUBLANE - N_MOD), (0, 0)))
        modv = modseg.reshape(b * 2 * SUBLANE, SUBLANE, LANE)

        qk, v, r, nq, nk, nv, gates, z = _project(xa, modseg, g_norm1[i].reshape(1, d),
                                                  _permute_w_in(w_in[i], d), n_ctx_tiles)
        waf, baf = _gate_weights(w_alpha_f[i], b_alpha_f[i], 0)
        wab, bab = _gate_weights(w_alpha_b[i], b_alpha_b[i], GATE_RANK)
        o_f = _gla(qk, v, z, cs, sn, waf, baf, lc // GLA_BLOCK, reverse=False)
        o_b = _gla(qk, v, z, cs, sn, wab, bab, lc // GLA_BLOCK, reverse=True)
        na = _na(nq, nk, nv, _na_bias_tables(rpb[i]), lc)
        xa = _merge(xa, o_f, o_b, r, na, gates, modseg, g_gla[i].reshape(1, GLA_DV),
                    w_proj_gla[i].astype(BF16), w_proj_na[i].astype(BF16), w_out[i].astype(BF16), n_ctx_tiles)

        xa = _peer(xa, modseg, modv, g_norm2[i], w_query[i], sub_keys[i], expert_u[i], expert_v[i], lc)

    return _final_norm(xa, g_final.reshape(1, d), lc)
```

```python
import functools

import jax
import jax.numpy as jnp
from jax import lax
from jax.experimental import pallas as pl
from jax.experimental.pallas import tpu as pltpu

F32 = jnp.float32
BF16 = jnp.bfloat16
I32 = jnp.int32
U32 = jnp.uint32
HIGHEST = lax.Precision.HIGHEST

GRID_W = 64
EPS = 1e-6
N_MOD = 6
GLA_HEADS = 4
GLA_DK = 64
GLA_DV = 128
GLA_KW = GLA_HEADS * GLA_DK
GLA_VW = GLA_HEADS * GLA_DV
GATE_RANK = 16
GATE_TAU = 16.0
CHUNK = 64
ROPE_BASE = 10000.0
NA_HEADS = 8
NA_DH = 64
NA_W = NA_HEADS * NA_DH
NA_KH = 8
NA_KW = 16
PEER_HEADS = 8
N_KEYS = 128
PEER_DK = 256
PEER_TOPK = 16

LANE = 128
SUBLANE = 8
VREG_ELEMS = LANE * SUBLANE

ROW_TILE = 256
GLA_BLOCK = 256
SELECT_TILE = 256
PEER_TOKENS = 128
MASK_NEG = -1e30

_NT = (((1,), (1,)), ((), ()))


def _cp(sem, vmem_mb):
    return pltpu.CompilerParams(dimension_semantics=sem, vmem_limit_bytes=vmem_mb * 1024 * 1024)


def _norm_mod(x, g, shift, scale):
    ms = jnp.mean(x * x, axis=-1, keepdims=True)
    return (x * lax.rsqrt(ms + EPS) * g) * (1.0 + scale) + shift


def _mod_kernel(c_ref, w_ref, b_ref, o_ref):
    a = c_ref[...]
    a = a * jax.nn.sigmoid(a)
    o_ref[0] = jnp.dot(a, w_ref[0], preferred_element_type=F32, precision=HIGHEST) + b_ref[0]


def _modulation(cin, w_mod, b_mod):
    depth, d, n = w_mod.shape
    rows = cin.shape[0]
    tn = n // 4
    return pl.pallas_call(
        _mod_kernel,
        grid=(depth, n // tn),
        in_specs=[pl.BlockSpec((rows, d), lambda l, j: (0, 0)),
                  pl.BlockSpec((1, d, tn), lambda l, j: (l, 0, j)),
                  pl.BlockSpec((1, 1, tn), lambda l, j: (l, 0, j))],
        out_specs=pl.BlockSpec((1, rows, tn), lambda l, j: (l, 0, j)),
        out_shape=jax.ShapeDtypeStruct((depth, rows, n), F32),
        compiler_params=_cp(("arbitrary", "arbitrary"), 40),
        name="modulation",
    )(cin, w_mod, b_mod.reshape(depth, 1, n))


_PROJ_OUTS = (("qk", 2 * GLA_KW, F32), ("v", GLA_VW, F32), ("r", GLA_VW, F32),
              ("nq", NA_W, BF16), ("nk", NA_W, BF16), ("nv", NA_W, BF16),
              ("gates", None, F32), ("z", LANE, F32))


def _proj_kernel(x_ref, mod_ref, g_ref, w_ref, *out_refs):
    m = mod_ref[0, 0]
    h = _norm_mod(x_ref[0], g_ref[...], m[0:1], m[1:2]).astype(BF16)
    col = 0
    for ref in out_refs:
        n = ref.shape[-1]
        ref[0] = jnp.dot(h, w_ref[:, col:col + n], preferred_element_type=F32).astype(ref.dtype)
        col += n


def _project(xa, modseg, g, w_perm, n_ctx_tiles):
    b, t, d = xa.shape
    outs = [(name, (2 * d if n is None else n), dt) for name, n, dt in _PROJ_OUTS]
    assert sum(n for _, n, _ in outs) == w_perm.shape[1]
    return pl.pallas_call(
        _proj_kernel,
        grid=(b, t // ROW_TILE),
        in_specs=[pl.BlockSpec((1, ROW_TILE, d), lambda i, j: (i, j, 0)),
                  pl.BlockSpec((1, 1, SUBLANE, d), lambda i, j: (i, (j >= n_ctx_tiles).astype(I32), 0, 0)),
                  pl.BlockSpec((1, d), lambda i, j: (0, 0)),
                  pl.BlockSpec(w_perm.shape, lambda i, j: (0, 0))],
        out_specs=[pl.BlockSpec((1, ROW_TILE, n), lambda i, j: (i, j, 0)) for _, n, _ in outs],
        out_shape=[jax.ShapeDtypeStruct((b, t, n), dt) for _, n, dt in outs],
        compiler_params=_cp(("parallel", "arbitrary"), 48),
        name="in_proj",
    )(xa, modseg, g, w_perm)


def _gla_kernel(qk_ref, v_ref, z_ref, cs_ref, sn_ref, wa_ref, ba_ref, o_ref, st_ref, *, reverse):
    @pl.when(pl.program_id(1) == 0)
    def _init():
        st_ref[...] = jnp.zeros_like(st_ref)

    kw2 = 2 * GLA_KW
    blk = qk_ref.shape[1]
    qk = qk_ref[0]
    lane = lax.broadcasted_iota(I32, (1, kw2), 1)
    first = (lane % 32) < 16
    partner = jnp.where(first, pltpu.roll(qk, kw2 - 16, 1), pltpu.roll(qk, 16, 1))
    qk = qk * cs_ref[...] + partner * sn_ref[...]
    zz = jnp.dot(z_ref[0], wa_ref[...], preferred_element_type=F32) + ba_ref[...]
    la = (jnp.minimum(zz, 0.0) - jnp.log1p(jnp.exp(-jnp.abs(zz)))) * (1.0 / GATE_TAU)
    la_hi = la.astype(BF16)
    rem = la - la_hi.astype(F32)
    la_mid = rem.astype(BF16)
    la_lo = (rem - la_mid.astype(F32)).astype(BF16)

    ri = lax.broadcasted_iota(I32, (CHUNK, CHUNK), 0)
    ci = lax.broadcasted_iota(I32, (CHUNK, CHUNK), 1)
    keep = (ci >= ri) if reverse else (ci <= ri)
    tri3 = jnp.concatenate([keep.astype(BF16)] * 3, axis=1)
    khead = lax.broadcasted_iota(I32, (1, GLA_KW), 1) // GLA_DK
    vrow = lax.broadcasted_iota(I32, (GLA_VW, GLA_KW), 0) // GLA_DV
    kcol = lax.broadcasted_iota(I32, (GLA_VW, GLA_KW), 1) // GLA_DK
    same_head = vrow == kcol

    order = range(blk // CHUNK)
    if reverse:
        order = reversed(order)
    for c in order:
        lo = c * CHUNK
        q = qk[lo:lo + CHUNK, :GLA_KW]
        k = qk[lo:lo + CHUNK, GLA_KW:]
        v = v_ref[0, lo:lo + CHUNK, :].astype(BF16)
        pieces = jnp.concatenate([p[lo:lo + CHUNK] for p in (la_hi, la_mid, la_lo)], axis=0)
        bcum = jnp.dot(tri3, pieces, preferred_element_type=F32)
        bend = bcum[0:1] if reverse else bcum[CHUNK - 1:CHUNK]
        qd = q * jnp.exp(bcum)
        kd = (k * jnp.exp(-bcum)).astype(BF16)
        kd2 = (k * jnp.exp(bend - bcum)).astype(BF16)
        st = st_ref[...]
        o_inter = lax.dot_general(qd.astype(BF16), st.astype(BF16), _NT, preferred_element_type=F32)
        for h in range(GLA_HEADS):
            qm = jnp.where(khead == h, qd, 0.0).astype(BF16)
            s = lax.dot_general(qm, kd, _NT, preferred_element_type=F32)
            s = jnp.where(keep, s, 0.0).astype(BF16)
            vs = slice(h * GLA_DV, (h + 1) * GLA_DV)
            o_ref[0, lo:lo + CHUNK, vs] = (jnp.dot(s, v[:, vs], preferred_element_type=F32)
                                           + o_inter[:, vs])
        upd = jnp.dot(v.T, kd2, preferred_element_type=F32)
        st_ref[...] = st * jnp.exp(bend) + jnp.where(same_head, upd, 0.0)


def _gla(qk, v, z, cs, sn, wa, ba, n_ctx_blocks, reverse):
    b, t, _ = qk.shape
    nblk = t // GLA_BLOCK

    if reverse:
        def blk(j):
            return jnp.where(j < n_ctx_blocks, n_ctx_blocks - 1 - j, nblk - 1 - (j - n_ctx_blocks))
    else:
        def blk(j):
            return j

    tok = lambda n: pl.BlockSpec((1, GLA_BLOCK, n), lambda i, j: (i, blk(j), 0))
    tab = lambda n: pl.BlockSpec((GLA_BLOCK, n), lambda i, j: (blk(j), 0))
    return pl.pallas_call(
        functools.partial(_gla_kernel, reverse=reverse),
        grid=(b, nblk),
        in_specs=[tok(2 * GLA_KW), tok(GLA_VW), tok(LANE), tab(2 * GLA_KW), tab(2 * GLA_KW),
                  pl.BlockSpec((LANE, GLA_KW), lambda i, j: (0, 0)),
                  pl.BlockSpec((1, GLA_KW), lambda i, j: (0, 0))],
        out_specs=tok(GLA_VW),
        out_shape=jax.ShapeDtypeStruct((b, t, GLA_VW), F32),
        scratch_shapes=[pltpu.VMEM((GLA_VW, GLA_KW), F32)],
        compiler_params=_cp(("parallel", "arbitrary"), 40),
        name="gla_bwd" if reverse else "gla_fwd",
    )(qk, v, z, cs, sn, wa, ba)


def _na_kernel(*refs):
    q_ref = refs[0]
    k_refs = refs[1:1 + NA_KH]
    v_refs = refs[1 + NA_KH:1 + 2 * NA_KH]
    kc_ref, vc_ref, tb_ref, o_ref, s_ref, p_ref = refs[1 + 2 * NA_KH:]
    scale = NA_DH ** -0.5
    nw = NA_KH * GRID_W
    half = lax.broadcasted_iota(I32, (1, LANE), 1) // NA_DH
    for p in range(NA_HEADS // 2):
        ls = slice(p * LANE, (p + 1) * LANE)
        qp = q_ref[0, :, ls]
        kw = jnp.concatenate([r[0, :, ls] for r in k_refs], axis=0)
        kc = kc_ref[0, :, ls]
        for s in range(2):
            qm = jnp.where(half == s, qp, jnp.zeros_like(qp))
            s_ref[2 * p + s, :, :nw] = (lax.dot_general(qm, kw, _NT, preferred_element_type=F32) * scale
                                        + tb_ref[0, 2 * p + s])
            s_ref[2 * p + s, :, nw:] = lax.dot_general(qm, kc, _NT, preferred_element_type=F32) * scale
    for h in range(NA_HEADS):
        sc = s_ref[h]
        e = jnp.exp(sc - jnp.max(sc, axis=-1, keepdims=True))
        p_ref[h] = (e * (1.0 / jnp.sum(e, axis=-1, keepdims=True))).astype(BF16)
    for p in range(NA_HEADS // 2):
        ls = slice(p * LANE, (p + 1) * LANE)
        vw = jnp.concatenate([r[0, :, ls] for r in v_refs], axis=0)
        vc = vc_ref[0, :, ls]
        outs = [jnp.dot(p_ref[2 * p + s, :, :nw], vw, preferred_element_type=F32)
                + jnp.dot(p_ref[2 * p + s, :, nw:], vc, preferred_element_type=F32) for s in range(2)]
        o_ref[0, :, ls] = jnp.where(half == 0, outs[0], outs[1]).astype(o_ref.dtype)


def _na_bias_tables(rpb):
    w = jnp.arange(GRID_W, dtype=I32)
    c0 = jnp.clip(w - NA_KW // 2, 0, GRID_W - NA_KW)
    cp = w[None, :]
    valid = (cp >= c0[:, None]) & (cp < c0[:, None] + NA_KW)
    off = jnp.clip(cp - w[:, None] + (NA_KW - 1), 0, 2 * NA_KW - 2)
    t = jnp.where(valid[None, None], rpb[:, :, off], MASK_NEG)
    tabs = [jnp.concatenate([t[:, s + a] for a in range(NA_KH)], axis=-1) for s in range(NA_KH)]
    tabs.append(jnp.full_like(tabs[0], MASK_NEG))
    return jnp.stack(tabs, axis=0).astype(F32)


def _na(nq, nk, nv, tables, lc):
    b, t, _ = nq.shape
    n_ctx = lc // GRID_W
    rows = (t - lc) // GRID_W
    assert rows >= NA_KH

    def row0(j):
        r = jnp.maximum(j - n_ctx, 0)
        return r, jnp.clip(r - NA_KH // 2, 0, rows - NA_KH)

    def win(a):
        return pl.BlockSpec((1, GRID_W, NA_W), lambda i, j: (i, n_ctx + row0(j)[1] + a, 0))

    def tab_idx(i, j):
        r, r0 = row0(j)
        return (jnp.where(j < n_ctx, NA_KH, r0 - r + NA_KH - 1), 0, 0, 0)

    return pl.pallas_call(
        _na_kernel,
        grid=(b, t // GRID_W),
        in_specs=([pl.BlockSpec((1, GRID_W, NA_W), lambda i, j: (i, j, 0))]
                  + [win(a) for a in range(NA_KH)] + [win(a) for a in range(NA_KH)]
                  + [pl.BlockSpec((1, lc, NA_W), lambda i, j: (i, 0, 0))] * 2
                  + [pl.BlockSpec((1, NA_HEADS, GRID_W, NA_KH * GRID_W), tab_idx)]),
        out_specs=pl.BlockSpec((1, GRID_W, NA_W), lambda i, j: (i, j, 0)),
        out_shape=jax.ShapeDtypeStruct((b, t, NA_W), BF16),
        scratch_shapes=[pltpu.VMEM((NA_HEADS, GRID_W, NA_KH * GRID_W + lc), F32),
                        pltpu.VMEM((NA_HEADS, GRID_W, NA_KH * GRID_W + lc), BF16)],
        compiler_params=_cp(("parallel", "arbitrary"), 40),
        name="natten",
    )(nq, *([nk] * NA_KH), *([nv] * NA_KH), nk, nv, tables)


def _merge_kernel(x_ref, of_ref, ob_ref, r_ref, na_ref, gt_ref, mod_ref, gg_ref, wpg_ref, wpn_ref, wo_ref,
                  o_ref):
    d = x_ref.shape[-1]
    o = of_ref[0] + ob_ref[0]
    r = r_ref[0]
    parts = []
    for h in range(GLA_HEADS):
        oh = o[:, h * GLA_DV:(h + 1) * GLA_DV]
        ms = jnp.mean(oh * oh, axis=-1, keepdims=True)
        parts.append(oh * lax.rsqrt(ms + EPS) * gg_ref[...])
    on = jnp.concatenate(parts, axis=-1) * (r * jax.nn.sigmoid(r))
    ya = jnp.dot(on.astype(BF16), wpg_ref[...], preferred_element_type=F32)
    yb = jnp.dot(na_ref[0], wpn_ref[...], preferred_element_type=F32)
    gt = gt_ref[0]
    y = jax.nn.sigmoid(gt[:, :d]) * ya + jax.nn.sigmoid(gt[:, d:]) * yb
    y = jnp.dot(y.astype(BF16), wo_ref[...], preferred_element_type=F32)
    o_ref[0] = x_ref[0] + mod_ref[0, 0][2:3] * y


def _merge(xa, o_f, o_b, r, na, gates, modseg, g_gla, wpg, wpn, wo, n_ctx_tiles):
    b, t, d = xa.shape
    tok = lambda n: pl.BlockSpec((1, ROW_TILE, n), lambda i, j: (i, j, 0))
    full = lambda a: pl.BlockSpec(a.shape, lambda i, j: (0,) * a.ndim)
    return pl.pallas_call(
        _merge_kernel,
        grid=(b, t // ROW_TILE),
        in_specs=[tok(d), tok(GLA_VW), tok(GLA_VW), tok(GLA_VW), tok(NA_W), tok(2 * d),
                  pl.BlockSpec((1, 1, SUBLANE, d), lambda i, j: (i, (j >= n_ctx_tiles).astype(I32), 0, 0)),
                  full(g_gla), full(wpg), full(wpn), full(wo)],
        out_specs=tok(d),
        out_shape=jax.ShapeDtypeStruct((b, t, d), F32),
        compiler_params=_cp(("parallel", "arbitrary"), 40),
        name="merge",
    )(xa, o_f, o_b, r, na, gates, modseg, g_gla, wpg, wpn, wo)


def _topk_rows(s, ids, k):
    rows = s.shape[0]
    rid = lax.broadcasted_iota(I32, s.shape, 0).astype(F32)
    vals, out_ids = [], []
    for _ in range(k):
        m = jnp.max(s, axis=0, keepdims=True)
        pos = jnp.min(jnp.where(s == m, rid, float(rows)), axis=0, keepdims=True)
        hit = rid == pos
        vals.append(m)
        out_ids.append(pos if ids is None else jnp.max(jnp.where(hit, ids, -1.0), axis=0, keepdims=True))
        s = jnp.where(hit, -jnp.inf, s)
    return jnp.concatenate(vals, axis=0), jnp.concatenate(out_ids, axis=0)


def _staircase_candidates(v0, i0, v1, i1):
    k = PEER_TOPK
    sub = lax.broadcasted_iota(I32, (SUBLANE, v0.shape[1]), 0)
    vals = [v0[0:1] + v1]
    ids = [i0[0:1] * float(N_KEYS) + i1]
    for a in range(1, k // 2):
        nb = k // (a + 1)
        live = sub < nb
        vals.append(jnp.where(live, v0[a:a + 1] + v1[:SUBLANE], -jnp.inf))
        ids.append(i0[a:a + 1] * float(N_KEYS) + i1[:SUBLANE])
    vals.append(v0[k // 2:] + v1[0:1])
    ids.append(i0[k // 2:] * float(N_KEYS) + i1[0:1])
    return jnp.concatenate(vals, axis=0), jnp.concatenate(ids, axis=0)


def _peer1_kernel(x_ref, mod_ref, g_ref, wq_ref, sk_ref, idx_ref, pair_ref, gate_ref):
    m = mod_ref[0, 0]
    h = _norm_mod(x_ref[0], g_ref[...], m[3:4], m[4:5]).astype(BF16)
    half = PEER_DK // 2
    chosen = []
    for hd in range(PEER_HEADS):
        tops = []
        for p in range(2):
            col = (2 * hd + p) * half
            qp = jnp.dot(h, wq_ref[:, col:col + half], preferred_element_type=F32).astype(BF16)
            st = lax.dot_general(sk_ref[hd, p], qp, _NT, preferred_element_type=F32)
            tops.append(_topk_rows(st, None, PEER_TOPK))
        (v0, i0), (v1, i1) = tops
        cand, cidx = _staircase_candidates(v0, i0, v1, i1)
        best, experts = _topk_rows(cand, cidx, PEER_TOPK)
        e = jnp.exp(best - best[0:1])
        rs = slice(hd * PEER_TOPK, (hd + 1) * PEER_TOPK)
        gate_ref[0, rs, :] = e / jnp.sum(e, axis=0, keepdims=True)
        idx_ref[0, rs, :] = experts.astype(I32)
        chosen.append(experts)
    pair_ref[0] = lax.shift_right_logical(jnp.concatenate(chosen, axis=0).T.astype(I32), 1) * SUBLANE


def _peer_select(xa, modseg, g, wq, sk, lc):
    b, t, d = xa.shape
    ne = PEER_HEADS * PEER_TOPK
    tm = SELECT_TILE
    n_ctx_tiles = lc // tm
    return pl.pallas_call(
        _peer1_kernel,
        grid=(b, t // tm),
        in_specs=[pl.BlockSpec((1, tm, d), lambda i, j: (i, j, 0)),
                  pl.BlockSpec((1, 1, SUBLANE, d), lambda i, j: (i, (j >= n_ctx_tiles).astype(I32), 0, 0)),
                  pl.BlockSpec((1, d), lambda i, j: (0, 0)),
                  pl.BlockSpec(wq.shape, lambda i, j: (0, 0)),
                  pl.BlockSpec(sk.shape, lambda i, j: (0, 0, 0, 0))],
        out_specs=[pl.BlockSpec((1, ne, tm), lambda i, j: (i, 0, j)),
                   pl.BlockSpec((1, tm, ne), lambda i, j: (i, j, 0)),
                   pl.BlockSpec((1, ne, tm), lambda i, j: (i, 0, j))],
        out_shape=[jax.ShapeDtypeStruct((b, ne, t), I32), jax.ShapeDtypeStruct((b, t, ne), I32),
                   jax.ShapeDtypeStruct((b, ne, t), F32)],
        compiler_params=_cp(("parallel", "arbitrary"), 40),
        name="peer_select",
    )(xa, modseg, g, wq, sk)


PAIR_ROWS = 2 * SUBLANE


def _gather_tiles(tbl_ref, pair_ref, t, ne):
    base = pl.multiple_of(t * ne, ne)
    tiles = []
    for j in range(ne):
        row = pl.multiple_of(pair_ref[base + j], SUBLANE)
        tiles.append(pltpu.bitcast(tbl_ref[pl.ds(row, SUBLANE), :], BF16))
    return jnp.concatenate(tiles, axis=0)


TOKENS_PER_TRIP = 128


def _for_token_groups(tb, body):
    def trip(i, carry):
        for u in range(TOKENS_PER_TRIP):
            body(TOKENS_PER_TRIP * i + u)
        return carry

    lax.fori_loop(0, tb // TOKENS_PER_TRIP, trip, 0)


PRODUCT_GROUPS = 3


def _peer2_kernel(pair_ref, idx_ref, x_ref, mod_ref, g_ref, gate_ref, tbl_ref, cc_ref, accp_ref, acct_ref):
    tb = accp_ref.shape[0]
    ne = idx_ref.shape[1]
    kk = ne * PAIR_ROWS
    grp = 2 * PAIR_ROWS
    n_p = PRODUCT_GROUPS * PAIR_ROWS
    n_t = ne - n_p
    kt = n_t * PAIR_ROWS
    d = x_ref.shape[1]
    a = g_ref[0] * (1.0 + mod_ref[4])
    shift = mod_ref[3]
    srow = lax.broadcasted_iota(I32, (grp, PAIR_ROWS * PAIR_ROWS), 0)
    scol = lax.broadcasted_iota(I32, (grp, PAIR_ROWS * PAIR_ROWS), 1)
    sel = (((scol // PAIR_ROWS) == (srow % PAIR_ROWS)) & ((scol % 2) == (srow // PAIR_ROWS))).astype(BF16)
    dlane = lax.broadcasted_iota(I32, (SUBLANE, kt), 1)
    dsub = lax.broadcasted_iota(I32, (SUBLANE, kt), 0)
    diag = ((dlane % PAIR_ROWS) // 2) == dsub

    def token(t):
        x = x_ref[pl.ds(t, 1), :].reshape(SUBLANE, LANE)
        ms = jnp.sum(x * x, keepdims=True) * (1.0 / d)
        h = x * lax.rsqrt(ms + EPS) * a + shift
        hu = lax.bitcast_convert_type(h.astype(BF16).astype(F32), U32)
        hp = pltpu.bitcast(hu | lax.shift_right_logical(hu, jnp.uint32(16)), BF16)
        hb = jnp.concatenate([h, jnp.zeros_like(h)], axis=0).astype(BF16)
        base = pl.multiple_of(t * ne, ne)

        def tile(j):
            row = pl.multiple_of(pair_ref[base + j], SUBLANE)
            return pltpu.bitcast(tbl_ref[pl.ds(row, SUBLANE), :], BF16)

        parts = []
        for gi in range(PRODUCT_GROUPS):
            prods = [tile(j) * hp for j in range(gi * PAIR_ROWS, (gi + 1) * PAIR_ROWS)]
            parts.append(jnp.dot(sel, jnp.concatenate(prods, axis=0), preferred_element_type=F32))
        r = jnp.concatenate(parts, axis=0)
        accp_ref[pl.ds(t, 1), :] = jnp.sum(r.T, axis=0, keepdims=True)
        tiles = jnp.concatenate([tile(j) for j in range(n_p, ne)], axis=0)
        out = lax.dot_general(hb, tiles, _NT, preferred_element_type=F32)
        acct_ref[pl.ds(t, 1), :] = jnp.sum(jnp.where(diag, out[:SUBLANE], 0.0), axis=0, keepdims=True)

    _for_token_groups(tb, token)

    atp = accp_ref[...].T
    kq = lax.broadcasted_iota(I32, (2 * n_t, kt), 1)
    rq = lax.broadcasted_iota(I32, (2 * n_t, kt), 0)
    selt = (((kq // PAIR_ROWS) == (rq % n_t)) & ((kq % 2) == (rq // n_t))).astype(F32)
    att = lax.dot_general(selt, acct_ref[...], _NT, preferred_element_type=F32, precision=HIGHEST)
    a0 = jnp.concatenate([atp[gi * grp:gi * grp + PAIR_ROWS] for gi in range(PRODUCT_GROUPS)] + [att[:n_t]], axis=0)
    a1 = jnp.concatenate([atp[gi * grp + PAIR_ROWS:(gi + 1) * grp] for gi in range(PRODUCT_GROUPS)] + [att[n_t:]],
                         axis=0)
    par = idx_ref[0] & 1
    act = jnp.where(par == 0, a0, a1)
    coef = jax.nn.gelu(act, approximate=True) * gate_ref[0]
    ke = lax.broadcasted_iota(I32, (ne, kk), 1)
    re = lax.broadcasted_iota(I32, (ne, kk), 0)
    expand = ((ke // PAIR_ROWS) == re).astype(BF16)
    cexp = jnp.dot(coef.T.astype(BF16), expand, preferred_element_type=F32)
    pexp = jnp.dot(par.astype(F32).T.astype(BF16), expand, preferred_element_type=F32)
    lanepar = (lax.broadcasted_iota(I32, (tb, kk), 1) % 2).astype(F32)
    cc_ref[...] = jnp.where(pexp == lanepar, cexp, 0.0)


def _peer3_kernel(pair_ref, cc_ref, x_ref, mod_ref, tbl_ref, o_ref):
    tb, kk = cc_ref.shape
    ne = kk // PAIR_ROWS
    ga = mod_ref[5]
    lane = lax.broadcasted_iota(I32, (PAIR_ROWS, kk), 1)
    sub = lax.broadcasted_iota(I32, (PAIR_ROWS, kk), 0)
    diag = ((lane % PAIR_ROWS) // 2) == sub

    def token(t):
        row = cc_ref[pl.ds(t, 1), :]
        cm = jnp.where(diag, jnp.broadcast_to(row, (PAIR_ROWS, kk)), 0.0).astype(BF16)
        tiles = _gather_tiles(tbl_ref, pair_ref, t, ne)
        o = jnp.dot(cm, tiles, preferred_element_type=F32)
        xrow = x_ref[pl.ds(t, 1), :].reshape(SUBLANE, LANE)
        o_ref[pl.ds(t, 1), :] = (xrow + ga * o[:SUBLANE]).reshape(1, SUBLANE * LANE)

    _for_token_groups(tb, token)


def _peer_blocks(b, t, lc):
    nj = t // PEER_TOKENS
    n_ctx = lc // PEER_TOKENS
    tokb = pl.BlockSpec((PEER_TOKENS, SUBLANE * LANE), lambda n: (n, 0))
    modb = pl.BlockSpec((SUBLANE, SUBLANE, LANE), lambda n: ((n // nj) * 2 + ((n % nj) >= n_ctx).astype(I32), 0, 0))
    return nj, tokb, modb


def _peer_act(x2, pair2, idx, gate, modv, g3, tbl, b, t, lc):
    ne = PEER_HEADS * PEER_TOPK
    kk = ne * PAIR_ROWS
    nj, tokb, modb = _peer_blocks(b, t, lc)
    expb = pl.BlockSpec((1, ne, PEER_TOKENS), lambda n: (n // nj, 0, n % nj))
    return pl.pallas_call(
        _peer2_kernel,
        grid=(b * nj,),
        in_specs=[pl.BlockSpec((PEER_TOKENS * ne,), lambda n: (n,), memory_space=pltpu.SMEM),
                  expb, tokb, modb,
                  pl.BlockSpec((1, SUBLANE, LANE), lambda n: (0, 0, 0)),
                  expb,
                  pl.BlockSpec(tbl.shape, lambda n: (0, 0), pipeline_mode=pl.Buffered(1))],
        out_specs=pl.BlockSpec((PEER_TOKENS, kk), lambda n: (n, 0)),
        out_shape=jax.ShapeDtypeStruct((b * t, kk), F32),
        scratch_shapes=[pltpu.VMEM((PEER_TOKENS, 2 * PRODUCT_GROUPS * PAIR_ROWS), F32),
                        pltpu.VMEM((PEER_TOKENS, (ne - PRODUCT_GROUPS * PAIR_ROWS) * PAIR_ROWS), F32)],
        compiler_params=_cp(("arbitrary",), 56),
        name="peer_act",
    )(pair2, idx, x2, modv, g3, gate, tbl)


def _peer_out(x2, pair2, cc, modv, tbl, b, t, lc):
    ne = PEER_HEADS * PEER_TOPK
    kk = ne * PAIR_ROWS
    nj, tokb, modb = _peer_blocks(b, t, lc)
    return pl.pallas_call(
        _peer3_kernel,
        grid=(b * nj,),
        in_specs=[pl.BlockSpec((PEER_TOKENS * ne,), lambda n: (n,), memory_space=pltpu.SMEM),
                  pl.BlockSpec((PEER_TOKENS, kk), lambda n: (n, 0)),
                  tokb, modb,
                  pl.BlockSpec(tbl.shape, lambda n: (0, 0), pipeline_mode=pl.Buffered(1))],
        out_specs=tokb,
        out_shape=jax.ShapeDtypeStruct(x2.shape, F32),
        compiler_params=_cp(("arbitrary",), 56),
        name="peer_out",
    )(pair2, cc, x2, modv, tbl)


def _pack_experts(w):
    e, d = w.shape
    assert d == VREG_ELEMS
    bits = lax.bitcast_convert_type(w.astype(BF16), jnp.uint16).astype(U32).reshape(e // 2, 2, SUBLANE, LANE)
    return (bits[:, 0] | (bits[:, 1] << 16)).reshape(e // 2 * SUBLANE, LANE)


def _peer(xa, modseg, modv, g2, w_query, sub_keys, expert_u, expert_v, lc):
    b, t, d = xa.shape
    idx, pair, gate = _peer_select(xa, modseg, g2.reshape(1, d), w_query.astype(BF16), sub_keys.astype(BF16), lc)
    pair2 = pair.reshape(b * t * idx.shape[1])
    x2 = xa.reshape(b * t, d)
    cc = _peer_act(x2, pair2, idx, gate, modv, g2.reshape(1, SUBLANE, LANE), _pack_experts(expert_u), b, t, lc)
    x2 = _peer_out(x2, pair2, cc, modv, _pack_experts(expert_v), b, t, lc)
    return x2.reshape(b, t, d)


def _final_kernel(x_ref, g_ref, o_ref):
    x = x_ref[0]
    ms = jnp.mean(x * x, axis=-1, keepdims=True)
    o_ref[0] = x * lax.rsqrt(ms + EPS) * g_ref[...]


def _final_norm(xa, g, lc):
    b, t, d = xa.shape
    off = lc // ROW_TILE
    return pl.pallas_call(
        _final_kernel,
        grid=(b, (t - lc) // ROW_TILE),
        in_specs=[pl.BlockSpec((1, ROW_TILE, d), lambda i, j: (i, j + off, 0)),
                  pl.BlockSpec((1, d), lambda i, j: (0, 0))],
        out_specs=pl.BlockSpec((1, ROW_TILE, d), lambda i, j: (i, j, 0)),
        out_shape=jax.ShapeDtypeStruct((b, t - lc, d), F32),
        compiler_params=_cp(("parallel", "arbitrary"), 32),
        name="final_norm",
    )(xa, g)


def _rope_tables(l, lc):
    t = jnp.arange(l, dtype=I32)
    quarter = GLA_DK // 4
    freqs = ROPE_BASE ** (-jnp.arange(quarter, dtype=F32) / quarter)
    ang_r = (t // GRID_W).astype(F32)[:, None] * freqs
    ang_c = (t % GRID_W).astype(F32)[:, None] * freqs
    cos = jnp.concatenate([jnp.cos(ang_r)] * 2 + [jnp.cos(ang_c)] * 2, axis=-1)
    sin = jnp.concatenate([-jnp.sin(ang_r), jnp.sin(ang_r), -jnp.sin(ang_c), jnp.sin(ang_c)], axis=-1)
    cos = jnp.tile(cos, (1, GLA_HEADS))
    sin = jnp.tile(sin, (1, GLA_HEADS))
    qs = GLA_DK ** -0.5
    cs = jnp.concatenate([cos * qs, cos], axis=-1)
    sn = jnp.concatenate([sin * qs, sin], axis=-1)
    cs_ctx = jnp.concatenate([jnp.full((lc, GLA_KW), qs, F32), jnp.ones((lc, GLA_KW), F32)], axis=-1)
    return (jnp.concatenate([cs_ctx, cs], axis=0),
            jnp.concatenate([jnp.zeros((lc, 2 * GLA_KW), F32), sn], axis=0))


def _permute_w_in(w, d):
    sizes = (GLA_KW, GLA_KW, GLA_VW, 2 * GATE_RANK, GLA_VW, NA_W, NA_W, NA_W, 2 * d)
    offs = [0]
    for s in sizes:
        offs.append(offs[-1] + s)
    q, k, v, z, r, nq, nk, nv, gates = [w[:, offs[i]:offs[i + 1]] for i in range(len(sizes))]
    zpad = jnp.pad(z, ((0, 0), (0, LANE - 2 * GATE_RANK)))
    return jnp.concatenate([q, k, v, r, nq, nk, nv, gates, zpad], axis=1).astype(BF16)


def _gate_weights(w_a, b_a, first_row):
    wa = jnp.zeros((LANE, GLA_KW), F32).at[first_row:first_row + GATE_RANK].set(w_a)
    return wa, b_a.reshape(1, GLA_KW)


def kernel(x, c, ctx, c_ctx, w_mod, b_mod, g_norm1, w_in, w_alpha_f, b_alpha_f, w_alpha_b, b_alpha_b, g_gla, rpb,
           w_proj_gla, w_proj_na, w_out, g_norm2, w_query, sub_keys, expert_u, expert_v, g_final):
    b, l, d = x.shape
    lc = ctx.shape[1]
    depth = w_mod.shape[0]
    t = lc + l
    assert d == VREG_ELEMS and lc % ROW_TILE == 0 and l % ROW_TILE == 0 and l % GRID_W == 0
    n_ctx_tiles = lc // ROW_TILE

    rows = -(-(b + 1) // SUBLANE) * SUBLANE
    cin = jnp.zeros((rows, d), F32).at[:b].set(c).at[b].set(c_ctx)
    mod_all = _modulation(cin, w_mod, b_mod)

    cs, sn = _rope_tables(l, lc)
    xa = jnp.concatenate([ctx, x], axis=1)

    for i in range(depth):
        mod = mod_all[i].reshape(rows, N_MOD, d)
        modseg = jnp.stack([jnp.broadcast_to(mod[b], (b, N_MOD, d)), mod[:b]], axis=1)
        modseg = jnp.pad(modseg, ((0, 0), (0, 0), (0, SUBLANE - N_MOD), (0, 0)))
        modv = modseg.reshape(b * 2 * SUBLANE, SUBLANE, LANE)

        qk, v, r, nq, nk, nv, gates, z = _project(xa, modseg, g_norm1[i].reshape(1, d),
                                                  _permute_w_in(w_in[i], d), n_ctx_tiles)
        waf, baf = _gate_weights(w_alpha_f[i], b_alpha_f[i], 0)
        wab, bab = _gate_weights(w_alpha_b[i], b_alpha_b[i], GATE_RANK)
        o_f = _gla(qk, v, z, cs, sn, waf, baf, lc // GLA_BLOCK, reverse=False)
        o_b = _gla(qk, v, z, cs, sn, wab, bab, lc // GLA_BLOCK, reverse=True)
        na = _na(nq, nk, nv, _na_bias_tables(rpb[i]), lc)
        xa = _merge(xa, o_f, o_b, r, na, gates, modseg, g_gla[i].reshape(1, GLA_DV),
                    w_proj_gla[i].astype(BF16), w_proj_na[i].astype(BF16), w_out[i].astype(BF16), n_ctx_tiles)

        xa = _peer(xa, modseg, modv, g_norm2[i], w_query[i], sub_keys[i], expert_u[i], expert_v[i], lc)

    return _final_norm(xa, g_final.reshape(1, d), lc)
```
